```python
import jax, jax.numpy as jnp
from jax import lax
import numpy as np

D_MODEL = 1024
BATCH = 16
SEQ = 256
DEPTH = 2
DEC_BATCH = 4
DEC_SEQ = 2048
PAST_LEN = 256

GRID_W = 64
N_MIXERS = 4
GROUP_W = D_MODEL // N_MIXERS
HEAD_DIM = 64
N_HEADS = GROUP_W // HEAD_DIM
NA_WIN_H = 8
NA_WIN_W = 16
Q_BLOCK = 128
CONV_W = 31
RET_CHUNK = 128
RWKV_LORA_W = 64
RWKV_LORA_A = 64
RWKV_LORA_G = 128
RWKV_SHIFT_W = 3
D_FF = 2816
FFN_CONV_W = 3
ROPE_BASE = 10000.0
EPS = 1e-6
RWKV_GN_EPS = 64e-5
NEG_INF = -1e30
SPLIT_SIZES = (GROUP_W,) * 12 + (RWKV_LORA_W, RWKV_LORA_A, RWKV_LORA_G)
D_IN = sum(SPLIT_SIZES)
SPLIT_POINTS = tuple(int(s) for s in np.cumsum(SPLIT_SIZES)[:-1])

kernel_name = "hybrid_na_conformer_retnet_rwkv7_denoise_step"

F32 = jnp.float32


def rms_norm(x, g):
    xf = x.astype(F32)
    y = xf * lax.rsqrt(jnp.mean(xf * xf, axis=-1, keepdims=True) + EPS)
    return (y * g.astype(F32)).astype(x.dtype)


def layer_norm(x, g, b):
    xf = x.astype(F32)
    xc = xf - jnp.mean(xf, axis=-1, keepdims=True)
    return xc * lax.rsqrt(jnp.mean(xc * xc, axis=-1, keepdims=True) + EPS) * g.astype(F32) + b.astype(F32)


def head_norm(x, g, eps):
    xf = x.astype(F32)
    xc = xf - jnp.mean(xf, axis=-1, keepdims=True)
    y = xc * lax.rsqrt(jnp.mean(xc * xc, axis=-1, keepdims=True) + eps)
    return y.reshape(x.shape[0], x.shape[1], -1) * g.astype(F32)


def dwconv(x, w):
    k, ch = w.shape
    return lax.conv_general_dilated(x, w[:, None, :].astype(x.dtype), window_strides=(1,),
                                    padding=[(k // 2, k // 2)], dimension_numbers=("NWC", "WIO", "NWC"),
                                    feature_group_count=ch)


def rev(t):
    return jnp.flip(t, axis=1)


def axial_rope(length):
    t = jnp.arange(length)
    n_freq = HEAD_DIM // 4
    inv = ROPE_BASE ** (-jnp.arange(n_freq, dtype=F32) / n_freq)
    ang = jnp.concatenate([(t // GRID_W).astype(F32)[:, None] * inv,
                           (t % GRID_W).astype(F32)[:, None] * inv], axis=-1)
    return jnp.cos(ang), jnp.sin(ang)


def apply_rope(x, cos, sin):
    xf = x.astype(F32)
    x1, x2 = jnp.split(xf, 2, axis=-1)
    c, s = cos[None, :, None, :], sin[None, :, None, :]
    return jnp.concatenate([x1 * c - x2 * s, x1 * s + x2 * c], axis=-1).astype(x.dtype)


def context_attention(q, k, v):
    B, L, H, dh = q.shape
    qb = q.reshape(B, L // Q_BLOCK, Q_BLOCK, H, dh).transpose(1, 0, 2, 3, 4)

    def block(qi):
        s = jnp.einsum("bqhd,bkhd->bhqk", qi, k).astype(F32)
        p = jax.nn.softmax(s, axis=-1).astype(v.dtype)
        return jnp.einsum("bhqk,bkhd->bqhd", p, v)

    o = lax.map(block, qb)
    return o.transpose(1, 0, 2, 3, 4).reshape(B, L, H, dh)


def neighbourhood_attention(q, k, v, ctx_k, ctx_v, rpb):
    B, L, H, dh = q.shape
    rows = L // GRID_W
    kh = min(NA_WIN_H, rows)
    r = jnp.arange(rows)
    row_idx = jnp.clip(r - kh // 2, 0, rows - kh)[:, None] + jnp.arange(kh)[None, :]
    col = jnp.arange(GRID_W)
    col_start = jnp.clip(col - NA_WIN_W // 2, 0, GRID_W - NA_WIN_W)
    col_in = (col[None, :] >= col_start[:, None]) & (col[None, :] < col_start[:, None] + NA_WIN_W)
    d_row = row_idx - r[:, None] + (NA_WIN_H - 1)
    d_col = jnp.clip(col[None, :] - col[:, None], -(NA_WIN_W - 1), NA_WIN_W - 1) + (NA_WIN_W - 1)
    bias = rpb[:, d_row[:, None, :, None], d_col[None, :, None, :]].astype(F32)
    bias = bias.transpose(1, 0, 2, 3, 4)
    qg = q.reshape(B, rows, GRID_W, H, dh)
    kg = k.reshape(B, rows, GRID_W, H, dh)[:, row_idx]
    vg = v.reshape(B, rows, GRID_W, H, dh)[:, row_idx]
    s_loc = jnp.einsum("brqhd,brikhd->brhqik", qg, kg).astype(F32) + bias[None]
    s_loc = jnp.where(col_in[:, None, :], s_loc, NEG_INF)
    s_ctx = jnp.einsum("brqhd,bkhd->brhqk", qg, ctx_k).astype(F32)
    n_loc = kh * GRID_W
    p = jax.nn.softmax(jnp.concatenate([s_loc.reshape(B, rows, H, GRID_W, n_loc), s_ctx], axis=-1),
                       axis=-1).astype(v.dtype)
    o = (jnp.einsum("brhqik,brikhd->brqhd", p[..., :n_loc].reshape(B, rows, H, GRID_W, kh, GRID_W), vg)
         + jnp.einsum("brhqk,bkhd->brqhd", p[..., n_loc:], ctx_v))
    return o.reshape(B, L, H, dh)


def conformer_conv(a, b, w_dw, ln_g, ln_b):
    h = a * jax.nn.sigmoid(b)
    h = dwconv(h, w_dw)
    return jax.nn.silu(layer_norm(h, ln_g, ln_b))


def retention_scan(q, k, v, log_g, s0):
    B, L, H, dh = q.shape
    n = L // RET_CHUNK

    def chunks(t):
        return t.astype(F32).reshape(B, n, RET_CHUNK, H, dh).transpose(1, 0, 3, 2, 4)

    pos = jnp.arange(RET_CHUNK, dtype=F32)
    diff = pos[:, None] - pos[None, :]
    decay_in = jnp.where(diff >= 0, jnp.exp(jnp.maximum(diff, 0.0) * log_g[:, None, None]), 0.0)
    decay_q = jnp.exp((pos + 1.0) * log_g[:, None])[None, :, :, None]
    decay_k = jnp.exp((RET_CHUNK - 1.0 - pos) * log_g[:, None])[None, :, :, None]
    decay_c = jnp.exp(RET_CHUNK * log_g)[None, :, None, None]

    def step(s, inp):
        qc, kc, vc = inp
        inner = jnp.einsum("bhid,bhjd->bhij", qc, kc) * decay_in
        o = jnp.einsum("bhij,bhjd->bhid", inner, vc) + jnp.einsum("bhid,bhde->bhie", qc, s) * decay_q
        s = s * decay_c + jnp.einsum("bhjd,bhje->bhde", kc * decay_k, vc)
        return s, o

    s, o = lax.scan(step, s0.astype(F32), (chunks(q), chunks(k), chunks(v)))
    return o.transpose(1, 0, 3, 2, 4).reshape(B, L, H, dh), s


def rwkv7_scan(r, decay, k, v, kk, a, s0):
    def step(s, inp):
        r_t, w_t, k_t, v_t, kk_t, a_t = inp
        sa = jnp.einsum("bhvk,bhk->bhv", s, -kk_t)
        s = (s * w_t[:, :, None, :] + sa[..., None] * (kk_t * a_t)[:, :, None, :]
             + v_t[..., None] * k_t[:, :, None, :])
        return s, jnp.einsum("bhvk,bhk->bhv", s, r_t)

    xs = tuple(jnp.swapaxes(t, 0, 1) for t in (r, decay, k, v, kk, a))
    s, o = lax.scan(step, s0.astype(F32), xs)
    return jnp.swapaxes(o, 0, 1), s


def token_mixers(h, p, cache):
    B, L, _ = h.shape
    dt = h.dtype
    latent = cache is not None
    (na_q, na_k, na_v, cv_a, cv_b, rt_q, rt_k, rt_v, rt_g,
     rw_r, rw_k, rw_v, rw_w1, rw_a1, rw_g1) = jnp.split(h @ p["w_in"], SPLIT_POINTS, axis=-1)

    def heads(t):
        return t.reshape(B, L, N_HEADS, HEAD_DIM)

    scale = HEAD_DIM ** -0.5

    a_q, a_k, a_v = heads(na_q) * scale, heads(na_k), heads(na_v)
    if latent:
        o_a = neighbourhood_attention(a_q, a_k, a_v, cache["na_k"], cache["na_v"], p["na_rpb"])
    else:
        o_a = context_attention(a_q, a_k, a_v)
    o_a = o_a.reshape(B, L, GROUP_W)

    o_b = conformer_conv(cv_a, cv_b, p["conv_dw"], p["conv_ln_g"], p["conv_ln_b"])

    c_q, c_k, c_v = heads(rt_q), heads(rt_k) * scale, heads(rt_v)
    if latent:
        cos, sin = axial_rope(L)
        c_q, c_k = apply_rope(c_q, cos, sin), apply_rope(c_k, cos, sin)
        ret0 = cache["ret"].astype(F32)
    else:
        ret0 = jnp.zeros((B, 2, N_HEADS, HEAD_DIM, HEAD_DIM), F32)
    log_g = jax.nn.log_sigmoid(p["ret_decay"].astype(F32))
    o_cf, s_cf = retention_scan(c_q, c_k, c_v, log_g[0], ret0[:, 0])
    o_cb, s_cb = retention_scan(rev(c_q), rev(c_k), rev(c_v), log_g[1], ret0[:, 1])
    o_c = head_norm(o_cf + rev(o_cb), p["ret_gn"], EPS) * jax.nn.silu(rt_g.astype(F32))
    ret_state = jnp.stack([s_cf, s_cb], axis=1)

    d_r, d_k, d_v = jnp.split(dwconv(jnp.concatenate([rw_r, rw_k, rw_v], axis=-1), p["rwkv_shift"]).astype(F32), 3, axis=-1)
    w_low = jnp.tanh(rw_w1.astype(F32))
    a_low = rw_a1.astype(F32)
    gate = jax.nn.sigmoid(rw_g1.astype(F32)) @ p["rwkv_g2"].astype(F32)
    kk = heads(d_k * p["rwkv_kk"])
    kk = kk * lax.rsqrt(jnp.sum(kk * kk, axis=-1, keepdims=True) + 1e-12)
    r_h, v_h, rk_h = heads(d_r), heads(d_v), p["rwkv_rk"].astype(F32).reshape(N_HEADS, HEAD_DIM)
    rwkv0 = cache["rwkv"].astype(F32) if latent else jnp.zeros((B, 2, N_HEADS, HEAD_DIM, HEAD_DIM), F32)
    outs, bonuses, states = [], [], []
    for d in range(2):
        z_w = p["rwkv_w0"][d] + w_low @ p["rwkv_w2"][d]
        decay = heads(jnp.exp(-jnp.exp(-jax.nn.softplus(-z_w) - 0.5)))
        a = jax.nn.sigmoid(p["rwkv_a0"][d] + a_low @ p["rwkv_a2"][d])
        k_d = heads(d_k * (1.0 + (a - 1.0) * p["rwkv_ka"]))
        seqs = (r_h, decay, k_d, v_h, kk, heads(a))
        if d == 1:
            seqs = tuple(rev(t) for t in seqs)
        o, s = rwkv7_scan(*seqs, rwkv0[:, d])
        outs.append(rev(o) if d == 1 else o)
        bonuses.append(jnp.sum(r_h * k_d * rk_h, axis=-1, keepdims=True) * v_h)
        states.append(s)
    o_d = (head_norm(outs[0] + outs[1], p["rwkv_gn"], RWKV_GN_EPS)
           + (bonuses[0] + bonuses[1]).reshape(B, L, GROUP_W)) * gate
    rwkv_state = jnp.stack(states, axis=1)

    mixed = jnp.concatenate([o_a.astype(dt), o_b.astype(dt), o_c.astype(dt), o_d.astype(dt)], axis=-1) @ p["w_out"]
    if latent:
        return mixed, None
    return mixed, (a_k, a_v, ret_state, rwkv_state)


def conv_ffn(h, w_up, w_conv, w_down):
    u = dwconv(h @ w_up, w_conv)
    g, val = jnp.split(u, 2, axis=-1)
    return (jax.nn.silu(g) * val) @ w_down


def trunk_layer(x, mod, p, cache):
    shift1, scale1, gate1, shift2, scale2, gate2 = jnp.split(mod[:, None, :].astype(x.dtype), 6, axis=-1)
    h = rms_norm(x, p["norm_g"][0]) * (1.0 + scale1) + shift1
    m, state = token_mixers(h, p, cache)
    x = x + gate1 * rms_norm(m, p["norm_g"][1])
    h = rms_norm(x, p["norm_g"][2]) * (1.0 + scale2) + shift2
    f = conv_ffn(h, p["ffn_up"], p["ffn_conv"], p["ffn_down"])
    x = x + gate2 * rms_norm(f, p["norm_g"][3])
    return x, state


def setup_inputs(seed: int = 0) -> dict:
    key = jax.random.key(seed)
    ks = iter(jax.random.split(key, 48))

    def nrm(shape, s):
        return s * jax.random.normal(next(ks), shape, F32)

    G, H, dh = GROUP_W, N_HEADS, HEAD_DIM
    ret_base = jnp.log(2.0 ** (5.0 + jnp.arange(H, dtype=F32)) - 1.0)
    center3 = jnp.array([0.0, 1.0, 0.0], F32)[None, :, None]
    return {
        "x_prompt": nrm((BATCH, SEQ, D_MODEL), 1.0),
        "x_sample": nrm((DEC_BATCH, DEC_SEQ, D_MODEL), 1.0),
        "cache_na_k": nrm((DEC_BATCH, DEPTH, PAST_LEN, H, dh), 1.0),
        "cache_na_v": nrm((DEC_BATCH, DEPTH, PAST_LEN, H, dh), 1.0),
        "state_retention": nrm((DEC_BATCH, DEPTH, 2, H, dh, dh), 0.5),
        "state_rwkv": nrm((DEC_BATCH, DEPTH, 2, H, dh, dh), 0.5),
        "c": nrm((DEC_BATCH, D_MODEL), 1.0),
        "c_ctx": nrm((D_MODEL,), 1.0),
        "ada_w": nrm((DEPTH, D_MODEL, 6 * D_MODEL), 0.5 * D_MODEL ** -0.5),
        "ada_b": nrm((DEPTH, 6 * D_MODEL), 0.02),
        "norm_g": 1.0 + nrm((DEPTH, 4, D_MODEL), 0.05),
        "w_in": nrm((DEPTH, D_MODEL, D_IN), D_MODEL ** -0.5),
        "w_out": nrm((DEPTH, N_MIXERS * G, D_MODEL), (N_MIXERS * G) ** -0.5),
        "na_rpb": nrm((DEPTH, H, 2 * NA_WIN_H - 1, 2 * NA_WIN_W - 1), 0.1),
        "conv_dw": nrm((DEPTH, CONV_W, G), CONV_W ** -0.5),
        "conv_ln_g": 1.0 + nrm((DEPTH, G), 0.05),
        "conv_ln_b": nrm((DEPTH, G), 0.02),
        "ret_decay": ret_base + nrm((DEPTH, 2, H), 0.1),
        "ret_gn": 1.0 + nrm((DEPTH, G), 0.05),
        "rwkv_shift": center3 + nrm((DEPTH, RWKV_SHIFT_W, 3 * G), 0.2),
        "rwkv_w0": jnp.linspace(-5.5, -0.5, G, dtype=F32) + nrm((DEPTH, 2, G), 0.1),
        "rwkv_w2": nrm((DEPTH, 2, RWKV_LORA_W, G), 0.1 * RWKV_LORA_W ** -0.5),
        "rwkv_a0": nrm((DEPTH, 2, G), 0.1),
        "rwkv_a2": nrm((DEPTH, 2, RWKV_LORA_A, G), 0.1 * RWKV_LORA_A ** -0.5),
        "rwkv_g2": nrm((DEPTH, RWKV_LORA_G, G), RWKV_LORA_G ** -0.5),
        "rwkv_kk": 0.85 + nrm((DEPTH, G), 0.05),
        "rwkv_ka": 1.0 + nrm((DEPTH, G), 0.05),
        "rwkv_rk": nrm((DEPTH, G), 0.1),
        "rwkv_gn": 1.0 + nrm((DEPTH, G), 0.05),
        "ffn_up": nrm((DEPTH, D_MODEL, 2 * D_FF), D_MODEL ** -0.5),
        "ffn_conv": center3 + nrm((DEPTH, FFN_CONV_W, 2 * D_FF), 0.2),
        "ffn_down": nrm((DEPTH, D_FF, D_MODEL), D_FF ** -0.5),
    }


def reference(x_prompt, x_sample, cache_na_k, cache_na_v, state_retention, state_rwkv, c, c_ctx,
              ada_w, ada_b, norm_g, w_in, w_out, na_rpb, conv_dw, conv_ln_g, conv_ln_b, ret_decay, ret_gn,
              rwkv_shift, rwkv_w0, rwkv_w2, rwkv_a0, rwkv_a2, rwkv_g2, rwkv_kk, rwkv_ka, rwkv_rk, rwkv_gn,
              ffn_up, ffn_conv, ffn_down):
    y_prompt, y_sample = x_prompt, x_sample
    new_k, new_v, new_ret, new_rwkv = [], [], [], []
    for l in range(DEPTH):
        p = {"norm_g": norm_g[l], "w_in": w_in[l], "w_out": w_out[l], "na_rpb": na_rpb[l],
             "conv_dw": conv_dw[l], "conv_ln_g": conv_ln_g[l], "conv_ln_b": conv_ln_b[l],
             "ret_decay": ret_decay[l], "ret_gn": ret_gn[l], "rwkv_shift": rwkv_shift[l],
             "rwkv_w0": rwkv_w0[l], "rwkv_w2": rwkv_w2[l], "rwkv_a0": rwkv_a0[l], "rwkv_a2": rwkv_a2[l],
             "rwkv_g2": rwkv_g2[l], "rwkv_kk": rwkv_kk[l], "rwkv_ka": rwkv_ka[l], "rwkv_rk": rwkv_rk[l],
             "rwkv_gn": rwkv_gn[l], "ffn_up": ffn_up[l], "ffn_conv": ffn_conv[l], "ffn_down": ffn_down[l]}
        mod_ctx = (jax.nn.silu(c_ctx) @ ada_w[l] + ada_b[l])[None]
        y_prompt, (k_l, v_l, ret_l, rwkv_l) = trunk_layer(y_prompt, mod_ctx, p, None)
        new_k.append(k_l)
        new_v.append(v_l)
        new_ret.append(ret_l)
        new_rwkv.append(rwkv_l)
        mod_lat = jax.nn.silu(c) @ ada_w[l] + ada_b[l]
        cache_l = {"na_k": cache_na_k[:, l], "na_v": cache_na_v[:, l],
                   "ret": state_retention[:, l], "rwkv": state_rwkv[:, l]}
        y_sample, _ = trunk_layer(y_sample, mod_lat, p, cache_l)
    new_na_k = jnp.stack(new_k, axis=1)
    new_na_v = jnp.stack(new_v, axis=1)
    new_state_retention = jnp.stack(new_ret, axis=1)
    new_state_rwkv = jnp.stack(new_rwkv, axis=1)
    return (y_prompt, y_sample, new_na_k, new_na_v, new_state_retention, new_state_rwkv)
```

```python
import functools

import jax
import jax.numpy as jnp
import numpy as np
from jax import lax
from jax.experimental import pallas as pl
from jax.experimental.pallas import tpu as pltpu

F32 = jnp.float32
BF16 = jnp.bfloat16

D_MODEL = 1024
DEPTH = 2
GRID_W = 64
N_MIXERS = 4
GROUP_W = D_MODEL // N_MIXERS
HEAD_DIM = 64
N_HEADS = GROUP_W // HEAD_DIM
NA_WIN_H = 8
NA_WIN_W = 16
Q_BLOCK = 128
CONV_W = 31
RET_CHUNK = 128
RWKV_LORA_W = 64
RWKV_LORA_A = 64
RWKV_LORA_G = 128
D_FF = 2816
ROPE_BASE = 10000.0
EPS = 1e-6
RWKV_GN_EPS = 64e-5
NEG_INF = -1e30
SPLIT_SIZES = (GROUP_W,) * 12 + (RWKV_LORA_W, RWKV_LORA_A, RWKV_LORA_G)
SPLIT_POINTS = tuple(int(s) for s in np.cumsum(SPLIT_SIZES)[:-1])

RWKV_CHUNK = 64
VMEM_LIMIT_BYTES = 48 * 1024 * 1024


def _mm_kernel(a_ref, b_ref, o_ref):
    o_ref[...] = jnp.dot(a_ref[...].astype(BF16), b_ref[...], preferred_element_type=F32)


def _pick_tile(n, target):
    best = 128
    for t in range(128, target + 1, 128):
        if n % t == 0:
            best = t
    return best


def matmul(a, b_bf16, tm=512, tn_target=1792):
    m, k = a.shape
    n = b_bf16.shape[1]
    tn = _pick_tile(n, tn_target)
    tm = min(tm, m)
    assert m % tm == 0 and n % tn == 0
    return pl.pallas_call(
        _mm_kernel,
        grid=(n // tn, m // tm),
        in_specs=[pl.BlockSpec((tm, k), lambda j, i: (i, 0)),
                  pl.BlockSpec((k, tn), lambda j, i: (0, j))],
        out_specs=pl.BlockSpec((tm, tn), lambda j, i: (i, j)),
        out_shape=jax.ShapeDtypeStruct((m, n), F32),
        compiler_params=pltpu.CompilerParams(
            dimension_semantics=("parallel", "parallel"), vmem_limit_bytes=VMEM_LIMIT_BYTES),
        name="matmul",
    )(a, b_bf16)


def _split2(x):
    hi = x.astype(BF16)
    lo = (x - hi.astype(F32)).astype(BF16)
    return hi, lo


def _dot(a, b, dims):
    return lax.dot_general(a, b, (dims, ((), ())), preferred_element_type=F32)


_NN = ((1,), (0,))
_NT = ((1,), (1,))
_TN = ((0,), (0,))


def _dot3(a2, b2, dims=_NN):
    (ah, al), (bh, bl) = a2, b2
    return _dot(ah, bh, dims) + (_dot(ah, bl, dims) + _dot(al, bh, dims))


def _rwkv_kernel(r_ref, v_ref, kk_ref, lw_ref, k_ref, a_ref, s0_ref, o_ref, st_ref, t_scr, *, chunk, n_chunks):
    C = chunk
    d = pl.program_id(0)
    c = pl.program_id(2)

    @pl.when(c == 0)
    def _():
        t_scr[...] = s0_ref[0, 0]

    sgn = jnp.where(d == 0, 1, -1)
    row = lax.broadcasted_iota(jnp.int32, (C, C), 0)
    col = lax.broadcasted_iota(jnp.int32, (C, C), 1)
    diff = (row - col) * sgn
    incl = diff >= 0
    strict = diff > 0
    incl_bf = jnp.where(incl, 1.0, 0.0).astype(BF16)
    eye = jnp.where(row == col, 1.0, 0.0)
    drow = lax.broadcasted_iota(jnp.int32, (HEAD_DIM, HEAD_DIM), 0)
    dcol = lax.broadcasted_iota(jnp.int32, (HEAD_DIM, HEAD_DIM), 1)

    for h in range(N_HEADS):
        sl = slice(h * HEAD_DIM, (h + 1) * HEAD_DIM)
        lw = lw_ref[0, 0, :, sl]
        r = r_ref[0, :, sl]
        v = v_ref[0, :, sl]
        kk = kk_ref[0, :, sl]
        k = k_ref[0, 0, :, sl]
        a = a_ref[0, 0, :, sl]

        l_hi = lw.astype(BF16)
        l_r1 = lw - l_hi.astype(F32)
        l_mid = l_r1.astype(BF16)
        l_lo = (l_r1 - l_mid.astype(F32)).astype(BF16)
        cs = _dot(incl_bf, l_hi, _NN) + (_dot(incl_bf, l_mid, _NN) + _dot(incl_bf, l_lo, _NN))
        tot = jnp.sum(lw, axis=0, keepdims=True)

        beta = kk * a
        e_out = jnp.exp(-cs)
        e_rem = jnp.exp(tot - cs)
        at = -kk * jnp.exp(cs - lw)
        rt = r * jnp.exp(cs)
        ar = _split2(jnp.concatenate([at, rt], axis=0))
        bk = _split2(jnp.concatenate([beta * e_out, k * e_out], axis=0))
        bkh = _split2(jnp.concatenate([beta * e_rem, k * e_rem], axis=0))
        v2 = _split2(v)

        big = _dot3(ar, bk, _NT)
        a_ab = jnp.where(strict, big[:C, :C], 0.0)
        a_ak = jnp.where(strict, big[:C, C:], 0.0)
        a_rb = jnp.where(incl, big[C:, :C], 0.0)
        a_rk = jnp.where(incl, big[C:, C:], 0.0)

        p2 = _split2(a_ab)
        minv = eye + a_ab
        steps = int(np.log2(C)) - 1
        for i in range(steps):
            p = _dot3(p2, p2)
            p2 = _split2(p)
            minv = minv + _dot3(_split2(minv), p2)
        m2 = _split2(minv)

        t0 = t_scr[h]
        t02 = _split2(t0)
        x = _dot3(ar, t02)
        rhs = x[:C] + _dot3(_split2(a_ak), v2)
        u = _dot3(m2, _split2(rhs))
        u2 = _split2(u)
        o = x[C:] + _dot3(_split2(a_rb), u2) + _dot3(_split2(a_rk), v2)
        o_ref[0, 0, :, sl] = o

        uv = (jnp.concatenate([u2[0], v2[0]], axis=0), jnp.concatenate([u2[1], v2[1]], axis=0))
        gcol = jnp.sum(jnp.where(drow == dcol, jnp.exp(tot), 0.0), axis=1, keepdims=True)
        t_scr[h] = gcol * t0 + _dot3(bkh, uv, _TN)

    @pl.when(c == n_chunks - 1)
    def _():
        st_ref[0, 0] = t_scr[...]


def rwkv7_scan(r, v, kk, lw, k, a, s0t, chunk=RWKV_CHUNK):
    B, L, G = r.shape
    n = L // chunk
    assert L % chunk == 0

    def tok(d, b, c):
        return c + d * (n - 1 - 2 * c)

    shared = pl.BlockSpec((1, chunk, G), lambda d, b, c: (b, tok(d, b, c), 0))
    perdir = pl.BlockSpec((1, 1, chunk, G), lambda d, b, c: (d, b, tok(d, b, c), 0))
    state = pl.BlockSpec((1, 1, N_HEADS, HEAD_DIM, HEAD_DIM), lambda d, b, c: (d, b, 0, 0, 0))
    return pl.pallas_call(
        functools.partial(_rwkv_kernel, chunk=chunk, n_chunks=n),
        grid=(2, B, n),
        in_specs=[shared, shared, shared, perdir, perdir, perdir, state],
        out_specs=[perdir, state],
        out_shape=[jax.ShapeDtypeStruct((2, B, L, G), F32),
                   jax.ShapeDtypeStruct((2, B, N_HEADS, HEAD_DIM, HEAD_DIM), F32)],
        scratch_shapes=[pltpu.VMEM((N_HEADS, HEAD_DIM, HEAD_DIM), F32)],
        compiler_params=pltpu.CompilerParams(
            dimension_semantics=("parallel", "parallel", "arbitrary"), vmem_limit_bytes=VMEM_LIMIT_BYTES),
        name="rwkv7_scan",
    )(r, v, kk, lw, k, a, s0t)


def rms_norm(x, g):
    return x * lax.rsqrt(jnp.mean(x * x, axis=-1, keepdims=True) + EPS) * g


def layer_norm(x, g, b):
    xc = x - jnp.mean(x, axis=-1, keepdims=True)
    return xc * lax.rsqrt(jnp.mean(xc * xc, axis=-1, keepdims=True) + EPS) * g + b


def head_norm(x, g, eps):
    xc = x - jnp.mean(x, axis=-1, keepdims=True)
    y = xc * lax.rsqrt(jnp.mean(xc * xc, axis=-1, keepdims=True) + eps)
    return y.reshape(x.shape[0], x.shape[1], -1) * g


def dwconv(x, w):
    k, ch = w.shape
    return lax.conv_general_dilated(x, w[:, None, :], window_strides=(1,), padding=[(k // 2, k // 2)],
                                    dimension_numbers=("NWC", "WIO", "NWC"), feature_group_count=ch)


def rev(t):
    return jnp.flip(t, axis=1)


def axial_rope(length):
    t = jnp.arange(length)
    n_freq = HEAD_DIM // 4
    inv = ROPE_BASE ** (-jnp.arange(n_freq, dtype=F32) / n_freq)
    ang = jnp.concatenate([(t // GRID_W).astype(F32)[:, None] * inv,
                           (t % GRID_W).astype(F32)[:, None] * inv], axis=-1)
    return jnp.cos(ang), jnp.sin(ang)


def apply_rope(x, cos, sin):
    x1, x2 = jnp.split(x, 2, axis=-1)
    c, s = cos[None, :, None, :], sin[None, :, None, :]
    return jnp.concatenate([x1 * c - x2 * s, x1 * s + x2 * c], axis=-1)


def context_attention(q, k, v):
    s = jnp.einsum("bqhd,bkhd->bhqk", q, k)
    p = jax.nn.softmax(s, axis=-1)
    return jnp.einsum("bhqk,bkhd->bqhd", p, v)


def neighbourhood_attention(q, k, v, ctx_k, ctx_v, rpb):
    B, L, H, dh = q.shape
    rows = L // GRID_W
    kh = min(NA_WIN_H, rows)
    r = jnp.arange(rows)
    row_idx = jnp.clip(r - kh // 2, 0, rows - kh)[:, None] + jnp.arange(kh)[None, :]
    col = jnp.arange(GRID_W)
    col_start = jnp.clip(col - NA_WIN_W // 2, 0, GRID_W - NA_WIN_W)
    col_in = (col[None, :] >= col_start[:, None]) & (col[None, :] < col_start[:, None] + NA_WIN_W)
    d_row = row_idx - r[:, None] + (NA_WIN_H - 1)
    d_col = jnp.clip(col[None, :] - col[:, None], -(NA_WIN_W - 1), NA_WIN_W - 1) + (NA_WIN_W - 1)
    bias = rpb[:, d_row[:, None, :, None], d_col[None, :, None, :]]
    bias = bias.transpose(1, 0, 2, 3, 4)
    qg = q.reshape(B, rows, GRID_W, H, dh)
    kg = k.reshape(B, rows, GRID_W, H, dh)[:, row_idx]
    vg = v.reshape(B, rows, GRID_W, H, dh)[:, row_idx]
    s_loc = jnp.einsum("brqhd,brikhd->brhqik", qg, kg) + bias[None]
    s_loc = jnp.where(col_in[:, None, :], s_loc, NEG_INF)
    s_ctx = jnp.einsum("brqhd,bkhd->brhqk", qg, ctx_k)
    n_loc = kh * GRID_W
    p = jax.nn.softmax(jnp.concatenate([s_loc.reshape(B, rows, H, GRID_W, n_loc), s_ctx], axis=-1), axis=-1)
    o = (jnp.einsum("brhqik,brikhd->brqhd", p[..., :n_loc].reshape(B, rows, H, GRID_W, kh, GRID_W), vg)
         + jnp.einsum("brhqk,bkhd->brqhd", p[..., n_loc:], ctx_v))
    return o.reshape(B, L, H, dh)


def retention_scan(q, k, v, log_g, s0):
    B, L, H, dh = q.shape
    n = L // RET_CHUNK

    def chunks(t):
        return t.reshape(B, n, RET_CHUNK, H, dh).transpose(1, 0, 3, 2, 4)

    pos = jnp.arange(RET_CHUNK, dtype=F32)
    diff = pos[:, None] - pos[None, :]
    decay_in = jnp.where(diff >= 0, jnp.exp(jnp.maximum(diff, 0.0) * log_g[:, None, None]), 0.0)
    decay_q = jnp.exp((pos + 1.0) * log_g[:, None])[None, :, :, None]
    decay_k = jnp.exp((RET_CHUNK - 1.0 - pos) * log_g[:, None])[None, :, :, None]
    decay_c = jnp.exp(RET_CHUNK * log_g)[None, :, None, None]

    def step(s, inp):
        qc, kc, vc = inp
        inner = jnp.einsum("bhid,bhjd->bhij", qc, kc) * decay_in
        o = jnp.einsum("bhij,bhjd->bhid", inner, vc) + jnp.einsum("bhid,bhde->bhie", qc, s) * decay_q
        s = s * decay_c + jnp.einsum("bhjd,bhje->bhde", kc * decay_k, vc)
        return s, o

    s, o = lax.scan(step, s0, (chunks(q), chunks(k), chunks(v)))
    return o.transpose(1, 0, 3, 2, 4).reshape(B, L, H, dh), s


def token_mixers(h, p, cache):
    B, L, _ = h.shape
    latent = cache is not None
    proj = matmul(h.reshape(B * L, D_MODEL), p["w_in"]).reshape(B, L, -1)
    (na_q, na_k, na_v, cv_a, cv_b, rt_q, rt_k, rt_v, rt_g,
     rw_r, rw_k, rw_v, rw_w1, rw_a1, rw_g1) = jnp.split(proj, SPLIT_POINTS, axis=-1)

    def heads(t):
        return t.reshape(B, L, N_HEADS, HEAD_DIM)

    scale = HEAD_DIM ** -0.5

    a_q, a_k, a_v = heads(na_q) * scale, heads(na_k), heads(na_v)
    if latent:
        o_a = neighbourhood_attention(a_q, a_k, a_v, cache["na_k"], cache["na_v"], p["na_rpb"])
    else:
        o_a = context_attention(a_q, a_k, a_v)
    o_a = o_a.reshape(B, L, GROUP_W)

    hb = cv_a * jax.nn.sigmoid(cv_b)
    o_b = jax.nn.silu(layer_norm(dwconv(hb, p["conv_dw"]), p["conv_ln_g"], p["conv_ln_b"]))

    c_q, c_k, c_v = heads(rt_q), heads(rt_k) * scale, heads(rt_v)
    if latent:
        cos, sin = axial_rope(L)
        c_q, c_k = apply_rope(c_q, cos, sin), apply_rope(c_k, cos, sin)
        ret0 = cache["ret"]
    else:
        ret0 = jnp.zeros((B, 2, N_HEADS, HEAD_DIM, HEAD_DIM), F32)
    log_g = jax.nn.log_sigmoid(p["ret_decay"])
    o_cf, s_cf = retention_scan(c_q, c_k, c_v, log_g[0], ret0[:, 0])
    o_cb, s_cb = retention_scan(rev(c_q), rev(c_k), rev(c_v), log_g[1], ret0[:, 1])
    o_c = head_norm(o_cf + rev(o_cb), p["ret_gn"], EPS) * jax.nn.silu(rt_g)
    ret_state = jnp.stack([s_cf, s_cb], axis=1)

    d_r, d_k, d_v = jnp.split(dwconv(jnp.concatenate([rw_r, rw_k, rw_v], axis=-1), p["rwkv_shift"]), 3, axis=-1)
    w_low = jnp.tanh(rw_w1)
    gate = jax.nn.sigmoid(rw_g1) @ p["rwkv_g2"]
    kk = heads(d_k * p["rwkv_kk"])
    kk = (kk * lax.rsqrt(jnp.sum(kk * kk, axis=-1, keepdims=True) + 1e-12)).reshape(B, L, GROUP_W)
    rwkv0 = cache["rwkv"] if latent else jnp.zeros((B, 2, N_HEADS, HEAD_DIM, HEAD_DIM), F32)
    lws, ks, As = [], [], []
    for d in range(2):
        z_w = p["rwkv_w0"][d] + w_low @ p["rwkv_w2"][d]
        lws.append(-jnp.exp(-jax.nn.softplus(-z_w) - 0.5))
        a = jax.nn.sigmoid(p["rwkv_a0"][d] + rw_a1 @ p["rwkv_a2"][d])
        ks.append(d_k * (1.0 + (a - 1.0) * p["rwkv_ka"]))
        As.append(a)
    lw2, k2, a2 = jnp.stack(lws), jnp.stack(ks), jnp.stack(As)
    s0t = jnp.swapaxes(rwkv0, -1, -2).transpose(1, 0, 2, 3, 4)
    o2, st = rwkv7_scan(d_r, d_v, kk, lw2, k2, a2, s0t)
    rwkv_state = jnp.swapaxes(st, -1, -2).transpose(1, 0, 2, 3, 4)
    bonus = (jnp.sum(heads(d_r * (k2[0] + k2[1]) * p["rwkv_rk"]), axis=-1, keepdims=True) * heads(d_v))
    o_d = (head_norm(heads(o2[0] + o2[1]), p["rwkv_gn"], RWKV_GN_EPS) + bonus.reshape(B, L, GROUP_W)) * gate

    mixed = matmul(jnp.concatenate([o_a, o_b, o_c, o_d], axis=-1).reshape(B * L, D_MODEL), p["w_out"])
    mixed = mixed.reshape(B, L, D_MODEL)
    if latent:
        return mixed, None
    return mixed, (a_k, a_v, ret_state, rwkv_state)


def conv_ffn(h, w_up, w_conv, w_down):
    B, L, _ = h.shape
    u = dwconv(matmul(h.reshape(B * L, D_MODEL), w_up).reshape(B, L, -1), w_conv)
    g, val = jnp.split(u, 2, axis=-1)
    return matmul((jax.nn.silu(g) * val).reshape(B * L, D_FF), w_down).reshape(B, L, D_MODEL)


def trunk_layer(x, mod, p, cache):
    shift1, scale1, gate1, shift2, scale2, gate2 = jnp.split(mod[:, None, :], 6, axis=-1)
    h = rms_norm(x, p["norm_g"][0]) * (1.0 + scale1) + shift1
    m, state = token_mixers(h, p, cache)
    x = x + gate1 * rms_norm(m, p["norm_g"][1])
    h = rms_norm(x, p["norm_g"][2]) * (1.0 + scale2) + shift2
    f = conv_ffn(h, p["ffn_up"], p["ffn_conv"], p["ffn_down"])
    x = x + gate2 * rms_norm(f, p["norm_g"][3])
    return x, state


def kernel(x_prompt, x_sample, cache_na_k, cache_na_v, state_retention, state_rwkv, c, c_ctx, ada_w, ada_b, norm_g, w_in, w_out, na_rpb, conv_dw, conv_ln_g, conv_ln_b, ret_decay, ret_gn, rwkv_shift, rwkv_w0, rwkv_w2, rwkv_a0, rwkv_a2, rwkv_g2, rwkv_kk, rwkv_ka, rwkv_rk, rwkv_gn, ffn_up, ffn_conv, ffn_down):
    y_prompt, y_sample = x_prompt, x_sample
    new_k, new_v, new_ret, new_rwkv = [], [], [], []
    for l in range(DEPTH):
        p = {"norm_g": norm_g[l], "w_in": w_in[l].astype(BF16), "w_out": w_out[l].astype(BF16), "na_rpb": na_rpb[l],
             "conv_dw": conv_dw[l], "conv_ln_g": conv_ln_g[l], "conv_ln_b": conv_ln_b[l],
             "ret_decay": ret_decay[l], "ret_gn": ret_gn[l], "rwkv_shift": rwkv_shift[l],
             "rwkv_w0": rwkv_w0[l], "rwkv_w2": rwkv_w2[l], "rwkv_a0": rwkv_a0[l], "rwkv_a2": rwkv_a2[l],
             "rwkv_g2": rwkv_g2[l], "rwkv_kk": rwkv_kk[l], "rwkv_ka": rwkv_ka[l], "rwkv_rk": rwkv_rk[l],
             "rwkv_gn": rwkv_gn[l], "ffn_up": ffn_up[l].astype(BF16), "ffn_conv": ffn_conv[l],
             "ffn_down": ffn_down[l].astype(BF16)}
        mod_ctx = (jax.nn.silu(c_ctx) @ ada_w[l] + ada_b[l])[None]
        y_prompt, (k_l, v_l, ret_l, rwkv_l) = trunk_layer(y_prompt, mod_ctx, p, None)
        new_k.append(k_l)
        new_v.append(v_l)
        new_ret.append(ret_l)
        new_rwkv.append(rwkv_l)
        mod_lat = jax.nn.silu(c) @ ada_w[l] + ada_b[l]
        cache_l = {"na_k": cache_na_k[:, l], "na_v": cache_na_v[:, l],
                   "ret": state_retention[:, l], "rwkv": state_rwkv[:, l]}
        y_sample, _ = trunk_layer(y_sample, mod_lat, p, cache_l)
    return (y_prompt, y_sample, jnp.stack(new_k, axis=1), jnp.stack(new_v, axis=1),
            jnp.stack(new_ret, axis=1), jnp.stack(new_rwkv, axis=1))
```

```python
import functools

import jax
import jax.numpy as jnp
import numpy as np
from jax import lax
from jax.experimental import pallas as pl
from jax.experimental.pallas import tpu as pltpu

F32 = jnp.float32
BF16 = jnp.bfloat16

D_MODEL = 1024
DEPTH = 2
GRID_W = 64
N_MIXERS = 4
GROUP_W = D_MODEL // N_MIXERS
HEAD_DIM = 64
N_HEADS = GROUP_W // HEAD_DIM
NA_WIN_H = 8
NA_WIN_W = 16
Q_BLOCK = 128
CONV_W = 31
RET_CHUNK = 128
RWKV_LORA_W = 64
RWKV_LORA_A = 64
RWKV_LORA_G = 128
D_FF = 2816
ROPE_BASE = 10000.0
EPS = 1e-6
RWKV_GN_EPS = 64e-5
NEG_INF = -1e30
SPLIT_SIZES = (GROUP_W,) * 12 + (RWKV_LORA_W, RWKV_LORA_A, RWKV_LORA_G)
SPLIT_POINTS = tuple(int(s) for s in np.cumsum(SPLIT_SIZES)[:-1])

RWKV_CHUNK = 64
VMEM_LIMIT_BYTES = 48 * 1024 * 1024


def _mm_kernel(a_ref, b_ref, o_ref):
    o_ref[...] = jnp.dot(a_ref[...].astype(BF16), b_ref[...], preferred_element_type=F32)


def _pick_tile(n, target):
    best = 128
    for t in range(128, target + 1, 128):
        if n % t == 0:
            best = t
    return best


def matmul(a, b_bf16, tm=512, tn_target=1792):
    m, k = a.shape
    n = b_bf16.shape[1]
    tn = _pick_tile(n, tn_target)
    tm = min(tm, m)
    assert m % tm == 0 and n % tn == 0
    return pl.pallas_call(
        _mm_kernel,
        grid=(n // tn, m // tm),
        in_specs=[pl.BlockSpec((tm, k), lambda j, i: (i, 0)),
                  pl.BlockSpec((k, tn), lambda j, i: (0, j))],
        out_specs=pl.BlockSpec((tm, tn), lambda j, i: (i, j)),
        out_shape=jax.ShapeDtypeStruct((m, n), F32),
        compiler_params=pltpu.CompilerParams(
            dimension_semantics=("parallel", "parallel"), vmem_limit_bytes=VMEM_LIMIT_BYTES),
        name="matmul",
    )(a, b_bf16)


def _split2(x):
    hi = x.astype(BF16)
    lo = (x - hi.astype(F32)).astype(BF16)
    return hi, lo


def _dot(a, b, dims):
    return lax.dot_general(a, b, (dims, ((), ())), preferred_element_type=F32)


_NN = ((1,), (0,))
_NT = ((1,), (1,))
_TN = ((0,), (0,))


def _dot3(a2, b2, dims=_NN):
    (ah, al), (bh, bl) = a2, b2
    return _dot(ah, bh, dims) + (_dot(ah, bl, dims) + _dot(al, bh, dims))


def _rwkv_kernel(r_ref, v_ref, kk_ref, lw_ref, k_ref, a_ref, s0_ref, o_ref, st_ref, t_scr, *, chunk, n_chunks):
    C = chunk
    d = pl.program_id(0)
    c = pl.program_id(2)

    @pl.when(c == 0)
    def _():
        t_scr[...] = s0_ref[0, 0]

    sgn = jnp.where(d == 0, 1, -1)
    row = lax.broadcasted_iota(jnp.int32, (C, C), 0)
    col = lax.broadcasted_iota(jnp.int32, (C, C), 1)
    diff = (row - col) * sgn
    incl = diff >= 0
    strict = diff > 0
    incl_bf = jnp.where(incl, 1.0, 0.0).astype(BF16)
    eye = jnp.where(row == col, 1.0, 0.0)
    drow = lax.broadcasted_iota(jnp.int32, (HEAD_DIM, HEAD_DIM), 0)
    dcol = lax.broadcasted_iota(jnp.int32, (HEAD_DIM, HEAD_DIM), 1)

    for h in range(N_HEADS):
        sl = slice(h * HEAD_DIM, (h + 1) * HEAD_DIM)
        lw = lw_ref[0, 0, :, sl]
        r = r_ref[0, :, sl]
        v = v_ref[0, :, sl]
        kk = kk_ref[0, :, sl]
        k = k_ref[0, 0, :, sl]
        a = a_ref[0, 0, :, sl]

        l_hi = lw.astype(BF16)
        l_r1 = lw - l_hi.astype(F32)
        l_mid = l_r1.astype(BF16)
        l_lo = (l_r1 - l_mid.astype(F32)).astype(BF16)
        cs = _dot(incl_bf, l_hi, _NN) + (_dot(incl_bf, l_mid, _NN) + _dot(incl_bf, l_lo, _NN))
        tot = jnp.sum(lw, axis=0, keepdims=True)

        beta = kk * a
        e_out = jnp.exp(-cs)
        e_rem = jnp.exp(tot - cs)
        at = -kk * jnp.exp(cs - lw)
        rt = r * jnp.exp(cs)
        ar = _split2(jnp.concatenate([at, rt], axis=0))
        bk = _split2(jnp.concatenate([beta * e_out, k * e_out], axis=0))
        bkh = _split2(jnp.concatenate([beta * e_rem, k * e_rem], axis=0))
        v2 = _split2(v)

        big = _dot3(ar, bk, _NT)
        a_ab = jnp.where(strict, big[:C, :C], 0.0)
        a_ak = jnp.where(strict, big[:C, C:], 0.0)
        a_rb = jnp.where(incl, big[C:, :C], 0.0)
        a_rk = jnp.where(incl, big[C:, C:], 0.0)

        p2 = _split2(a_ab)
        minv = eye + a_ab
        steps = int(np.log2(C)) - 1
        for i in range(steps):
            p = _dot3(p2, p2)
            p2 = _split2(p)
            minv = minv + _dot3(_split2(minv), p2)
        m2 = _split2(minv)

        t0 = t_scr[h]
        t02 = _split2(t0)
        x = _dot3(ar, t02)
        rhs = x[:C] + _dot3(_split2(a_ak), v2)
        u = _dot3(m2, _split2(rhs))
        u2 = _split2(u)
        o = x[C:] + _dot3(_split2(a_rb), u2) + _dot3(_split2(a_rk), v2)
        o_ref[0, 0, :, sl] = o

        uv = (jnp.concatenate([u2[0], v2[0]], axis=0), jnp.concatenate([u2[1], v2[1]], axis=0))
        gcol = jnp.sum(jnp.where(drow == dcol, jnp.exp(tot), 0.0), axis=1, keepdims=True)
        t_scr[h] = gcol * t0 + _dot3(bkh, uv, _TN)

    @pl.when(c == n_chunks - 1)
    def _():
        st_ref[0, 0] = t_scr[...]


def rwkv7_scan(r, v, kk, lw, k, a, s0t, chunk=RWKV_CHUNK):
    B, L, G = r.shape
    n = L // chunk
    assert L % chunk == 0

    def tok(d, b, c):
        return c + d * (n - 1 - 2 * c)

    shared = pl.BlockSpec((1, chunk, G), lambda d, b, c: (b, tok(d, b, c), 0))
    perdir = pl.BlockSpec((1, 1, chunk, G), lambda d, b, c: (d, b, tok(d, b, c), 0))
    state = pl.BlockSpec((1, 1, N_HEADS, HEAD_DIM, HEAD_DIM), lambda d, b, c: (d, b, 0, 0, 0))
    return pl.pallas_call(
        functools.partial(_rwkv_kernel, chunk=chunk, n_chunks=n),
        grid=(2, B, n),
        in_specs=[shared, shared, shared, perdir, perdir, perdir, state],
        out_specs=[perdir, state],
        out_shape=[jax.ShapeDtypeStruct((2, B, L, G), F32),
                   jax.ShapeDtypeStruct((2, B, N_HEADS, HEAD_DIM, HEAD_DIM), F32)],
        scratch_shapes=[pltpu.VMEM((N_HEADS, HEAD_DIM, HEAD_DIM), F32)],
        compiler_params=pltpu.CompilerParams(
            dimension_semantics=("parallel", "parallel", "arbitrary"), vmem_limit_bytes=VMEM_LIMIT_BYTES),
        name="rwkv7_scan",
    )(r, v, kk, lw, k, a, s0t)


def _softmax_pv(s_list, v_list):
    m = s_list[0].max(axis=-1, keepdims=True)
    for s in s_list[1:]:
        m = jnp.maximum(m, s.max(axis=-1, keepdims=True))
    den = 0.0
    acc = 0.0
    for s, v in zip(s_list, v_list):
        p = jnp.exp(s - m)
        den = den + p.sum(axis=-1, keepdims=True)
        acc = acc + _dot(p.astype(BF16), v, _NN)
    return acc / den


def _na_kernel(q_ref, k_ref, v_ref, ck_ref, cv_ref, tb_ref, mask_ref, o_ref, *, rows, kh):
    r = pl.program_id(1)
    rs = jnp.clip(r - kh // 2, 0, rows - kh)
    pat = r - rs
    start = pl.multiple_of(rs * GRID_W, GRID_W)
    kwin = k_ref[0, pl.ds(start, kh * GRID_W), :].astype(BF16)
    vwin = v_ref[0, pl.ds(start, kh * GRID_W), :].astype(BF16)
    ck = ck_ref[0, 0].astype(BF16)
    cv = cv_ref[0, 0].astype(BF16)
    q = (q_ref[0] * (HEAD_DIM ** -0.5)).astype(BF16)
    mask = mask_ref[...] > 0.0
    for h in range(N_HEADS):
        sl = slice(h * HEAD_DIM, (h + 1) * HEAD_DIM)
        qh = q[:, sl]
        s_loc = jnp.where(mask, _dot(qh, kwin[:, sl], _NT) + tb_ref[h, pat], NEG_INF)
        s_ctx = _dot(qh, ck[:, sl], _NT)
        o_ref[0, :, sl] = _softmax_pv([s_loc, s_ctx], [vwin[:, sl], cv[:, sl]])


def na_bias_table(rpb, kh):
    col = jnp.arange(GRID_W)
    d_col = jnp.clip(col[None, :] - col[:, None], -(NA_WIN_W - 1), NA_WIN_W - 1) + (NA_WIN_W - 1)
    d_row = jnp.arange(kh)[None, :] - jnp.arange(kh)[:, None] + (NA_WIN_H - 1)
    tb = rpb[:, d_row[:, None, :, None], d_col[None, :, None, :]]
    col_start = jnp.clip(col - NA_WIN_W // 2, 0, GRID_W - NA_WIN_W)
    col_in = (col[None, :] >= col_start[:, None]) & (col[None, :] < col_start[:, None] + NA_WIN_W)
    mask = jnp.tile(col_in.astype(F32), (1, kh))
    return tb.reshape(N_HEADS, kh, GRID_W, kh * GRID_W), mask


def neighbourhood_attention(proj, cache_k, cache_v, layer, rpb):
    B, L, _ = proj.shape
    rows = L // GRID_W
    kh = NA_WIN_H
    assert rows >= kh
    Lc = cache_k.shape[2]
    tb, mask = na_bias_table(rpb, kh)
    G = GROUP_W
    return pl.pallas_call(
        functools.partial(_na_kernel, rows=rows, kh=kh),
        grid=(B, rows),
        in_specs=[pl.BlockSpec((1, GRID_W, G), lambda b, r: (b, r, 0)),
                  pl.BlockSpec((1, L, G), lambda b, r: (b, 0, 1)),
                  pl.BlockSpec((1, L, G), lambda b, r: (b, 0, 2)),
                  pl.BlockSpec((1, 1, Lc, G), lambda b, r: (b, layer, 0, 0)),
                  pl.BlockSpec((1, 1, Lc, G), lambda b, r: (b, layer, 0, 0)),
                  pl.BlockSpec((N_HEADS, kh, GRID_W, kh * GRID_W), lambda b, r: (0, 0, 0, 0)),
                  pl.BlockSpec((GRID_W, kh * GRID_W), lambda b, r: (0, 0))],
        out_specs=pl.BlockSpec((1, GRID_W, G), lambda b, r: (b, r, 0)),
        out_shape=jax.ShapeDtypeStruct((B, L, G), F32),
        compiler_params=pltpu.CompilerParams(
            dimension_semantics=("parallel", "arbitrary"), vmem_limit_bytes=VMEM_LIMIT_BYTES),
        name="na_attention",
    )(proj, proj, proj, cache_k, cache_v, tb, mask)


def _ctx_attn_kernel(q_ref, k_ref, v_ref, o_ref):
    q = (q_ref[0] * (HEAD_DIM ** -0.5)).astype(BF16)
    k = k_ref[0].astype(BF16)
    v = v_ref[0].astype(BF16)
    for h in range(N_HEADS):
        sl = slice(h * HEAD_DIM, (h + 1) * HEAD_DIM)
        o_ref[0, :, sl] = _softmax_pv([_dot(q[:, sl], k[:, sl], _NT)], [v[:, sl]])


def context_attention(proj):
    B, L, _ = proj.shape
    G = GROUP_W
    return pl.pallas_call(
        _ctx_attn_kernel,
        grid=(B,),
        in_specs=[pl.BlockSpec((1, L, G), lambda b: (b, 0, 0)),
                  pl.BlockSpec((1, L, G), lambda b: (b, 0, 1)),
                  pl.BlockSpec((1, L, G), lambda b: (b, 0, 2))],
        out_specs=pl.BlockSpec((1, L, G), lambda b: (b, 0, 0)),
        out_shape=jax.ShapeDtypeStruct((B, L, G), F32),
        compiler_params=pltpu.CompilerParams(dimension_semantics=("parallel",), vmem_limit_bytes=VMEM_LIMIT_BYTES),
        name="ctx_attention",
    )(proj, proj, proj)


def rms_norm(x, g):
    return x * lax.rsqrt(jnp.mean(x * x, axis=-1, keepdims=True) + EPS) * g


def layer_norm(x, g, b):
    xc = x - jnp.mean(x, axis=-1, keepdims=True)
    return xc * lax.rsqrt(jnp.mean(xc * xc, axis=-1, keepdims=True) + EPS) * g + b


def head_norm(x, g, eps):
    xc = x - jnp.mean(x, axis=-1, keepdims=True)
    y = xc * lax.rsqrt(jnp.mean(xc * xc, axis=-1, keepdims=True) + eps)
    return y.reshape(x.shape[0], x.shape[1], -1) * g


def dwconv(x, w):
    k, ch = w.shape
    return lax.conv_general_dilated(x, w[:, None, :], window_strides=(1,), padding=[(k // 2, k // 2)],
                                    dimension_numbers=("NWC", "WIO", "NWC"), feature_group_count=ch)


def rev(t):
    return jnp.flip(t, axis=1)


def axial_rope(length):
    t = jnp.arange(length)
    n_freq = HEAD_DIM // 4
    inv = ROPE_BASE ** (-jnp.arange(n_freq, dtype=F32) / n_freq)
    ang = jnp.concatenate([(t // GRID_W).astype(F32)[:, None] * inv,
                           (t % GRID_W).astype(F32)[:, None] * inv], axis=-1)
    return jnp.cos(ang), jnp.sin(ang)


def apply_rope(x, cos, sin):
    x1, x2 = jnp.split(x, 2, axis=-1)
    c, s = cos[None, :, None, :], sin[None, :, None, :]
    return jnp.concatenate([x1 * c - x2 * s, x1 * s + x2 * c], axis=-1)


def retention_scan(q, k, v, log_g, s0):
    B, L, H, dh = q.shape
    n = L // RET_CHUNK

    def chunks(t):
        return t.reshape(B, n, RET_CHUNK, H, dh).transpose(1, 0, 3, 2, 4)

    pos = jnp.arange(RET_CHUNK, dtype=F32)
    diff = pos[:, None] - pos[None, :]
    decay_in = jnp.where(diff >= 0, jnp.exp(jnp.maximum(diff, 0.0) * log_g[:, None, None]), 0.0)
    decay_q = jnp.exp((pos + 1.0) * log_g[:, None])[None, :, :, None]
    decay_k = jnp.exp((RET_CHUNK - 1.0 - pos) * log_g[:, None])[None, :, :, None]
    decay_c = jnp.exp(RET_CHUNK * log_g)[None, :, None, None]

    def step(s, inp):
        qc, kc, vc = inp
        inner = jnp.einsum("bhid,bhjd->bhij", qc, kc) * decay_in
        o = jnp.einsum("bhij,bhjd->bhid", inner, vc) + jnp.einsum("bhid,bhde->bhie", qc, s) * decay_q
        s = s * decay_c + jnp.einsum("bhjd,bhje->bhde", kc * decay_k, vc)
        return s, o

    s, o = lax.scan(step, s0, (chunks(q), chunks(k), chunks(v)))
    return o.transpose(1, 0, 3, 2, 4).reshape(B, L, H, dh), s


def token_mixers(h, p, cache):
    B, L, _ = h.shape
    latent = cache is not None
    proj = matmul(h.reshape(B * L, D_MODEL), p["w_in"]).reshape(B, L, -1)
    (na_q, na_k, na_v, cv_a, cv_b, rt_q, rt_k, rt_v, rt_g,
     rw_r, rw_k, rw_v, rw_w1, rw_a1, rw_g1) = jnp.split(proj, SPLIT_POINTS, axis=-1)

    def heads(t):
        return t.reshape(B, L, N_HEADS, HEAD_DIM)

    scale = HEAD_DIM ** -0.5

    a_k, a_v = heads(na_k), heads(na_v)
    if latent:
        o_a = neighbourhood_attention(proj, cache["na_k"], cache["na_v"], cache["layer"], p["na_rpb"])
    else:
        o_a = context_attention(proj)

    hb = cv_a * jax.nn.sigmoid(cv_b)
    o_b = jax.nn.silu(layer_norm(dwconv(hb, p["conv_dw"]), p["conv_ln_g"], p["conv_ln_b"]))

    c_q, c_k, c_v = heads(rt_q), heads(rt_k) * scale, heads(rt_v)
    if latent:
        cos, sin = axial_rope(L)
        c_q, c_k = apply_rope(c_q, cos, sin), apply_rope(c_k, cos, sin)
        ret0 = cache["ret"]
    else:
        ret0 = jnp.zeros((B, 2, N_HEADS, HEAD_DIM, HEAD_DIM), F32)
    log_g = jax.nn.log_sigmoid(p["ret_decay"])
    o_cf, s_cf = retention_scan(c_q, c_k, c_v, log_g[0], ret0[:, 0])
    o_cb, s_cb = retention_scan(rev(c_q), rev(c_k), rev(c_v), log_g[1], ret0[:, 1])
    o_c = head_norm(o_cf + rev(o_cb), p["ret_gn"], EPS) * jax.nn.silu(rt_g)
    ret_state = jnp.stack([s_cf, s_cb], axis=1)

    d_r, d_k, d_v = jnp.split(dwconv(jnp.concatenate([rw_r, rw_k, rw_v], axis=-1), p["rwkv_shift"]), 3, axis=-1)
    w_low = jnp.tanh(rw_w1)
    gate = jax.nn.sigmoid(rw_g1) @ p["rwkv_g2"]
    kk = heads(d_k * p["rwkv_kk"])
    kk = (kk * lax.rsqrt(jnp.sum(kk * kk, axis=-1, keepdims=True) + 1e-12)).reshape(B, L, GROUP_W)
    rwkv0 = cache["rwkv"] if latent else jnp.zeros((B, 2, N_HEADS, HEAD_DIM, HEAD_DIM), F32)
    lws, ks, As = [], [], []
    for d in range(2):
        z_w = p["rwkv_w0"][d] + w_low @ p["rwkv_w2"][d]
        lws.append(-jnp.exp(-jax.nn.softplus(-z_w) - 0.5))
        a = jax.nn.sigmoid(p["rwkv_a0"][d] + rw_a1 @ p["rwkv_a2"][d])
        ks.append(d_k * (1.0 + (a - 1.0) * p["rwkv_ka"]))
        As.append(a)
    lw2, k2, a2 = jnp.stack(lws), jnp.stack(ks), jnp.stack(As)
    s0t = jnp.swapaxes(rwkv0, -1, -2).transpose(1, 0, 2, 3, 4)
    o2, st = rwkv7_scan(d_r, d_v, kk, lw2, k2, a2, s0t)
    rwkv_state = jnp.swapaxes(st, -1, -2).transpose(1, 0, 2, 3, 4)
    bonus = (jnp.sum(heads(d_r * (k2[0] + k2[1]) * p["rwkv_rk"]), axis=-1, keepdims=True) * heads(d_v))
    o_d = (head_norm(heads(o2[0] + o2[1]), p["rwkv_gn"], RWKV_GN_EPS) + bonus.reshape(B, L, GROUP_W)) * gate

    mixed = matmul(jnp.concatenate([o_a, o_b, o_c, o_d], axis=-1).reshape(B * L, D_MODEL), p["w_out"])
    mixed = mixed.reshape(B, L, D_MODEL)
    if latent:
        return mixed, None
    return mixed, (a_k, a_v, ret_state, rwkv_state)


def conv_ffn(h, w_up, w_conv, w_down):
    B, L, _ = h.shape
    u = dwconv(matmul(h.reshape(B * L, D_MODEL), w_up).reshape(B, L, -1), w_conv)
    g, val = jnp.split(u, 2, axis=-1)
    return matmul((jax.nn.silu(g) * val).reshape(B * L, D_FF), w_down).reshape(B, L, D_MODEL)


def trunk_layer(x, mod, p, cache):
    shift1, scale1, gate1, shift2, scale2, gate2 = jnp.split(mod[:, None, :], 6, axis=-1)
    h = rms_norm(x, p["norm_g"][0]) * (1.0 + scale1) + shift1
    m, state = token_mixers(h, p, cache)
    x = x + gate1 * rms_norm(m, p["norm_g"][1])
    h = rms_norm(x, p["norm_g"][2]) * (1.0 + scale2) + shift2
    f = conv_ffn(h, p["ffn_up"], p["ffn_conv"], p["ffn_down"])
    x = x + gate2 * rms_norm(f, p["norm_g"][3])
    return x, state


def kernel(x_prompt, x_sample, cache_na_k, cache_na_v, state_retention, state_rwkv, c, c_ctx, ada_w, ada_b, norm_g, w_in, w_out, na_rpb, conv_dw, conv_ln_g, conv_ln_b, ret_decay, ret_gn, rwkv_shift, rwkv_w0, rwkv_w2, rwkv_a0, rwkv_a2, rwkv_g2, rwkv_kk, rwkv_ka, rwkv_rk, rwkv_gn, ffn_up, ffn_conv, ffn_down):
    y_prompt, y_sample = x_prompt, x_sample
    new_k, new_v, new_ret, new_rwkv = [], [], [], []
    for l in range(DEPTH):
        p = {"norm_g": norm_g[l], "w_in": w_in[l].astype(BF16), "w_out": w_out[l].astype(BF16), "na_rpb": na_rpb[l],
             "conv_dw": conv_dw[l], "conv_ln_g": conv_ln_g[l], "conv_ln_b": conv_ln_b[l],
             "ret_decay": ret_decay[l], "ret_gn": ret_gn[l], "rwkv_shift": rwkv_shift[l],
             "rwkv_w0": rwkv_w0[l], "rwkv_w2": rwkv_w2[l], "rwkv_a0": rwkv_a0[l], "rwkv_a2": rwkv_a2[l],
             "rwkv_g2": rwkv_g2[l], "rwkv_kk": rwkv_kk[l], "rwkv_ka": rwkv_ka[l], "rwkv_rk": rwkv_rk[l],
             "rwkv_gn": rwkv_gn[l], "ffn_up": ffn_up[l].astype(BF16), "ffn_conv": ffn_conv[l],
             "ffn_down": ffn_down[l].astype(BF16)}
        mod_ctx = (jax.nn.silu(c_ctx) @ ada_w[l] + ada_b[l])[None]
        y_prompt, (k_l, v_l, ret_l, rwkv_l) = trunk_layer(y_prompt, mod_ctx, p, None)
        new_k.append(k_l)
        new_v.append(v_l)
        new_ret.append(ret_l)
        new_rwkv.append(rwkv_l)
        mod_lat = jax.nn.silu(c) @ ada_w[l] + ada_b[l]
        cache_l = {"na_k": cache_na_k.reshape(cache_na_k.shape[:3] + (GROUP_W,)),
                   "na_v": cache_na_v.reshape(cache_na_v.shape[:3] + (GROUP_W,)), "layer": l,
                   "ret": state_retention[:, l], "rwkv": state_rwkv[:, l]}
        y_sample, _ = trunk_layer(y_sample, mod_lat, p, cache_l)
    return (y_prompt, y_sample, jnp.stack(new_k, axis=1), jnp.stack(new_v, axis=1),
            jnp.stack(new_ret, axis=1), jnp.stack(new_rwkv, axis=1))
```

```python
import functools

import jax
import jax.numpy as jnp
import numpy as np
from jax import lax
from jax.experimental import pallas as pl
from jax.experimental.pallas import tpu as pltpu

F32 = jnp.float32
BF16 = jnp.bfloat16

D_MODEL = 1024
DEPTH = 2
GRID_W = 64
N_MIXERS = 4
GROUP_W = D_MODEL // N_MIXERS
HEAD_DIM = 64
N_HEADS = GROUP_W // HEAD_DIM
NA_WIN_H = 8
NA_WIN_W = 16
Q_BLOCK = 128
CONV_W = 31
RET_CHUNK = 128
RWKV_LORA_W = 64
RWKV_LORA_A = 64
RWKV_LORA_G = 128
D_FF = 2816
ROPE_BASE = 10000.0
EPS = 1e-6
RWKV_GN_EPS = 64e-5
NEG_INF = -1e30
SPLIT_SIZES = (GROUP_W,) * 12 + (RWKV_LORA_W, RWKV_LORA_A, RWKV_LORA_G)
SPLIT_POINTS = tuple(int(s) for s in np.cumsum(SPLIT_SIZES)[:-1])

RWKV_CHUNK = 64
VMEM_LIMIT_BYTES = 48 * 1024 * 1024


def _mm_kernel(a_ref, b_ref, o_ref):
    o_ref[...] = jnp.dot(a_ref[...].astype(BF16), b_ref[...], preferred_element_type=F32)


def _pick_tile(n, target):
    best = 128
    for t in range(128, target + 1, 128):
        if n % t == 0:
            best = t
    return best


def matmul(a, b_bf16, tm=512, tn_target=1792):
    m, k = a.shape
    n = b_bf16.shape[1]
    tn = _pick_tile(n, tn_target)
    tm = min(tm, m)
    assert m % tm == 0 and n % tn == 0
    return pl.pallas_call(
        _mm_kernel,
        grid=(n // tn, m // tm),
        in_specs=[pl.BlockSpec((tm, k), lambda j, i: (i, 0)),
                  pl.BlockSpec((k, tn), lambda j, i: (0, j))],
        out_specs=pl.BlockSpec((tm, tn), lambda j, i: (i, j)),
        out_shape=jax.ShapeDtypeStruct((m, n), F32),
        compiler_params=pltpu.CompilerParams(
            dimension_semantics=("parallel", "parallel"), vmem_limit_bytes=VMEM_LIMIT_BYTES),
        name="matmul",
    )(a, b_bf16)


def _dot(a, b, dims):
    return lax.dot_general(a, b, (dims, ((), ())), preferred_element_type=F32)


_NN = ((1,), (0,))
_NT = ((1,), (1,))
_TN = ((0,), (0,))


def _bdot(a, b, dims=_NN):
    return _dot(a.astype(BF16), b.astype(BF16), dims)


def _tri_inverse(n_mat, eye, row, col):
    C = n_mat.shape[0]
    nd = jnp.where((row // 8) == (col // 8), n_mat, 0.0)
    s1 = eye + nd
    p1 = _bdot(nd, nd)
    yield
    prod = _bdot(p1, jnp.concatenate([s1, p1], axis=1))
    yield
    s2, p2 = s1 + prod[:, :C], prod[:, C:]
    d = _bdot(p2, s2)
    yield
    d = s2 + d
    size = 8
    while size < C:
        inner = (row // size) == (col // size)
        outer = (row // (2 * size)) == (col // (2 * size))
        n_off = jnp.where(outer & jnp.logical_not(inner), n_mat, 0.0)
        t = _bdot(n_off, d)
        yield
        t = _bdot(d, t)
        yield
        d = d + t
        size *= 2
    return d


def _interleave(gens):
    results = [None] * len(gens)
    active = list(enumerate(gens))
    while active:
        still = []
        for i, g in active:
            try:
                next(g)
                still.append((i, g))
            except StopIteration as stop:
                results[i] = stop.value
        active = still
    return results


def _rwkv_chain(fwd, r, v, kk, lw, k, a, t0, C):
    v = v.astype(BF16)

    row = lax.broadcasted_iota(jnp.int32, (C, C), 0)
    col = lax.broadcasted_iota(jnp.int32, (C, C), 1)
    incl = (row >= col) if fwd else (row <= col)
    strict = (row > col) if fwd else (row < col)
    incl_bf = jnp.where(incl, 1.0, 0.0).astype(BF16)
    eye = jnp.where(row == col, 1.0, 0.0)

    l_hi = lw.astype(BF16)
    l_lo = (lw - l_hi.astype(F32)).astype(BF16)
    cs = _dot(incl_bf, l_hi, _NN) + _dot(incl_bf, l_lo, _NN)
    yield
    tot = jnp.sum(lw, axis=0, keepdims=True)

    beta = kk * a
    e_out = jnp.exp(-cs)
    e_rem = jnp.exp(tot - cs)
    at = -kk * jnp.exp(cs - lw)
    rt = r * jnp.exp(cs)
    ar = jnp.concatenate([at, rt], axis=0).astype(BF16)
    bk = jnp.concatenate([beta * e_out, k * e_out], axis=0)
    bkh = jnp.concatenate([beta * e_rem, k * e_rem], axis=0)

    big = _bdot(ar, bk, _NT)
    x = _bdot(ar, t0)
    yield
    a_ab = jnp.where(strict, big[:C, :C], 0.0)
    a_ak = jnp.where(strict, big[:C, C:], 0.0)
    row2 = lax.broadcasted_iota(jnp.int32, (C, 2 * C), 0)
    col2 = lax.broadcasted_iota(jnp.int32, (C, 2 * C), 1)
    col2 = jnp.where(col2 >= C, col2 - C, col2)
    incl2 = (row2 >= col2) if fwd else (row2 <= col2)
    a_r = jnp.where(incl2, big[C:], 0.0)
    akv = _bdot(a_ak, v)
    minv = yield from _tri_inverse(a_ab, eye, row, col)

    u = _bdot(minv, x[:C] + akv)
    yield
    uv = jnp.concatenate([u.astype(BF16), v], axis=0)
    o = x[C:] + _bdot(a_r, uv)

    drow = lax.broadcasted_iota(jnp.int32, (HEAD_DIM, HEAD_DIM), 0)
    dcol = lax.broadcasted_iota(jnp.int32, (HEAD_DIM, HEAD_DIM), 1)
    gcol = jnp.sum(jnp.where(drow == dcol, jnp.exp(tot), 0.0), axis=1, keepdims=True)
    return o, gcol * t0 + _bdot(bkh, uv, _TN)


def _rwkv_kernel(rf_ref, vf_ref, kkf_ref, lwf_ref, kf_ref, af_ref, rb_ref, vb_ref, kkb_ref, lwb_ref, kb_ref, ab_ref,
                 s0_ref, of_ref, ob_ref, st_ref, t_scr, *, chunk, n_chunks):
    c = pl.program_id(1)

    @pl.when(c == 0)
    def _():
        t_scr[...] = s0_ref[:, 0].reshape(t_scr.shape)

    t_all = t_scr[...]
    fw = [ref[0] for ref in (rf_ref, vf_ref, kkf_ref)] + [ref[0, 0] for ref in (lwf_ref, kf_ref, af_ref)]
    bw = [ref[0] for ref in (rb_ref, vb_ref, kkb_ref)] + [ref[0, 0] for ref in (lwb_ref, kb_ref, ab_ref)]
    chains = []
    for fwd, tiles, base in ((True, fw, 0), (False, bw, N_HEADS)):
        for h in range(N_HEADS):
            sl = slice(h * HEAD_DIM, (h + 1) * HEAD_DIM)
            chains.append(_rwkv_chain(fwd, *[t[:, sl] for t in tiles], t_all[base + h], chunk))
    res = _interleave(chains)
    of_ref[0] = jnp.concatenate([o for o, _ in res[:N_HEADS]], axis=1)
    ob_ref[0] = jnp.concatenate([o for o, _ in res[N_HEADS:]], axis=1)
    t_scr[...] = jnp.stack([t for _, t in res])

    @pl.when(c == n_chunks - 1)
    def _():
        st_ref[:, 0] = t_scr[...].reshape(2, N_HEADS, HEAD_DIM, HEAD_DIM)


def rwkv7_scan(r, v, kk, lw, k, a, s0t, chunk=RWKV_CHUNK):
    B, L, G = r.shape
    n = L // chunk
    assert L % chunk == 0
    sh_f = pl.BlockSpec((1, chunk, G), lambda b, c: (b, c, 0))
    sh_b = pl.BlockSpec((1, chunk, G), lambda b, c: (b, n - 1 - c, 0))
    pd_f = pl.BlockSpec((1, 1, chunk, G), lambda b, c: (0, b, c, 0))
    pd_b = pl.BlockSpec((1, 1, chunk, G), lambda b, c: (1, b, n - 1 - c, 0))
    state = pl.BlockSpec((2, 1, N_HEADS, HEAD_DIM, HEAD_DIM), lambda b, c: (0, b, 0, 0, 0))
    return pl.pallas_call(
        functools.partial(_rwkv_kernel, chunk=chunk, n_chunks=n),
        grid=(B, n),
        in_specs=[sh_f, sh_f, sh_f, pd_f, pd_f, pd_f, sh_b, sh_b, sh_b, pd_b, pd_b, pd_b, state],
        out_specs=[sh_f, sh_b, state],
        out_shape=[jax.ShapeDtypeStruct((B, L, G), F32), jax.ShapeDtypeStruct((B, L, G), F32),
                   jax.ShapeDtypeStruct((2, B, N_HEADS, HEAD_DIM, HEAD_DIM), F32)],
        scratch_shapes=[pltpu.VMEM((2 * N_HEADS, HEAD_DIM, HEAD_DIM), F32)],
        compiler_params=pltpu.CompilerParams(
            dimension_semantics=("parallel", "arbitrary"), vmem_limit_bytes=VMEM_LIMIT_BYTES),
        name="rwkv7_scan",
    )(r, v, kk, lw, k, a, r, v, kk, lw, k, a, s0t)


def _ret_chain(fwd, h, q, k, v, lg, o_ref, s_scr, C):
    sl = slice(h * HEAD_DIM, (h + 1) * HEAD_DIM)
    row = lax.broadcasted_iota(jnp.int32, (C, C), 0)
    col = lax.broadcasted_iota(jnp.int32, (C, C), 1)
    dist = ((row - col) if fwd else (col - row)).astype(F32)
    decay_in = jnp.where(dist >= 0, jnp.exp(jnp.maximum(dist, 0.0) * lg), 0.0)
    pos = lax.broadcasted_iota(jnp.int32, (C, 1), 0).astype(F32)
    step = pos if fwd else (C - 1.0) - pos
    decay_q = jnp.exp((step + 1.0) * lg)
    decay_k = jnp.exp((C - 1.0 - step) * lg)
    qh, kh, vh = q[:, sl], k[:, sl], v[:, sl].astype(BF16)
    si = h if fwd else N_HEADS + h
    s0 = s_scr[si]
    qk = _bdot(qh, kh, _NT)
    qs = _bdot(qh, s0)
    kv = _bdot(kh * decay_k, vh, _TN)
    yield
    ov = _bdot(qk * decay_in, vh)
    yield
    o_ref[0, :, sl] = ov + qs * decay_q
    s_scr[si] = s0 * jnp.exp(C * lg) + kv


def _rope(x, cos, sin):
    lane = lax.broadcasted_iota(jnp.int32, x.shape, 1)
    n = x.shape[1]
    swapped = jnp.where((lane % HEAD_DIM) < HEAD_DIM // 2,
                        pltpu.roll(x, n - HEAD_DIM // 2, axis=1), pltpu.roll(x, HEAD_DIM // 2, axis=1))
    return x * cos + swapped * sin


def _ret_kernel(lg_ref, qf_ref, kf_ref, vf_ref, qb_ref, kb_ref, vb_ref, cf_ref, sf_ref, cb_ref, sb_ref, s0_ref,
                of_ref, ob_ref, st_ref, s_scr, *, chunk, n_chunks, rope):
    c = pl.program_id(1)

    @pl.when(c == 0)
    def _():
        s_scr[...] = s0_ref[0].reshape(s_scr.shape)

    scale = HEAD_DIM ** -0.5
    qf, kf, qb, kb = qf_ref[0], kf_ref[0] * scale, qb_ref[0], kb_ref[0] * scale
    if rope:
        qf, kf = _rope(qf, cf_ref[...], sf_ref[...]), _rope(kf, cf_ref[...], sf_ref[...])
        qb, kb = _rope(qb, cb_ref[...], sb_ref[...]), _rope(kb, cb_ref[...], sb_ref[...])
    vf, vb = vf_ref[0], vb_ref[0]
    _interleave([_ret_chain(True, h, qf, kf, vf, lg_ref[0, h], of_ref, s_scr, chunk) for h in range(N_HEADS)]
                + [_ret_chain(False, h, qb, kb, vb, lg_ref[1, h], ob_ref, s_scr, chunk) for h in range(N_HEADS)])

    @pl.when(c == n_chunks - 1)
    def _():
        st_ref[0] = s_scr[...].reshape(2, N_HEADS, HEAD_DIM, HEAD_DIM)


def rope_tables(length):
    t = jnp.arange(length)
    n_freq = HEAD_DIM // 4
    inv = ROPE_BASE ** (-jnp.arange(n_freq, dtype=F32) / n_freq)
    ang = jnp.concatenate([(t // GRID_W).astype(F32)[:, None] * inv,
                           (t % GRID_W).astype(F32)[:, None] * inv], axis=-1)
    cos, sin = jnp.cos(ang), jnp.sin(ang)
    return (jnp.tile(jnp.concatenate([cos, cos], axis=-1), (1, N_HEADS)),
            jnp.tile(jnp.concatenate([-sin, sin], axis=-1), (1, N_HEADS)))


def retention(proj, ret_decay, s0, rope, chunk=RET_CHUNK):
    B, L, _ = proj.shape
    G = GROUP_W
    n = L // chunk
    assert L % chunk == 0
    log_g = jax.nn.log_sigmoid(ret_decay)
    cos, sin = rope_tables(L)
    fw = lambda j: pl.BlockSpec((1, chunk, G), lambda b, c: (b, c, j))
    bw = lambda j: pl.BlockSpec((1, chunk, G), lambda b, c: (b, n - 1 - c, j))
    tab_f = pl.BlockSpec((chunk, G), lambda b, c: (c, 0))
    tab_b = pl.BlockSpec((chunk, G), lambda b, c: (n - 1 - c, 0))
    state = pl.BlockSpec((1, 2, N_HEADS, HEAD_DIM, HEAD_DIM), lambda b, c: (b, 0, 0, 0, 0))
    return pl.pallas_call(
        functools.partial(_ret_kernel, chunk=chunk, n_chunks=n, rope=rope),
        grid=(B, n),
        in_specs=[pl.BlockSpec(memory_space=pltpu.SMEM), fw(5), fw(6), fw(7), bw(5), bw(6), bw(7),
                  tab_f, tab_f, tab_b, tab_b, state],
        out_specs=[pl.BlockSpec((1, chunk, G), lambda b, c: (b, c, 0)),
                   pl.BlockSpec((1, chunk, G), lambda b, c: (b, n - 1 - c, 0)), state],
        out_shape=[jax.ShapeDtypeStruct((B, L, G), F32), jax.ShapeDtypeStruct((B, L, G), F32),
                   jax.ShapeDtypeStruct((B, 2, N_HEADS, HEAD_DIM, HEAD_DIM), F32)],
        scratch_shapes=[pltpu.VMEM((2 * N_HEADS, HEAD_DIM, HEAD_DIM), F32)],
        compiler_params=pltpu.CompilerParams(
            dimension_semantics=("parallel", "arbitrary"), vmem_limit_bytes=VMEM_LIMIT_BYTES),
        name="retention",
    )(log_g, proj, proj, proj, proj, proj, proj, cos, sin, cos, sin, s0)


def _softmax_pv(s_list, v_list):
    m = s_list[0].max(axis=-1, keepdims=True)
    for s in s_list[1:]:
        m = jnp.maximum(m, s.max(axis=-1, keepdims=True))
    den = 0.0
    acc = 0.0
    for s, v in zip(s_list, v_list):
        p = jnp.exp(s - m)
        den = den + p.sum(axis=-1, keepdims=True)
        acc = acc + _dot(p.astype(BF16), v, _NN)
    yield
    return acc / den


def _na_kernel(q_ref, k_ref, v_ref, ck_ref, cv_ref, tb_ref, mask_ref, o_ref, *, rows, kh):
    r = pl.program_id(1)
    rs = jnp.clip(r - kh // 2, 0, rows - kh)
    pat = r - rs
    start = pl.multiple_of(rs * GRID_W, GRID_W)
    kwin = k_ref[0, pl.ds(start, kh * GRID_W), :].astype(BF16)
    vwin = v_ref[0, pl.ds(start, kh * GRID_W), :].astype(BF16)
    ck = ck_ref[0, 0].astype(BF16)
    cv = cv_ref[0, 0].astype(BF16)
    q = (q_ref[0] * (HEAD_DIM ** -0.5)).astype(BF16)
    mask = mask_ref[...] > 0.0
    def head(h):
        sl = slice(h * HEAD_DIM, (h + 1) * HEAD_DIM)
        qh = q[:, sl]
        s_loc = _dot(qh, kwin[:, sl], _NT)
        s_ctx = _dot(qh, ck[:, sl], _NT)
        yield
        s_loc = jnp.where(mask, s_loc + tb_ref[h, pat], NEG_INF)
        return (yield from _softmax_pv([s_loc, s_ctx], [vwin[:, sl], cv[:, sl]]))

    o_ref[0] = jnp.concatenate(_interleave([head(h) for h in range(N_HEADS)]), axis=1)


def na_bias_table(rpb, kh):
    col = np.arange(GRID_W)
    d_col = np.clip(col[None, :] - col[:, None], -(NA_WIN_W - 1), NA_WIN_W - 1) + (NA_WIN_W - 1)
    onehot = jnp.asarray(d_col[:, :, None] == np.arange(2 * NA_WIN_W - 1), F32)
    rows = jnp.stack([rpb[:, NA_WIN_H - 1 - p:NA_WIN_H - 1 - p + kh] for p in range(kh)], axis=1)
    tb = jnp.einsum("hpic,qkc->hpqik", rows, onehot, precision=lax.Precision.HIGHEST)
    col_start = np.clip(col - NA_WIN_W // 2, 0, GRID_W - NA_WIN_W)
    col_in = (col[None, :] >= col_start[:, None]) & (col[None, :] < col_start[:, None] + NA_WIN_W)
    mask = jnp.asarray(np.tile(col_in.astype(np.float32), (1, kh)))
    return tb.reshape(N_HEADS, kh, GRID_W, kh * GRID_W), mask


def neighbourhood_attention(proj, cache_k, cache_v, layer, rpb):
    B, L, _ = proj.shape
    rows = L // GRID_W
    kh = NA_WIN_H
    assert rows >= kh
    Lc = cache_k.shape[2]
    tb, mask = na_bias_table(rpb, kh)
    G = GROUP_W
    return pl.pallas_call(
        functools.partial(_na_kernel, rows=rows, kh=kh),
        grid=(B, rows),
        in_specs=[pl.BlockSpec((1, GRID_W, G), lambda b, r: (b, r, 0)),
                  pl.BlockSpec((1, L, G), lambda b, r: (b, 0, 1)),
                  pl.BlockSpec((1, L, G), lambda b, r: (b, 0, 2)),
                  pl.BlockSpec((1, 1, Lc, G), lambda b, r: (b, layer, 0, 0)),
                  pl.BlockSpec((1, 1, Lc, G), lambda b, r: (b, layer, 0, 0)),
                  pl.BlockSpec((N_HEADS, kh, GRID_W, kh * GRID_W), lambda b, r: (0, 0, 0, 0)),
                  pl.BlockSpec((GRID_W, kh * GRID_W), lambda b, r: (0, 0))],
        out_specs=pl.BlockSpec((1, GRID_W, G), lambda b, r: (b, r, 0)),
        out_shape=jax.ShapeDtypeStruct((B, L, G), F32),
        compiler_params=pltpu.CompilerParams(
            dimension_semantics=("parallel", "arbitrary"), vmem_limit_bytes=VMEM_LIMIT_BYTES),
        name="na_attention",
    )(proj, proj, proj, cache_k, cache_v, tb, mask)


def _ctx_attn_kernel(q_ref, k_ref, v_ref, o_ref):
    q = (q_ref[0] * (HEAD_DIM ** -0.5)).astype(BF16)
    k = k_ref[0].astype(BF16)
    v = v_ref[0].astype(BF16)
    def head(h):
        sl = slice(h * HEAD_DIM, (h + 1) * HEAD_DIM)
        s = _dot(q[:, sl], k[:, sl], _NT)
        yield
        return (yield from _softmax_pv([s], [v[:, sl]]))

    o_ref[0] = jnp.concatenate(_interleave([head(h) for h in range(N_HEADS)]), axis=1)


def context_attention(proj):
    B, L, _ = proj.shape
    G = GROUP_W
    return pl.pallas_call(
        _ctx_attn_kernel,
        grid=(B,),
        in_specs=[pl.BlockSpec((1, L, G), lambda b: (b, 0, 0)),
                  pl.BlockSpec((1, L, G), lambda b: (b, 0, 1)),
                  pl.BlockSpec((1, L, G), lambda b: (b, 0, 2))],
        out_specs=pl.BlockSpec((1, L, G), lambda b: (b, 0, 0)),
        out_shape=jax.ShapeDtypeStruct((B, L, G), F32),
        compiler_params=pltpu.CompilerParams(dimension_semantics=("parallel",), vmem_limit_bytes=VMEM_LIMIT_BYTES),
        name="ctx_attention",
    )(proj, proj, proj)


def rms_norm(x, g):
    return x * lax.rsqrt(jnp.mean(x * x, axis=-1, keepdims=True) + EPS) * g


def layer_norm(x, g, b):
    xc = x - jnp.mean(x, axis=-1, keepdims=True)
    return xc * lax.rsqrt(jnp.mean(xc * xc, axis=-1, keepdims=True) + EPS) * g + b


def head_norm(x, g, eps):
    xc = x - jnp.mean(x, axis=-1, keepdims=True)
    y = xc * lax.rsqrt(jnp.mean(xc * xc, axis=-1, keepdims=True) + eps)
    return y.reshape(x.shape[0], x.shape[1], -1) * g


def dwconv(x, w):
    k, ch = w.shape
    return lax.conv_general_dilated(x, w[:, None, :], window_strides=(1,), padding=[(k // 2, k // 2)],
                                    dimension_numbers=("NWC", "WIO", "NWC"), feature_group_count=ch)


def token_mixers(h, p, cache):
    B, L, _ = h.shape
    latent = cache is not None
    proj = matmul(h.reshape(B * L, D_MODEL), p["w_in"]).reshape(B, L, -1)
    (na_q, na_k, na_v, cv_a, cv_b, rt_q, rt_k, rt_v, rt_g,
     rw_r, rw_k, rw_v, rw_w1, rw_a1, rw_g1) = jnp.split(proj, SPLIT_POINTS, axis=-1)

    def heads(t):
        return t.reshape(B, L, N_HEADS, HEAD_DIM)

    a_k, a_v = heads(na_k), heads(na_v)
    if latent:
        o_a = neighbourhood_attention(proj, cache["na_k"], cache["na_v"], cache["layer"], p["na_rpb"])
    else:
        o_a = context_attention(proj)

    hb = cv_a * jax.nn.sigmoid(cv_b)
    o_b = jax.nn.silu(layer_norm(dwconv(hb, p["conv_dw"]), p["conv_ln_g"], p["conv_ln_b"]))

    ret0 = cache["ret"] if latent else jnp.zeros((B, 2, N_HEADS, HEAD_DIM, HEAD_DIM), F32)
    o_cf, o_cb, ret_state = retention(proj, p["ret_decay"], ret0, rope=latent)
    o_c = head_norm(heads(o_cf + o_cb), p["ret_gn"], EPS) * jax.nn.silu(rt_g)

    d_r, d_k, d_v = jnp.split(dwconv(jnp.concatenate([rw_r, rw_k, rw_v], axis=-1), p["rwkv_shift"]), 3, axis=-1)
    w_low = jnp.tanh(rw_w1)
    gate = jax.nn.sigmoid(rw_g1) @ p["rwkv_g2"]
    kk = heads(d_k * p["rwkv_kk"])
    kk = (kk * lax.rsqrt(jnp.sum(kk * kk, axis=-1, keepdims=True) + 1e-12)).reshape(B, L, GROUP_W)
    rwkv0 = cache["rwkv"] if latent else jnp.zeros((B, 2, N_HEADS, HEAD_DIM, HEAD_DIM), F32)
    lws, ks, As = [], [], []
    for d in range(2):
        z_w = p["rwkv_w0"][d] + w_low @ p["rwkv_w2"][d]
        lws.append(-jnp.exp(-jax.nn.softplus(-z_w) - 0.5))
        a = jax.nn.sigmoid(p["rwkv_a0"][d] + rw_a1 @ p["rwkv_a2"][d])
        ks.append(d_k * (1.0 + (a - 1.0) * p["rwkv_ka"]))
        As.append(a)
    lw2, k2, a2 = jnp.stack(lws), jnp.stack(ks), jnp.stack(As)
    s0t = jnp.swapaxes(rwkv0, -1, -2).transpose(1, 0, 2, 3, 4)
    o_f, o_bw, st = rwkv7_scan(d_r, d_v, kk, lw2, k2, a2, s0t)
    rwkv_state = jnp.swapaxes(st, -1, -2).transpose(1, 0, 2, 3, 4)
    bonus = (jnp.sum(heads(d_r * (k2[0] + k2[1]) * p["rwkv_rk"]), axis=-1, keepdims=True) * heads(d_v))
    o_d = (head_norm(heads(o_f + o_bw), p["rwkv_gn"], RWKV_GN_EPS) + bonus.reshape(B, L, GROUP_W)) * gate

    mixed = matmul(jnp.concatenate([o_a, o_b, o_c, o_d], axis=-1).reshape(B * L, D_MODEL), p["w_out"])
    mixed = mixed.reshape(B, L, D_MODEL)
    if latent:
        return mixed, None
    return mixed, (a_k, a_v, ret_state, rwkv_state)


def conv_ffn(h, w_up, w_conv, w_down):
    B, L, _ = h.shape
    u = dwconv(matmul(h.reshape(B * L, D_MODEL), w_up).reshape(B, L, -1), w_conv)
    g, val = jnp.split(u, 2, axis=-1)
    return matmul((jax.nn.silu(g) * val).reshape(B * L, D_FF), w_down).reshape(B, L, D_MODEL)


def trunk_layer(x, mod, p, cache):
    shift1, scale1, gate1, shift2, scale2, gate2 = jnp.split(mod[:, None, :], 6, axis=-1)
    h = rms_norm(x, p["norm_g"][0]) * (1.0 + scale1) + shift1
    m, state = token_mixers(h, p, cache)
    x = x + gate1 * rms_norm(m, p["norm_g"][1])
    h = rms_norm(x, p["norm_g"][2]) * (1.0 + scale2) + shift2
    f = conv_ffn(h, p["ffn_up"], p["ffn_conv"], p["ffn_down"])
    x = x + gate2 * rms_norm(f, p["norm_g"][3])
    return x, state


def kernel(x_prompt, x_sample, cache_na_k, cache_na_v, state_retention, state_rwkv, c, c_ctx, ada_w, ada_b, norm_g, w_in, w_out, na_rpb, conv_dw, conv_ln_g, conv_ln_b, ret_decay, ret_gn, rwkv_shift, rwkv_w0, rwkv_w2, rwkv_a0, rwkv_a2, rwkv_g2, rwkv_kk, rwkv_ka, rwkv_rk, rwkv_gn, ffn_up, ffn_conv, ffn_down):
    y_prompt, y_sample = x_prompt, x_sample
    new_k, new_v, new_ret, new_rwkv = [], [], [], []
    for l in range(DEPTH):
        p = {"norm_g": norm_g[l], "w_in": w_in[l].astype(BF16), "w_out": w_out[l].astype(BF16), "na_rpb": na_rpb[l],
             "conv_dw": conv_dw[l], "conv_ln_g": conv_ln_g[l], "conv_ln_b": conv_ln_b[l],
             "ret_decay": ret_decay[l], "ret_gn": ret_gn[l], "rwkv_shift": rwkv_shift[l],
             "rwkv_w0": rwkv_w0[l], "rwkv_w2": rwkv_w2[l], "rwkv_a0": rwkv_a0[l], "rwkv_a2": rwkv_a2[l],
             "rwkv_g2": rwkv_g2[l], "rwkv_kk": rwkv_kk[l], "rwkv_ka": rwkv_ka[l], "rwkv_rk": rwkv_rk[l],
             "rwkv_gn": rwkv_gn[l], "ffn_up": ffn_up[l].astype(BF16), "ffn_conv": ffn_conv[l],
             "ffn_down": ffn_down[l].astype(BF16)}
        mod_ctx = (jax.nn.silu(c_ctx) @ ada_w[l] + ada_b[l])[None]
        y_prompt, (k_l, v_l, ret_l, rwkv_l) = trunk_layer(y_prompt, mod_ctx, p, None)
        new_k.append(k_l)
        new_v.append(v_l)
        new_ret.append(ret_l)
        new_rwkv.append(rwkv_l)
        mod_lat = jax.nn.silu(c) @ ada_w[l] + ada_b[l]
        cache_l = {"na_k": cache_na_k.reshape(cache_na_k.shape[:3] + (GROUP_W,)),
                   "na_v": cache_na_v.reshape(cache_na_v.shape[:3] + (GROUP_W,)), "layer": l,
                   "ret": state_retention[:, l], "rwkv": state_rwkv[:, l]}
        y_sample, _ = trunk_layer(y_sample, mod_lat, p, cache_l)
    return (y_prompt, y_sample, jnp.stack(new_k, axis=1), jnp.stack(new_v, axis=1),
            jnp.stack(new_ret, axis=1), jnp.stack(new_rwkv, axis=1))
```

```python
import functools

import jax
import jax.numpy as jnp
import numpy as np
from jax import lax
from jax.experimental import pallas as pl
from jax.experimental.pallas import tpu as pltpu

F32 = jnp.float32
BF16 = jnp.bfloat16

D_MODEL = 1024
DEPTH = 2
GRID_W = 64
N_MIXERS = 4
GROUP_W = D_MODEL // N_MIXERS
HEAD_DIM = 64
N_HEADS = GROUP_W // HEAD_DIM
NA_WIN_H = 8
NA_WIN_W = 16
Q_BLOCK = 128
CONV_W = 31
RET_CHUNK = 128
RWKV_LORA_W = 64
RWKV_LORA_A = 64
RWKV_LORA_G = 128
D_FF = 2816
ROPE_BASE = 10000.0
EPS = 1e-6
RWKV_GN_EPS = 64e-5
NEG_INF = -1e30
SPLIT_SIZES = (GROUP_W,) * 12 + (RWKV_LORA_W, RWKV_LORA_A, RWKV_LORA_G)
SPLIT_POINTS = tuple(int(s) for s in np.cumsum(SPLIT_SIZES)[:-1])

RWKV_CHUNK = 64
VMEM_LIMIT_BYTES = 48 * 1024 * 1024
SUBLANES = 8
MOD_ROWS = 8
ROW_TILE = 512
FFN_CHUNK = 1408


def _mm_kernel(a_ref, b_ref, o_ref):
    o_ref[...] = jnp.dot(a_ref[...].astype(BF16), b_ref[...], preferred_element_type=F32)


def _pick_tile(n, target):
    best = 128
    for t in range(128, target + 1, 128):
        if n % t == 0:
            best = t
    return best


def matmul(a, b_bf16, tm=512, tn_target=1792):
    m, k = a.shape
    n = b_bf16.shape[1]
    tn = _pick_tile(n, tn_target)
    tm = min(tm, m)
    assert m % tm == 0 and n % tn == 0
    return pl.pallas_call(
        _mm_kernel,
        grid=(n // tn, m // tm),
        in_specs=[pl.BlockSpec((tm, k), lambda j, i: (i, 0)),
                  pl.BlockSpec((k, tn), lambda j, i: (0, j))],
        out_specs=pl.BlockSpec((tm, tn), lambda j, i: (i, j)),
        out_shape=jax.ShapeDtypeStruct((m, n), F32),
        compiler_params=pltpu.CompilerParams(
            dimension_semantics=("parallel", "parallel"), vmem_limit_bytes=VMEM_LIMIT_BYTES),
        name="matmul",
    )(a, b_bf16)


def _rms(x, g):
    return x * lax.rsqrt(jnp.mean(x * x, axis=-1, keepdims=True) + EPS) * g


def _resident(shape):
    return pl.BlockSpec(shape, lambda *_: (0,) * len(shape), pipeline_mode=pl.Buffered(1))


def _mod_spec(tm, rows_per_mod):
    return pl.BlockSpec((1, MOD_ROWS, D_MODEL), lambda i: ((i * tm) // rows_per_mod, 0, 0))


def _norm_mm_kernel(x_ref, mod_ref, g_ref, w_ref, o_ref):
    mod = mod_ref[0]
    h = _rms(x_ref[...], g_ref[0:1]) * (1.0 + mod[1:2]) + mod[0:1]
    o_ref[...] = jnp.dot(h.astype(BF16), w_ref[...], preferred_element_type=F32)


def norm_matmul(x, mod, rows_per_mod, norm_g, w_bf16, tm=ROW_TILE):
    m, n = x.shape[0], w_bf16.shape[1]
    tm = min(tm, rows_per_mod)
    return pl.pallas_call(
        _norm_mm_kernel,
        grid=(m // tm,),
        in_specs=[pl.BlockSpec((tm, D_MODEL), lambda i: (i, 0)), _mod_spec(tm, rows_per_mod),
                  _resident(norm_g.shape), _resident(w_bf16.shape)],
        out_specs=pl.BlockSpec((tm, n), lambda i: (i, 0)),
        out_shape=jax.ShapeDtypeStruct((m, n), F32),
        compiler_params=pltpu.CompilerParams(dimension_semantics=("parallel",), vmem_limit_bytes=VMEM_LIMIT_BYTES),
        name="norm_matmul",
    )(x, mod, norm_g, w_bf16)


def _mm_res_kernel(a_ref, x_ref, mod_ref, g_ref, w_ref, o_ref):
    m = jnp.dot(a_ref[...].astype(BF16), w_ref[...], preferred_element_type=F32)
    o_ref[...] = x_ref[...] + mod_ref[0][2:3] * _rms(m, g_ref[1:2])


def matmul_residual(a, x, mod, rows_per_mod, norm_g, w_bf16, tm=ROW_TILE):
    m, k = a.shape
    tm = min(tm, rows_per_mod)
    return pl.pallas_call(
        _mm_res_kernel,
        grid=(m // tm,),
        in_specs=[pl.BlockSpec((tm, k), lambda i: (i, 0)), pl.BlockSpec((tm, D_MODEL), lambda i: (i, 0)),
                  _mod_spec(tm, rows_per_mod), _resident(norm_g.shape), _resident(w_bf16.shape)],
        out_specs=pl.BlockSpec((tm, D_MODEL), lambda i: (i, 0)),
        out_shape=jax.ShapeDtypeStruct((m, D_MODEL), F32),
        compiler_params=pltpu.CompilerParams(dimension_semantics=("parallel",), vmem_limit_bytes=VMEM_LIMIT_BYTES),
        name="matmul_residual",
    )(a, x, mod, norm_g, w_bf16)


def _halo_specs(tm, m, width, col=0):
    blocks = tm // SUBLANES
    return (pl.BlockSpec((SUBLANES, width), lambda i: (jnp.maximum(i * blocks - 1, 0), col)),
            pl.BlockSpec((SUBLANES, width), lambda i: (jnp.minimum((i + 1) * blocks, m // SUBLANES - 1), col)))


def _shift_rows(u, prev_row, next_row):
    tm = u.shape[0]
    row = lax.broadcasted_iota(jnp.int32, (tm, 1), 0)
    u_prev = jnp.where(row == 0, prev_row, pltpu.roll(u, 1, axis=0))
    u_next = jnp.where(row == tm - 1, next_row, pltpu.roll(u, tm - 1, axis=0))
    return u_prev, u_next


def _ffn_kernel(x_ref, xp_ref, xn_ref, mod_ref, g_ref, up_ref, cw_ref, down_ref, o_ref, *, seq_tiles, cw):
    i = pl.program_id(0)
    keep_prev = jnp.where(i % seq_tiles == 0, 0.0, 1.0)
    keep_next = jnp.where(i % seq_tiles == seq_tiles - 1, 0.0, 1.0)
    mod = mod_ref[0]
    g2, g3 = g_ref[2:3], g_ref[3:4]

    def pre(x):
        return (_rms(x, g2) * (1.0 + mod[4:5]) + mod[3:4]).astype(BF16)

    x = x_ref[...]
    h = pre(x)
    hh = pre(jnp.concatenate([xp_ref[...], xn_ref[...]], axis=0))
    acc = jnp.zeros(x.shape, F32)
    for j in range(D_FF // cw):
        conv = []
        for half in range(2):
            cols = slice(half * D_FF + j * cw, half * D_FF + (j + 1) * cw)
            w_up = up_ref[:, cols]
            u = jnp.dot(h, w_up, preferred_element_type=F32)
            uh = jnp.dot(hh, w_up, preferred_element_type=F32)
            u_prev, u_next = _shift_rows(u, uh[SUBLANES - 1:SUBLANES] * keep_prev, uh[SUBLANES:SUBLANES + 1] * keep_next)
            wc = cw_ref[:, cols]
            conv.append(wc[0:1] * u_prev + wc[1:2] * u + wc[2:3] * u_next)
        act = conv[0] * jax.nn.sigmoid(conv[0]) * conv[1]
        acc = acc + jnp.dot(act.astype(BF16), down_ref[j * cw:(j + 1) * cw, :], preferred_element_type=F32)
    o_ref[...] = x + mod[5:6] * _rms(acc, g3)


def ffn_block(x, mod, rows_per_mod, seq_len, norm_g, up_bf16, w_conv, down_bf16, tm=ROW_TILE, cw=FFN_CHUNK):
    m = x.shape[0]
    tm = min(tm, seq_len)
    assert seq_len % tm == 0 and D_FF % cw == 0
    prev_spec, next_spec = _halo_specs(tm, m, D_MODEL)
    return pl.pallas_call(
        functools.partial(_ffn_kernel, seq_tiles=seq_len // tm, cw=cw),
        grid=(m // tm,),
        in_specs=[pl.BlockSpec((tm, D_MODEL), lambda i: (i, 0)), prev_spec, next_spec, _mod_spec(tm, rows_per_mod),
                  _resident(norm_g.shape), _resident(up_bf16.shape), _resident(w_conv.shape),
                  _resident(down_bf16.shape)],
        out_specs=pl.BlockSpec((tm, D_MODEL), lambda i: (i, 0)),
        out_shape=jax.ShapeDtypeStruct((m, D_MODEL), F32),
        compiler_params=pltpu.CompilerParams(dimension_semantics=("parallel",), vmem_limit_bytes=VMEM_LIMIT_BYTES),
        name="ffn_block",
    )(x, x, x, mod, norm_g, up_bf16, w_conv, down_bf16)


def _conv_module_kernel(a_ref, b_ref, ap_ref, bp_ref, an_ref, bn_ref, w_ref, ln_ref, o_ref, pad_scr, *, seq_tiles):
    i = pl.program_id(0)
    tm = a_ref.shape[0]
    halo = 2 * SUBLANES
    keep_prev = jnp.where(i % seq_tiles == 0, 0.0, 1.0)
    keep_next = jnp.where(i % seq_tiles == seq_tiles - 1, 0.0, 1.0)

    def glu(a, b):
        return a * jax.nn.sigmoid(b)

    pad_scr[0:halo] = glu(ap_ref[...], bp_ref[...]) * keep_prev
    pad_scr[halo:halo + tm] = glu(a_ref[...], b_ref[...])
    pad_scr[halo + tm:2 * halo + tm] = glu(an_ref[...], bn_ref[...]) * keep_next
    acc = jnp.zeros((tm, GROUP_W), F32)
    for j in range(CONV_W):
        off = halo - CONV_W // 2 + j
        acc = acc + w_ref[j:j + 1] * pad_scr[off:off + tm]
    xc = acc - jnp.mean(acc, axis=-1, keepdims=True)
    y = xc * lax.rsqrt(jnp.mean(xc * xc, axis=-1, keepdims=True) + EPS) * ln_ref[0:1] + ln_ref[1:2]
    o_ref[...] = y * jax.nn.sigmoid(y)


def conv_module(proj2d, seq_len, w_dw, ln_g, ln_b, tm=ROW_TILE):
    m = proj2d.shape[0]
    tm = min(tm, seq_len)
    G = GROUP_W
    halo = 2 * SUBLANES
    assert CONV_W // 2 <= halo and seq_len % tm == 0
    blocks = tm // halo

    def prev(col):
        return pl.BlockSpec((halo, G), lambda i: (jnp.maximum(i * blocks - 1, 0), col))

    def nxt(col):
        return pl.BlockSpec((halo, G), lambda i: (jnp.minimum((i + 1) * blocks, m // halo - 1), col))

    ln = jnp.stack([ln_g, ln_b])
    return pl.pallas_call(
        functools.partial(_conv_module_kernel, seq_tiles=seq_len // tm),
        grid=(m // tm,),
        in_specs=[pl.BlockSpec((tm, G), lambda i: (i, 3)), pl.BlockSpec((tm, G), lambda i: (i, 4)),
                  prev(3), prev(4), nxt(3), nxt(4), _resident(w_dw.shape), _resident(ln.shape)],
        out_specs=pl.BlockSpec((tm, G), lambda i: (i, 0)),
        out_shape=jax.ShapeDtypeStruct((m, G), F32),
        scratch_shapes=[pltpu.VMEM((tm + 2 * halo, G), F32)],
        compiler_params=pltpu.CompilerParams(dimension_semantics=("parallel",), vmem_limit_bytes=VMEM_LIMIT_BYTES),
        name="conv_module",
    )(proj2d, proj2d, proj2d, proj2d, proj2d, proj2d, w_dw, ln)


def _dot(a, b, dims):
    return lax.dot_general(a, b, (dims, ((), ())), preferred_element_type=F32)


_NN = ((1,), (0,))
_NT = ((1,), (1,))
_TN = ((0,), (0,))


def _bdot(a, b, dims=_NN):
    return _dot(a.astype(BF16), b.astype(BF16), dims)


def _tri_inverse(n_mat, eye, row, col):
    C = n_mat.shape[0]
    nd = jnp.where((row // 8) == (col // 8), n_mat, 0.0)
    s1 = eye + nd
    p1 = _bdot(nd, nd)
    yield
    prod = _bdot(p1, jnp.concatenate([s1, p1], axis=1))
    yield
    s2, p2 = s1 + prod[:, :C], prod[:, C:]
    d = _bdot(p2, s2)
    yield
    d = s2 + d
    size = 8
    while size < C:
        inner = (row // size) == (col // size)
        outer = (row // (2 * size)) == (col // (2 * size))
        n_off = jnp.where(outer & jnp.logical_not(inner), n_mat, 0.0)
        t = _bdot(n_off, d)
        yield
        t = _bdot(d, t)
        yield
        d = d + t
        size *= 2
    return d


def _interleave(gens):
    results = [None] * len(gens)
    active = list(enumerate(gens))
    while active:
        still = []
        for i, g in active:
            try:
                next(g)
                still.append((i, g))
            except StopIteration as stop:
                results[i] = stop.value
        active = still
    return results


def _rwkv_chain(fwd, r, v, kk, lw, k, a, t0, C):
    v = v.astype(BF16)

    row = lax.broadcasted_iota(jnp.int32, (C, C), 0)
    col = lax.broadcasted_iota(jnp.int32, (C, C), 1)
    incl = (row >= col) if fwd else (row <= col)
    strict = (row > col) if fwd else (row < col)
    incl_bf = jnp.where(incl, 1.0, 0.0).astype(BF16)
    eye = jnp.where(row == col, 1.0, 0.0)

    l_hi = lw.astype(BF16)
    l_lo = (lw - l_hi.astype(F32)).astype(BF16)
    cs = _dot(incl_bf, l_hi, _NN) + _dot(incl_bf, l_lo, _NN)
    yield
    tot = jnp.sum(lw, axis=0, keepdims=True)

    beta = kk * a
    e_out = jnp.exp(-cs)
    e_rem = jnp.exp(tot - cs)
    at = -kk * jnp.exp(cs - lw)
    rt = r * jnp.exp(cs)
    ar = jnp.concatenate([at, rt], axis=0).astype(BF16)
    bk = jnp.concatenate([beta * e_out, k * e_out], axis=0)
    bkh = jnp.concatenate([beta * e_rem, k * e_rem], axis=0)

    big = _bdot(ar, bk, _NT)
    x = _bdot(ar, t0)
    yield
    a_ab = jnp.where(strict, big[:C, :C], 0.0)
    a_ak = jnp.where(strict, big[:C, C:], 0.0)
    row2 = lax.broadcasted_iota(jnp.int32, (C, 2 * C), 0)
    col2 = lax.broadcasted_iota(jnp.int32, (C, 2 * C), 1)
    col2 = jnp.where(col2 >= C, col2 - C, col2)
    incl2 = (row2 >= col2) if fwd else (row2 <= col2)
    a_r = jnp.where(incl2, big[C:], 0.0)
    akv = _bdot(a_ak, v)
    minv = yield from _tri_inverse(a_ab, eye, row, col)

    u = _bdot(minv, x[:C] + akv)
    yield
    uv = jnp.concatenate([u.astype(BF16), v], axis=0)
    o = x[C:] + _bdot(a_r, uv)

    drow = lax.broadcasted_iota(jnp.int32, (HEAD_DIM, HEAD_DIM), 0)
    dcol = lax.broadcasted_iota(jnp.int32, (HEAD_DIM, HEAD_DIM), 1)
    gcol = jnp.sum(jnp.where(drow == dcol, jnp.exp(tot), 0.0), axis=1, keepdims=True)
    return o, gcol * t0 + _bdot(bkh, uv, _TN)


def _rwkv_kernel(rf_ref, vf_ref, kkf_ref, lwf_ref, kf_ref, af_ref, rb_ref, vb_ref, kkb_ref, lwb_ref, kb_ref, ab_ref,
                 s0_ref, of_ref, ob_ref, st_ref, t_scr, *, chunk, n_chunks):
    c = pl.program_id(1)

    @pl.when(c == 0)
    def _():
        t_scr[...] = s0_ref[:, 0].reshape(t_scr.shape)

    t_all = t_scr[...]
    fw = [ref[0] for ref in (rf_ref, vf_ref, kkf_ref)] + [ref[0, 0] for ref in (lwf_ref, kf_ref, af_ref)]
    bw = [ref[0] for ref in (rb_ref, vb_ref, kkb_ref)] + [ref[0, 0] for ref in (lwb_ref, kb_ref, ab_ref)]
    chains = []
    for fwd, tiles, base in ((True, fw, 0), (False, bw, N_HEADS)):
        for h in range(N_HEADS):
            sl = slice(h * HEAD_DIM, (h + 1) * HEAD_DIM)
            chains.append(_rwkv_chain(fwd, *[t[:, sl] for t in tiles], t_all[base + h], chunk))
    res = _interleave(chains)
    of_ref[0] = jnp.concatenate([o for o, _ in res[:N_HEADS]], axis=1)
    ob_ref[0] = jnp.concatenate([o for o, _ in res[N_HEADS:]], axis=1)
    t_scr[...] = jnp.stack([t for _, t in res])

    @pl.when(c == n_chunks - 1)
    def _():
        st_ref[:, 0] = t_scr[...].reshape(2, N_HEADS, HEAD_DIM, HEAD_DIM)


def rwkv7_scan(r, v, kk, lw, k, a, s0t, chunk=RWKV_CHUNK):
    B, L, G = r.shape
    n = L // chunk
    assert L % chunk == 0
    sh_f = pl.BlockSpec((1, chunk, G), lambda b, c: (b, c, 0))
    sh_b = pl.BlockSpec((1, chunk, G), lambda b, c: (b, n - 1 - c, 0))
    pd_f = pl.BlockSpec((1, 1, chunk, G), lambda b, c: (0, b, c, 0))
    pd_b = pl.BlockSpec((1, 1, chunk, G), lambda b, c: (1, b, n - 1 - c, 0))
    state = pl.BlockSpec((2, 1, N_HEADS, HEAD_DIM, HEAD_DIM), lambda b, c: (0, b, 0, 0, 0))
    return pl.pallas_call(
        functools.partial(_rwkv_kernel, chunk=chunk, n_chunks=n),
        grid=(B, n),
        in_specs=[sh_f, sh_f, sh_f, pd_f, pd_f, pd_f, sh_b, sh_b, sh_b, pd_b, pd_b, pd_b, state],
        out_specs=[sh_f, sh_b, state],
        out_shape=[jax.ShapeDtypeStruct((B, L, G), F32), jax.ShapeDtypeStruct((B, L, G), F32),
                   jax.ShapeDtypeStruct((2, B, N_HEADS, HEAD_DIM, HEAD_DIM), F32)],
        scratch_shapes=[pltpu.VMEM((2 * N_HEADS, HEAD_DIM, HEAD_DIM), F32)],
        compiler_params=pltpu.CompilerParams(
            dimension_semantics=("parallel", "arbitrary"), vmem_limit_bytes=VMEM_LIMIT_BYTES),
        name="rwkv7_scan",
    )(r, v, kk, lw, k, a, r, v, kk, lw, k, a, s0t)


def _ret_chain(fwd, h, q, k, v, lg, o_ref, s_scr, C):
    sl = slice(h * HEAD_DIM, (h + 1) * HEAD_DIM)
    row = lax.broadcasted_iota(jnp.int32, (C, C), 0)
    col = lax.broadcasted_iota(jnp.int32, (C, C), 1)
    dist = ((row - col) if fwd else (col - row)).astype(F32)
    decay_in = jnp.where(dist >= 0, jnp.exp(jnp.maximum(dist, 0.0) * lg), 0.0)
    pos = lax.broadcasted_iota(jnp.int32, (C, 1), 0).astype(F32)
    step = pos if fwd else (C - 1.0) - pos
    decay_q = jnp.exp((step + 1.0) * lg)
    decay_k = jnp.exp((C - 1.0 - step) * lg)
    qh, kh, vh = q[:, sl], k[:, sl], v[:, sl].astype(BF16)
    si = h if fwd else N_HEADS + h
    s0 = s_scr[si]
    qk = _bdot(qh, kh, _NT)
    qs = _bdot(qh, s0)
    kv = _bdot(kh * decay_k, vh, _TN)
    yield
    ov = _bdot(qk * decay_in, vh)
    yield
    o_ref[0, :, sl] = ov + qs * decay_q
    s_scr[si] = s0 * jnp.exp(C * lg) + kv


def _rope(x, cos, sin):
    lane = lax.broadcasted_iota(jnp.int32, x.shape, 1)
    n = x.shape[1]
    swapped = jnp.where((lane % HEAD_DIM) < HEAD_DIM // 2,
                        pltpu.roll(x, n - HEAD_DIM // 2, axis=1), pltpu.roll(x, HEAD_DIM // 2, axis=1))
    return x * cos + swapped * sin


def _ret_kernel(lg_ref, qf_ref, kf_ref, vf_ref, qb_ref, kb_ref, vb_ref, cf_ref, sf_ref, cb_ref, sb_ref, s0_ref,
                of_ref, ob_ref, st_ref, s_scr, *, chunk, n_chunks, rope):
    c = pl.program_id(1)

    @pl.when(c == 0)
    def _():
        s_scr[...] = s0_ref[0].reshape(s_scr.shape)

    scale = HEAD_DIM ** -0.5
    qf, kf, qb, kb = qf_ref[0], kf_ref[0] * scale, qb_ref[0], kb_ref[0] * scale
    if rope:
        qf, kf = _rope(qf, cf_ref[...], sf_ref[...]), _rope(kf, cf_ref[...], sf_ref[...])
        qb, kb = _rope(qb, cb_ref[...], sb_ref[...]), _rope(kb, cb_ref[...], sb_ref[...])
    vf, vb = vf_ref[0], vb_ref[0]
    _interleave([_ret_chain(True, h, qf, kf, vf, lg_ref[0, h], of_ref, s_scr, chunk) for h in range(N_HEADS)]
                + [_ret_chain(False, h, qb, kb, vb, lg_ref[1, h], ob_ref, s_scr, chunk) for h in range(N_HEADS)])

    @pl.when(c == n_chunks - 1)
    def _():
        st_ref[0] = s_scr[...].reshape(2, N_HEADS, HEAD_DIM, HEAD_DIM)


def rope_tables(length):
    t = jnp.arange(length)
    n_freq = HEAD_DIM // 4
    inv = ROPE_BASE ** (-jnp.arange(n_freq, dtype=F32) / n_freq)
    ang = jnp.concatenate([(t // GRID_W).astype(F32)[:, None] * inv,
                           (t % GRID_W).astype(F32)[:, None] * inv], axis=-1)
    cos, sin = jnp.cos(ang), jnp.sin(ang)
    return (jnp.tile(jnp.concatenate([cos, cos], axis=-1), (1, N_HEADS)),
            jnp.tile(jnp.concatenate([-sin, sin], axis=-1), (1, N_HEADS)))


def retention(proj, ret_decay, s0, rope, chunk=RET_CHUNK):
    B, L, _ = proj.shape
    G = GROUP_W
    n = L // chunk
    assert L % chunk == 0
    log_g = jax.nn.log_sigmoid(ret_decay)
    cos, sin = rope_tables(L)
    fw = lambda j: pl.BlockSpec((1, chunk, G), lambda b, c: (b, c, j))
    bw = lambda j: pl.BlockSpec((1, chunk, G), lambda b, c: (b, n - 1 - c, j))
    tab_f = pl.BlockSpec((chunk, G), lambda b, c: (c, 0))
    tab_b = pl.BlockSpec((chunk, G), lambda b, c: (n - 1 - c, 0))
    state = pl.BlockSpec((1, 2, N_HEADS, HEAD_DIM, HEAD_DIM), lambda b, c: (b, 0, 0, 0, 0))
    return pl.pallas_call(
        functools.partial(_ret_kernel, chunk=chunk, n_chunks=n, rope=rope),
        grid=(B, n),
        in_specs=[pl.BlockSpec(memory_space=pltpu.SMEM), fw(5), fw(6), fw(7), bw(5), bw(6), bw(7),
                  tab_f, tab_f, tab_b, tab_b, state],
        out_specs=[pl.BlockSpec((1, chunk, G), lambda b, c: (b, c, 0)),
                   pl.BlockSpec((1, chunk, G), lambda b, c: (b, n - 1 - c, 0)), state],
        out_shape=[jax.ShapeDtypeStruct((B, L, G), F32), jax.ShapeDtypeStruct((B, L, G), F32),
                   jax.ShapeDtypeStruct((B, 2, N_HEADS, HEAD_DIM, HEAD_DIM), F32)],
        scratch_shapes=[pltpu.VMEM((2 * N_HEADS, HEAD_DIM, HEAD_DIM), F32)],
        compiler_params=pltpu.CompilerParams(
            dimension_semantics=("parallel", "arbitrary"), vmem_limit_bytes=VMEM_LIMIT_BYTES),
        name="retention",
    )(log_g, proj, proj, proj, proj, proj, proj, cos, sin, cos, sin, s0)


def _softmax_pv(s_list, v_list):
    m = s_list[0].max(axis=-1, keepdims=True)
    for s in s_list[1:]:
        m = jnp.maximum(m, s.max(axis=-1, keepdims=True))
    den = 0.0
    acc = 0.0
    for s, v in zip(s_list, v_list):
        p = jnp.exp(s - m)
        den = den + p.sum(axis=-1, keepdims=True)
        acc = acc + _dot(p.astype(BF16), v, _NN)
    yield
    return acc / den


def _na_kernel(q_ref, k_ref, v_ref, ck_ref, cv_ref, tb_ref, mask_ref, o_ref, *, rows, kh):
    r = pl.program_id(1)
    rs = jnp.clip(r - kh // 2, 0, rows - kh)
    pat = r - rs
    start = pl.multiple_of(rs * GRID_W, GRID_W)
    kwin = k_ref[0, pl.ds(start, kh * GRID_W), :].astype(BF16)
    vwin = v_ref[0, pl.ds(start, kh * GRID_W), :].astype(BF16)
    ck = ck_ref[0, 0].astype(BF16)
    cv = cv_ref[0, 0].astype(BF16)
    q = (q_ref[0] * (HEAD_DIM ** -0.5)).astype(BF16)
    mask = mask_ref[...] > 0.0
    def head(h):
        sl = slice(h * HEAD_DIM, (h + 1) * HEAD_DIM)
        qh = q[:, sl]
        s_loc = _dot(qh, kwin[:, sl], _NT)
        s_ctx = _dot(qh, ck[:, sl], _NT)
        yield
        s_loc = jnp.where(mask, s_loc + tb_ref[h, pat], NEG_INF)
        return (yield from _softmax_pv([s_loc, s_ctx], [vwin[:, sl], cv[:, sl]]))

    o_ref[0] = jnp.concatenate(_interleave([head(h) for h in range(N_HEADS)]), axis=1)


def na_bias_table(rpb, kh):
    col = np.arange(GRID_W)
    d_col = np.clip(col[None, :] - col[:, None], -(NA_WIN_W - 1), NA_WIN_W - 1) + (NA_WIN_W - 1)
    onehot = jnp.asarray(d_col[:, :, None] == np.arange(2 * NA_WIN_W - 1), F32)
    rows = jnp.stack([rpb[:, NA_WIN_H - 1 - p:NA_WIN_H - 1 - p + kh] for p in range(kh)], axis=1)
    tb = jnp.einsum("hpic,qkc->hpqik", rows, onehot, precision=lax.Precision.HIGHEST)
    col_start = np.clip(col - NA_WIN_W // 2, 0, GRID_W - NA_WIN_W)
    col_in = (col[None, :] >= col_start[:, None]) & (col[None, :] < col_start[:, None] + NA_WIN_W)
    mask = jnp.asarray(np.tile(col_in.astype(np.float32), (1, kh)))
    return tb.reshape(N_HEADS, kh, GRID_W, kh * GRID_W), mask


def neighbourhood_attention(proj, cache_k, cache_v, layer, rpb):
    B, L, _ = proj.shape
    rows = L // GRID_W
    kh = NA_WIN_H
    assert rows >= kh
    Lc = cache_k.shape[2]
    tb, mask = na_bias_table(rpb, kh)
    G = GROUP_W
    return pl.pallas_call(
        functools.partial(_na_kernel, rows=rows, kh=kh),
        grid=(B, rows),
        in_specs=[pl.BlockSpec((1, GRID_W, G), lambda b, r: (b, r, 0)),
                  pl.BlockSpec((1, L, G), lambda b, r: (b, 0, 1)),
                  pl.BlockSpec((1, L, G), lambda b, r: (b, 0, 2)),
                  pl.BlockSpec((1, 1, Lc, G), lambda b, r: (b, layer, 0, 0)),
                  pl.BlockSpec((1, 1, Lc, G), lambda b, r: (b, layer, 0, 0)),
                  pl.BlockSpec((N_HEADS, kh, GRID_W, kh * GRID_W), lambda b, r: (0, 0, 0, 0)),
                  pl.BlockSpec((GRID_W, kh * GRID_W), lambda b, r: (0, 0))],
        out_specs=pl.BlockSpec((1, GRID_W, G), lambda b, r: (b, r, 0)),
        out_shape=jax.ShapeDtypeStruct((B, L, G), F32),
        compiler_params=pltpu.CompilerParams(
            dimension_semantics=("parallel", "arbitrary"), vmem_limit_bytes=VMEM_LIMIT_BYTES),
        name="na_attention",
    )(proj, proj, proj, cache_k, cache_v, tb, mask)


def _ctx_attn_kernel(q_ref, k_ref, v_ref, o_ref):
    q = (q_ref[0] * (HEAD_DIM ** -0.5)).astype(BF16)
    k = k_ref[0].astype(BF16)
    v = v_ref[0].astype(BF16)
    def head(h):
        sl = slice(h * HEAD_DIM, (h + 1) * HEAD_DIM)
        s = _dot(q[:, sl], k[:, sl], _NT)
        yield
        return (yield from _softmax_pv([s], [v[:, sl]]))

    o_ref[0] = jnp.concatenate(_interleave([head(h) for h in range(N_HEADS)]), axis=1)


def context_attention(proj):
    B, L, _ = proj.shape
    G = GROUP_W
    return pl.pallas_call(
        _ctx_attn_kernel,
        grid=(B,),
        in_specs=[pl.BlockSpec((1, L, G), lambda b: (b, 0, 0)),
                  pl.BlockSpec((1, L, G), lambda b: (b, 0, 1)),
                  pl.BlockSpec((1, L, G), lambda b: (b, 0, 2))],
        out_specs=pl.BlockSpec((1, L, G), lambda b: (b, 0, 0)),
        out_shape=jax.ShapeDtypeStruct((B, L, G), F32),
        compiler_params=pltpu.CompilerParams(dimension_semantics=("parallel",), vmem_limit_bytes=VMEM_LIMIT_BYTES),
        name="ctx_attention",
    )(proj, proj, proj)


def rms_norm(x, g):
    return x * lax.rsqrt(jnp.mean(x * x, axis=-1, keepdims=True) + EPS) * g


def layer_norm(x, g, b):
    xc = x - jnp.mean(x, axis=-1, keepdims=True)
    return xc * lax.rsqrt(jnp.mean(xc * xc, axis=-1, keepdims=True) + EPS) * g + b


def head_norm(x, g, eps):
    xc = x - jnp.mean(x, axis=-1, keepdims=True)
    y = xc * lax.rsqrt(jnp.mean(xc * xc, axis=-1, keepdims=True) + eps)
    return y.reshape(x.shape[0], x.shape[1], -1) * g


def dwconv(x, w):
    k, ch = w.shape
    return lax.conv_general_dilated(x, w[:, None, :], window_strides=(1,), padding=[(k // 2, k // 2)],
                                    dimension_numbers=("NWC", "WIO", "NWC"), feature_group_count=ch)


def token_mixers(x, mod, rows_per_mod, p, cache):
    B, L, _ = x.shape
    latent = cache is not None
    x2d = x.reshape(B * L, D_MODEL)
    proj2d = norm_matmul(x2d, mod, rows_per_mod, p["norm_g"], p["w_in"])
    proj = proj2d.reshape(B, L, -1)
    (na_q, na_k, na_v, cv_a, cv_b, rt_q, rt_k, rt_v, rt_g,
     rw_r, rw_k, rw_v, rw_w1, rw_a1, rw_g1) = jnp.split(proj, SPLIT_POINTS, axis=-1)

    def heads(t):
        return t.reshape(B, L, N_HEADS, HEAD_DIM)

    a_k, a_v = heads(na_k), heads(na_v)
    if latent:
        o_a = neighbourhood_attention(proj, cache["na_k"], cache["na_v"], cache["layer"], p["na_rpb"])
    else:
        o_a = context_attention(proj)

    o_b = conv_module(proj2d, L, p["conv_dw"], p["conv_ln_g"], p["conv_ln_b"]).reshape(B, L, GROUP_W)

    ret0 = cache["ret"] if latent else jnp.zeros((B, 2, N_HEADS, HEAD_DIM, HEAD_DIM), F32)
    o_cf, o_cb, ret_state = retention(proj, p["ret_decay"], ret0, rope=latent)
    o_c = head_norm(heads(o_cf + o_cb), p["ret_gn"], EPS) * jax.nn.silu(rt_g)

    d_r, d_k, d_v = jnp.split(dwconv(jnp.concatenate([rw_r, rw_k, rw_v], axis=-1), p["rwkv_shift"]), 3, axis=-1)
    w_low = jnp.tanh(rw_w1)
    gate = jax.nn.sigmoid(rw_g1) @ p["rwkv_g2"]
    kk = heads(d_k * p["rwkv_kk"])
    kk = (kk * lax.rsqrt(jnp.sum(kk * kk, axis=-1, keepdims=True) + 1e-12)).reshape(B, L, GROUP_W)
    rwkv0 = cache["rwkv"] if latent else jnp.zeros((B, 2, N_HEADS, HEAD_DIM, HEAD_DIM), F32)
    lws, ks, As = [], [], []
    for d in range(2):
        z_w = p["rwkv_w0"][d] + w_low @ p["rwkv_w2"][d]
        lws.append(-jnp.exp(-jax.nn.softplus(-z_w) - 0.5))
        a = jax.nn.sigmoid(p["rwkv_a0"][d] + rw_a1 @ p["rwkv_a2"][d])
        ks.append(d_k * (1.0 + (a - 1.0) * p["rwkv_ka"]))
        As.append(a)
    lw2, k2, a2 = jnp.stack(lws), jnp.stack(ks), jnp.stack(As)
    s0t = jnp.swapaxes(rwkv0, -1, -2).transpose(1, 0, 2, 3, 4)
    o_f, o_bw, st = rwkv7_scan(d_r, d_v, kk, lw2, k2, a2, s0t)
    rwkv_state = jnp.swapaxes(st, -1, -2).transpose(1, 0, 2, 3, 4)
    bonus = (jnp.sum(heads(d_r * (k2[0] + k2[1]) * p["rwkv_rk"]), axis=-1, keepdims=True) * heads(d_v))
    o_d = (head_norm(heads(o_f + o_bw), p["rwkv_gn"], RWKV_GN_EPS) + bonus.reshape(B, L, GROUP_W)) * gate

    mixed_in = jnp.concatenate([o_a, o_b, o_c, o_d], axis=-1).reshape(B * L, D_MODEL)
    x2d = matmul_residual(mixed_in, x2d, mod, rows_per_mod, p["norm_g"], p["w_out"])
    return x2d, (None if latent else (a_k, a_v, ret_state, rwkv_state))


def trunk_layer(x, mod_vec, p, cache):
    B, L, _ = x.shape
    n_mod = mod_vec.shape[0]
    rows_per_mod = B * L // n_mod
    mod = jnp.pad(mod_vec.reshape(n_mod, 6, D_MODEL), ((0, 0), (0, MOD_ROWS - 6), (0, 0)))
    x2d, state = token_mixers(x, mod, rows_per_mod, p, cache)
    x2d = ffn_block(x2d, mod, rows_per_mod, L, p["norm_g"], p["ffn_up"], p["ffn_conv"], p["ffn_down"])
    return x2d.reshape(B, L, D_MODEL), state


def kernel(x_prompt, x_sample, cache_na_k, cache_na_v, state_retention, state_rwkv, c, c_ctx, ada_w, ada_b, norm_g, w_in, w_out, na_rpb, conv_dw, conv_ln_g, conv_ln_b, ret_decay, ret_gn, rwkv_shift, rwkv_w0, rwkv_w2, rwkv_a0, rwkv_a2, rwkv_g2, rwkv_kk, rwkv_ka, rwkv_rk, rwkv_gn, ffn_up, ffn_conv, ffn_down):
    y_prompt, y_sample = x_prompt, x_sample
    new_k, new_v, new_ret, new_rwkv = [], [], [], []
    for l in range(DEPTH):
        p = {"norm_g": norm_g[l], "w_in": w_in[l].astype(BF16), "w_out": w_out[l].astype(BF16), "na_rpb": na_rpb[l],
             "conv_dw": conv_dw[l], "conv_ln_g": conv_ln_g[l], "conv_ln_b": conv_ln_b[l],
             "ret_decay": ret_decay[l], "ret_gn": ret_gn[l], "rwkv_shift": rwkv_shift[l],
             "rwkv_w0": rwkv_w0[l], "rwkv_w2": rwkv_w2[l], "rwkv_a0": rwkv_a0[l], "rwkv_a2": rwkv_a2[l],
             "rwkv_g2": rwkv_g2[l], "rwkv_kk": rwkv_kk[l], "rwkv_ka": rwkv_ka[l], "rwkv_rk": rwkv_rk[l],
             "rwkv_gn": rwkv_gn[l], "ffn_up": ffn_up[l].astype(BF16), "ffn_conv": ffn_conv[l],
             "ffn_down": ffn_down[l].astype(BF16)}
        mod_ctx = (jax.nn.silu(c_ctx) @ ada_w[l] + ada_b[l])[None]
        y_prompt, (k_l, v_l, ret_l, rwkv_l) = trunk_layer(y_prompt, mod_ctx, p, None)
        new_k.append(k_l)
        new_v.append(v_l)
        new_ret.append(ret_l)
        new_rwkv.append(rwkv_l)
        mod_lat = jax.nn.silu(c) @ ada_w[l] + ada_b[l]
        cache_l = {"na_k": cache_na_k.reshape(cache_na_k.shape[:3] + (GROUP_W,)),
                   "na_v": cache_na_v.reshape(cache_na_v.shape[:3] + (GROUP_W,)), "layer": l,
                   "ret": state_retention[:, l], "rwkv": state_rwkv[:, l]}
        y_sample, _ = trunk_layer(y_sample, mod_lat, p, cache_l)
    return (y_prompt, y_sample, jnp.stack(new_k, axis=1), jnp.stack(new_v, axis=1),
            jnp.stack(new_ret, axis=1), jnp.stack(new_rwkv, axis=1))
```

```python
import functools

import jax
import jax.numpy as jnp
import numpy as np
from jax import lax
from jax.experimental import pallas as pl
from jax.experimental.pallas import tpu as pltpu

F32 = jnp.float32
BF16 = jnp.bfloat16

D_MODEL = 1024
DEPTH = 2
GRID_W = 64
N_MIXERS = 4
GROUP_W = D_MODEL // N_MIXERS
HEAD_DIM = 64
N_HEADS = GROUP_W // HEAD_DIM
NA_WIN_H = 8
NA_WIN_W = 16
Q_BLOCK = 128
CONV_W = 31
RET_CHUNK = 128
RWKV_LORA_W = 64
RWKV_LORA_A = 64
RWKV_LORA_G = 128
D_FF = 2816
ROPE_BASE = 10000.0
EPS = 1e-6
RWKV_GN_EPS = 64e-5
NEG_INF = -1e30
SPLIT_SIZES = (GROUP_W,) * 12 + (RWKV_LORA_W, RWKV_LORA_A, RWKV_LORA_G)
SPLIT_POINTS = tuple(int(s) for s in np.cumsum(SPLIT_SIZES)[:-1])

RWKV_CHUNK = 64
VMEM_LIMIT_BYTES = 48 * 1024 * 1024
SUBLANES = 8
MOD_ROWS = 8
ROW_TILE = 512
FFN_CHUNK = 1408


def _mm_kernel(a_ref, b_ref, o_ref):
    o_ref[...] = jnp.dot(a_ref[...].astype(BF16), b_ref[...], preferred_element_type=F32)


def _pick_tile(n, target):
    best = 128
    for t in range(128, target + 1, 128):
        if n % t == 0:
            best = t
    return best


def matmul(a, b_bf16, tm=512, tn_target=1792):
    m, k = a.shape
    n = b_bf16.shape[1]
    tn = _pick_tile(n, tn_target)
    tm = min(tm, m)
    assert m % tm == 0 and n % tn == 0
    return pl.pallas_call(
        _mm_kernel,
        grid=(n // tn, m // tm),
        in_specs=[pl.BlockSpec((tm, k), lambda j, i: (i, 0)),
                  pl.BlockSpec((k, tn), lambda j, i: (0, j))],
        out_specs=pl.BlockSpec((tm, tn), lambda j, i: (i, j)),
        out_shape=jax.ShapeDtypeStruct((m, n), F32),
        compiler_params=pltpu.CompilerParams(
            dimension_semantics=("parallel", "parallel"), vmem_limit_bytes=VMEM_LIMIT_BYTES),
        name="matmul",
    )(a, b_bf16)


def _rms(x, g):
    return x * lax.rsqrt(jnp.mean(x * x, axis=-1, keepdims=True) + EPS) * g


def _resident(shape):
    return pl.BlockSpec(shape, lambda *_: (0,) * len(shape), pipeline_mode=pl.Buffered(1))


def _mod_spec(tm, rows_per_mod):
    return pl.BlockSpec((1, MOD_ROWS, D_MODEL), lambda i: ((i * tm) // rows_per_mod, 0, 0))


def _norm_mm_kernel(x_ref, mod_ref, g_ref, w_ref, o_ref):
    mod = mod_ref[0]
    h = _rms(x_ref[...], g_ref[0:1]) * (1.0 + mod[1:2]) + mod[0:1]
    o_ref[...] = jnp.dot(h.astype(BF16), w_ref[...], preferred_element_type=F32)


def norm_matmul(x, mod, rows_per_mod, norm_g, w_bf16, tm=ROW_TILE):
    m, n = x.shape[0], w_bf16.shape[1]
    tm = min(tm, rows_per_mod)
    return pl.pallas_call(
        _norm_mm_kernel,
        grid=(m // tm,),
        in_specs=[pl.BlockSpec((tm, D_MODEL), lambda i: (i, 0)), _mod_spec(tm, rows_per_mod),
                  _resident(norm_g.shape), _resident(w_bf16.shape)],
        out_specs=pl.BlockSpec((tm, n), lambda i: (i, 0)),
        out_shape=jax.ShapeDtypeStruct((m, n), F32),
        compiler_params=pltpu.CompilerParams(dimension_semantics=("parallel",), vmem_limit_bytes=VMEM_LIMIT_BYTES),
        name="norm_matmul",
    )(x, mod, norm_g, w_bf16)


def _head_sum(y):
    g = y.shape[1]
    row = lax.broadcasted_iota(jnp.int32, (g, g), 0) // HEAD_DIM
    col = lax.broadcasted_iota(jnp.int32, (g, g), 1) // HEAD_DIM
    ones = jnp.where(row == col, 1.0, 0.0).astype(BF16)
    hi = y.astype(BF16)
    lo = (y - hi.astype(F32)).astype(BF16)
    return jnp.dot(hi, ones, preferred_element_type=F32) + jnp.dot(lo, ones, preferred_element_type=F32)


def _head_norm(y, g, eps):
    yc = y - _head_sum(y) * (1.0 / HEAD_DIM)
    return yc * lax.rsqrt(_head_sum(yc * yc) * (1.0 / HEAD_DIM) + eps) * g


def _mix_out_kernel(oa_ref, ob_ref, cf_ref, cb_ref, rg_ref, df_ref, db_ref, bonus_ref, gate_ref, x_ref, mod_ref,
                    g_ref, gn_ref, w_ref, o_ref):
    G = GROUP_W
    rg = rg_ref[...]
    o_c = _head_norm(cf_ref[...] + cb_ref[...], gn_ref[0:1], EPS) * (rg * jax.nn.sigmoid(rg))
    o_d = (_head_norm(df_ref[...] + db_ref[...], gn_ref[1:2], RWKV_GN_EPS) + bonus_ref[...]) * gate_ref[...]
    m = 0.0
    for j, o in enumerate((oa_ref[...], ob_ref[...], o_c, o_d)):
        m = m + jnp.dot(o.astype(BF16), w_ref[j * G:(j + 1) * G, :], preferred_element_type=F32)
    o_ref[...] = x_ref[...] + mod_ref[0][2:3] * _rms(m, g_ref[1:2])


def mix_out(o_a, o_b, o_cf, o_cb, proj2d, o_df, o_db, bonus, gate, x, mod, rows_per_mod, norm_g, ret_gn, rwkv_gn,
            w_bf16, tm=ROW_TILE):
    m = x.shape[0]
    tm = min(tm, rows_per_mod)
    G = GROUP_W
    grp = pl.BlockSpec((tm, G), lambda i: (i, 0))
    gn = jnp.stack([ret_gn, rwkv_gn])
    return pl.pallas_call(
        _mix_out_kernel,
        grid=(m // tm,),
        in_specs=[grp, grp, grp, grp, pl.BlockSpec((tm, G), lambda i: (i, 8)), grp, grp, grp, grp,
                  pl.BlockSpec((tm, D_MODEL), lambda i: (i, 0)), _mod_spec(tm, rows_per_mod),
                  _resident(norm_g.shape), _resident(gn.shape), _resident(w_bf16.shape)],
        out_specs=pl.BlockSpec((tm, D_MODEL), lambda i: (i, 0)),
        out_shape=jax.ShapeDtypeStruct((m, D_MODEL), F32),
        compiler_params=pltpu.CompilerParams(dimension_semantics=("parallel",), vmem_limit_bytes=VMEM_LIMIT_BYTES),
        name="mix_out",
    )(o_a, o_b, o_cf, o_cb, proj2d, o_df, o_db, bonus, gate, x, mod, norm_g, gn, w_bf16)


def _rwkv_prep_kernel(rkv_ref, rkvp_ref, rkvn_ref, lora_ref, sh_ref, w0_ref, w2_ref, a0_ref, a2_ref, g2_ref, vec_ref,
                      r_ref, v_ref, kk_ref, lw_ref, k_ref, a_ref, bonus_ref, gate_ref, *, seq_tiles):
    i = pl.program_id(0)
    G = GROUP_W
    keep_prev = jnp.where(i % seq_tiles == 0, 0.0, 1.0)
    keep_next = jnp.where(i % seq_tiles == seq_tiles - 1, 0.0, 1.0)
    x = rkv_ref[...]
    x_prev, x_next = _shift_rows(x, rkvp_ref[SUBLANES - 1:SUBLANES] * keep_prev, rkvn_ref[0:1] * keep_next)
    rkv = sh_ref[0:1] * x_prev + sh_ref[1:2] * x + sh_ref[2:3] * x_next
    d_r, d_k, d_v = rkv[:, :G], rkv[:, G:2 * G], rkv[:, 2 * G:]
    lora = lora_ref[...]
    w_low = jnp.tanh(lora[:, :RWKV_LORA_W]).astype(BF16)
    a_low = lora[:, RWKV_LORA_W:RWKV_LORA_W + RWKV_LORA_A].astype(BF16)
    g_low = jax.nn.sigmoid(lora[:, RWKV_LORA_W + RWKV_LORA_A:]).astype(BF16)
    kk_w, ka, rk = vec_ref[0:1], vec_ref[1:2], vec_ref[2:3]
    k_sum = 0.0
    for d in range(2):
        z = w0_ref[d:d + 1] + jnp.dot(w_low, w2_ref[d].astype(BF16), preferred_element_type=F32)
        lw_ref[d] = -float(np.exp(-0.5)) * jax.nn.sigmoid(z)
        a = jax.nn.sigmoid(a0_ref[d:d + 1] + jnp.dot(a_low, a2_ref[d].astype(BF16), preferred_element_type=F32))
        k_d = d_k * (1.0 + (a - 1.0) * ka)
        a_ref[d] = a
        k_ref[d] = k_d
        k_sum = k_sum + k_d
    kk = d_k * kk_w
    r_ref[...] = d_r
    v_ref[...] = d_v
    kk_ref[...] = kk * lax.rsqrt(_head_sum(kk * kk) + 1e-12)
    bonus_ref[...] = _head_sum(d_r * k_sum * rk) * d_v
    gate_ref[...] = jnp.dot(g_low, g2_ref[...].astype(BF16), preferred_element_type=F32)


def rwkv_prep(proj2d, seq_len, p, tm=ROW_TILE):
    m = proj2d.shape[0]
    tm = min(tm, seq_len)
    G = GROUP_W
    assert seq_len % tm == 0
    prev_spec, next_spec = _halo_specs(tm, m, 3 * G, col=3)
    vec = jnp.stack([p["rwkv_kk"], p["rwkv_ka"], p["rwkv_rk"]])
    one = pl.BlockSpec((tm, G), lambda i: (i, 0))
    two = pl.BlockSpec((2, tm, G), lambda i: (0, i, 0))
    sds1 = jax.ShapeDtypeStruct((m, G), F32)
    sds2 = jax.ShapeDtypeStruct((2, m, G), F32)
    params = (p["rwkv_shift"], p["rwkv_w0"], p["rwkv_w2"], p["rwkv_a0"], p["rwkv_a2"], p["rwkv_g2"], vec)
    return pl.pallas_call(
        functools.partial(_rwkv_prep_kernel, seq_tiles=seq_len // tm),
        grid=(m // tm,),
        in_specs=[pl.BlockSpec((tm, 3 * G), lambda i: (i, 3)), prev_spec, next_spec,
                  pl.BlockSpec((tm, G), lambda i: (i, 12))] + [_resident(t.shape) for t in params],
        out_specs=[one, one, one, two, two, two, one, one],
        out_shape=[sds1, sds1, sds1, sds2, sds2, sds2, sds1, sds1],
        compiler_params=pltpu.CompilerParams(dimension_semantics=("parallel",), vmem_limit_bytes=VMEM_LIMIT_BYTES),
        name="rwkv_prep",
    )(proj2d, proj2d, proj2d, proj2d, *params)


def _halo_specs(tm, m, width, col=0):
    blocks = tm // SUBLANES
    return (pl.BlockSpec((SUBLANES, width), lambda i: (jnp.maximum(i * blocks - 1, 0), col)),
            pl.BlockSpec((SUBLANES, width), lambda i: (jnp.minimum((i + 1) * blocks, m // SUBLANES - 1), col)))


def _shift_rows(u, prev_row, next_row):
    tm = u.shape[0]
    row = lax.broadcasted_iota(jnp.int32, (tm, 1), 0)
    u_prev = jnp.where(row == 0, prev_row, pltpu.roll(u, 1, axis=0))
    u_next = jnp.where(row == tm - 1, next_row, pltpu.roll(u, tm - 1, axis=0))
    return u_prev, u_next


def _ffn_kernel(x_ref, xp_ref, xn_ref, mod_ref, g_ref, up_ref, cw_ref, down_ref, o_ref, *, seq_tiles, cw):
    i = pl.program_id(0)
    keep_prev = jnp.where(i % seq_tiles == 0, 0.0, 1.0)
    keep_next = jnp.where(i % seq_tiles == seq_tiles - 1, 0.0, 1.0)
    mod = mod_ref[0]
    g2, g3 = g_ref[2:3], g_ref[3:4]

    def pre(x):
        return (_rms(x, g2) * (1.0 + mod[4:5]) + mod[3:4]).astype(BF16)

    x = x_ref[...]
    h = pre(x)
    hh = pre(jnp.concatenate([xp_ref[...], xn_ref[...]], axis=0))
    acc = jnp.zeros(x.shape, F32)
    for j in range(D_FF // cw):
        conv = []
        for half in range(2):
            cols = slice(half * D_FF + j * cw, half * D_FF + (j + 1) * cw)
            w_up = up_ref[:, cols]
            u = jnp.dot(h, w_up, preferred_element_type=F32)
            uh = jnp.dot(hh, w_up, preferred_element_type=F32)
            u_prev, u_next = _shift_rows(u, uh[SUBLANES - 1:SUBLANES] * keep_prev, uh[SUBLANES:SUBLANES + 1] * keep_next)
            wc = cw_ref[:, cols]
            conv.append(wc[0:1] * u_prev + wc[1:2] * u + wc[2:3] * u_next)
        act = conv[0] * jax.nn.sigmoid(conv[0]) * conv[1]
        acc = acc + jnp.dot(act.astype(BF16), down_ref[j * cw:(j + 1) * cw, :], preferred_element_type=F32)
    o_ref[...] = x + mod[5:6] * _rms(acc, g3)


def ffn_block(x, mod, rows_per_mod, seq_len, norm_g, up_bf16, w_conv, down_bf16, tm=ROW_TILE, cw=FFN_CHUNK):
    m = x.shape[0]
    tm = min(tm, seq_len)
    assert seq_len % tm == 0 and D_FF % cw == 0
    prev_spec, next_spec = _halo_specs(tm, m, D_MODEL)
    return pl.pallas_call(
        functools.partial(_ffn_kernel, seq_tiles=seq_len // tm, cw=cw),
        grid=(m // tm,),
        in_specs=[pl.BlockSpec((tm, D_MODEL), lambda i: (i, 0)), prev_spec, next_spec, _mod_spec(tm, rows_per_mod),
                  _resident(norm_g.shape), _resident(up_bf16.shape), _resident(w_conv.shape),
                  _resident(down_bf16.shape)],
        out_specs=pl.BlockSpec((tm, D_MODEL), lambda i: (i, 0)),
        out_shape=jax.ShapeDtypeStruct((m, D_MODEL), F32),
        compiler_params=pltpu.CompilerParams(dimension_semantics=("parallel",), vmem_limit_bytes=VMEM_LIMIT_BYTES),
        name="ffn_block",
    )(x, x, x, mod, norm_g, up_bf16, w_conv, down_bf16)


def _conv_module_kernel(a_ref, b_ref, ap_ref, bp_ref, an_ref, bn_ref, w_ref, ln_ref, o_ref, pad_scr, *, seq_tiles):
    i = pl.program_id(0)
    tm = a_ref.shape[0]
    halo = 2 * SUBLANES
    keep_prev = jnp.where(i % seq_tiles == 0, 0.0, 1.0)
    keep_next = jnp.where(i % seq_tiles == seq_tiles - 1, 0.0, 1.0)

    def glu(a, b):
        return a * jax.nn.sigmoid(b)

    pad_scr[0:halo] = glu(ap_ref[...], bp_ref[...]) * keep_prev
    pad_scr[halo:halo + tm] = glu(a_ref[...], b_ref[...])
    pad_scr[halo + tm:2 * halo + tm] = glu(an_ref[...], bn_ref[...]) * keep_next
    acc = jnp.zeros((tm, GROUP_W), F32)
    for j in range(CONV_W):
        off = halo - CONV_W // 2 + j
        acc = acc + w_ref[j:j + 1] * pad_scr[off:off + tm]
    xc = acc - jnp.mean(acc, axis=-1, keepdims=True)
    y = xc * lax.rsqrt(jnp.mean(xc * xc, axis=-1, keepdims=True) + EPS) * ln_ref[0:1] + ln_ref[1:2]
    o_ref[...] = y * jax.nn.sigmoid(y)


def conv_module(proj2d, seq_len, w_dw, ln_g, ln_b, tm=ROW_TILE):
    m = proj2d.shape[0]
    tm = min(tm, seq_len)
    G = GROUP_W
    halo = 2 * SUBLANES
    assert CONV_W // 2 <= halo and seq_len % tm == 0
    blocks = tm // halo

    def prev(col):
        return pl.BlockSpec((halo, G), lambda i: (jnp.maximum(i * blocks - 1, 0), col))

    def nxt(col):
        return pl.BlockSpec((halo, G), lambda i: (jnp.minimum((i + 1) * blocks, m // halo - 1), col))

    ln = jnp.stack([ln_g, ln_b])
    return pl.pallas_call(
        functools.partial(_conv_module_kernel, seq_tiles=seq_len // tm),
        grid=(m // tm,),
        in_specs=[pl.BlockSpec((tm, G), lambda i: (i, 3)), pl.BlockSpec((tm, G), lambda i: (i, 4)),
                  prev(3), prev(4), nxt(3), nxt(4), _resident(w_dw.shape), _resident(ln.shape)],
        out_specs=pl.BlockSpec((tm, G), lambda i: (i, 0)),
        out_shape=jax.ShapeDtypeStruct((m, G), F32),
        scratch_shapes=[pltpu.VMEM((tm + 2 * halo, G), F32)],
        compiler_params=pltpu.CompilerParams(dimension_semantics=("parallel",), vmem_limit_bytes=VMEM_LIMIT_BYTES),
        name="conv_module",
    )(proj2d, proj2d, proj2d, proj2d, proj2d, proj2d, w_dw, ln)


def _dot(a, b, dims):
    return lax.dot_general(a, b, (dims, ((), ())), preferred_element_type=F32)


_NN = ((1,), (0,))
_NT = ((1,), (1,))
_TN = ((0,), (0,))


def _bdot(a, b, dims=_NN):
    return _dot(a.astype(BF16), b.astype(BF16), dims)


def _tri_inverse(n_mat, eye, row, col):
    C = n_mat.shape[0]
    nd = jnp.where((row // 8) == (col // 8), n_mat, 0.0)
    s1 = eye + nd
    p1 = _bdot(nd, nd)
    yield
    prod = _bdot(p1, jnp.concatenate([s1, p1], axis=1))
    yield
    s2, p2 = s1 + prod[:, :C], prod[:, C:]
    d = _bdot(p2, s2)
    yield
    d = s2 + d
    size = 8
    while size < C:
        inner = (row // size) == (col // size)
        outer = (row // (2 * size)) == (col // (2 * size))
        n_off = jnp.where(outer & jnp.logical_not(inner), n_mat, 0.0)
        t = _bdot(n_off, d)
        yield
        t = _bdot(d, t)
        yield
        d = d + t
        size *= 2
    return d


def _interleave(gens):
    results = [None] * len(gens)
    active = list(enumerate(gens))
    while active:
        still = []
        for i, g in active:
            try:
                next(g)
                still.append((i, g))
            except StopIteration as stop:
                results[i] = stop.value
        active = still
    return results


def _rwkv_chain(fwd, r, v, kk, lw, k, a, t0, C):
    v = v.astype(BF16)

    row = lax.broadcasted_iota(jnp.int32, (C, C), 0)
    col = lax.broadcasted_iota(jnp.int32, (C, C), 1)
    incl = (row >= col) if fwd else (row <= col)
    strict = (row > col) if fwd else (row < col)
    incl_bf = jnp.where(incl, 1.0, 0.0).astype(BF16)
    eye = jnp.where(row == col, 1.0, 0.0)

    l_hi = lw.astype(BF16)
    l_lo = (lw - l_hi.astype(F32)).astype(BF16)
    cs = _dot(incl_bf, l_hi, _NN) + _dot(incl_bf, l_lo, _NN)
    yield
    tot = jnp.sum(lw, axis=0, keepdims=True)

    beta = kk * a
    e_out = jnp.exp(-cs)
    e_rem = jnp.exp(tot - cs)
    at = -kk * jnp.exp(cs - lw)
    rt = r * jnp.exp(cs)
    ar = jnp.concatenate([at, rt], axis=0).astype(BF16)
    bk = jnp.concatenate([beta * e_out, k * e_out], axis=0)
    bkh = jnp.concatenate([beta * e_rem, k * e_rem], axis=0)

    big = _bdot(ar, bk, _NT)
    x = _bdot(ar, t0)
    yield
    a_ab = jnp.where(strict, big[:C, :C], 0.0)
    a_ak = jnp.where(strict, big[:C, C:], 0.0)
    row2 = lax.broadcasted_iota(jnp.int32, (C, 2 * C), 0)
    col2 = lax.broadcasted_iota(jnp.int32, (C, 2 * C), 1)
    col2 = jnp.where(col2 >= C, col2 - C, col2)
    incl2 = (row2 >= col2) if fwd else (row2 <= col2)
    a_r = jnp.where(incl2, big[C:], 0.0)
    akv = _bdot(a_ak, v)
    minv = yield from _tri_inverse(a_ab, eye, row, col)

    u = _bdot(minv, x[:C] + akv)
    yield
    uv = jnp.concatenate([u.astype(BF16), v], axis=0)
    o = x[C:] + _bdot(a_r, uv)

    drow = lax.broadcasted_iota(jnp.int32, (HEAD_DIM, HEAD_DIM), 0)
    dcol = lax.broadcasted_iota(jnp.int32, (HEAD_DIM, HEAD_DIM), 1)
    gcol = jnp.sum(jnp.where(drow == dcol, jnp.exp(tot), 0.0), axis=1, keepdims=True)
    return o, gcol * t0 + _bdot(bkh, uv, _TN)


def _rwkv_kernel(rf_ref, vf_ref, kkf_ref, lwf_ref, kf_ref, af_ref, rb_ref, vb_ref, kkb_ref, lwb_ref, kb_ref, ab_ref,
                 s0_ref, of_ref, ob_ref, st_ref, t_scr, *, chunk, n_chunks):
    c = pl.program_id(1)

    @pl.when(c == 0)
    def _():
        t_scr[...] = s0_ref[:, 0].reshape(t_scr.shape)

    t_all = t_scr[...]
    fw = [ref[0] for ref in (rf_ref, vf_ref, kkf_ref)] + [ref[0, 0] for ref in (lwf_ref, kf_ref, af_ref)]
    bw = [ref[0] for ref in (rb_ref, vb_ref, kkb_ref)] + [ref[0, 0] for ref in (lwb_ref, kb_ref, ab_ref)]
    chains = []
    for fwd, tiles, base in ((True, fw, 0), (False, bw, N_HEADS)):
        for h in range(N_HEADS):
            sl = slice(h * HEAD_DIM, (h + 1) * HEAD_DIM)
            chains.append(_rwkv_chain(fwd, *[t[:, sl] for t in tiles], t_all[base + h], chunk))
    res = _interleave(chains)
    of_ref[0] = jnp.concatenate([o for o, _ in res[:N_HEADS]], axis=1)
    ob_ref[0] = jnp.concatenate([o for o, _ in res[N_HEADS:]], axis=1)
    t_scr[...] = jnp.stack([t for _, t in res])

    @pl.when(c == n_chunks - 1)
    def _():
        st_ref[:, 0] = t_scr[...].reshape(2, N_HEADS, HEAD_DIM, HEAD_DIM)


def rwkv7_scan(r, v, kk, lw, k, a, s0t, chunk=RWKV_CHUNK):
    B, L, G = r.shape
    n = L // chunk
    assert L % chunk == 0
    sh_f = pl.BlockSpec((1, chunk, G), lambda b, c: (b, c, 0))
    sh_b = pl.BlockSpec((1, chunk, G), lambda b, c: (b, n - 1 - c, 0))
    pd_f = pl.BlockSpec((1, 1, chunk, G), lambda b, c: (0, b, c, 0))
    pd_b = pl.BlockSpec((1, 1, chunk, G), lambda b, c: (1, b, n - 1 - c, 0))
    state = pl.BlockSpec((2, 1, N_HEADS, HEAD_DIM, HEAD_DIM), lambda b, c: (0, b, 0, 0, 0))
    return pl.pallas_call(
        functools.partial(_rwkv_kernel, chunk=chunk, n_chunks=n),
        grid=(B, n),
        in_specs=[sh_f, sh_f, sh_f, pd_f, pd_f, pd_f, sh_b, sh_b, sh_b, pd_b, pd_b, pd_b, state],
        out_specs=[sh_f, sh_b, state],
        out_shape=[jax.ShapeDtypeStruct((B, L, G), F32), jax.ShapeDtypeStruct((B, L, G), F32),
                   jax.ShapeDtypeStruct((2, B, N_HEADS, HEAD_DIM, HEAD_DIM), F32)],
        scratch_shapes=[pltpu.VMEM((2 * N_HEADS, HEAD_DIM, HEAD_DIM), F32)],
        compiler_params=pltpu.CompilerParams(
            dimension_semantics=("parallel", "arbitrary"), vmem_limit_bytes=VMEM_LIMIT_BYTES),
        name="rwkv7_scan",
    )(r, v, kk, lw, k, a, r, v, kk, lw, k, a, s0t)


def _ret_chain(fwd, h, q, k, v, lg, o_ref, s_scr, C):
    sl = slice(h * HEAD_DIM, (h + 1) * HEAD_DIM)
    row = lax.broadcasted_iota(jnp.int32, (C, C), 0)
    col = lax.broadcasted_iota(jnp.int32, (C, C), 1)
    dist = ((row - col) if fwd else (col - row)).astype(F32)
    decay_in = jnp.where(dist >= 0, jnp.exp(jnp.maximum(dist, 0.0) * lg), 0.0)
    pos = lax.broadcasted_iota(jnp.int32, (C, 1), 0).astype(F32)
    step = pos if fwd else (C - 1.0) - pos
    decay_q = jnp.exp((step + 1.0) * lg)
    decay_k = jnp.exp((C - 1.0 - step) * lg)
    qh, kh, vh = q[:, sl], k[:, sl], v[:, sl].astype(BF16)
    si = h if fwd else N_HEADS + h
    s0 = s_scr[si]
    qk = _bdot(qh, kh, _NT)
    qs = _bdot(qh, s0)
    kv = _bdot(kh * decay_k, vh, _TN)
    yield
    ov = _bdot(qk * decay_in, vh)
    yield
    o_ref[0, :, sl] = ov + qs * decay_q
    s_scr[si] = s0 * jnp.exp(C * lg) + kv


def _rope(x, cos, sin):
    lane = lax.broadcasted_iota(jnp.int32, x.shape, 1)
    n = x.shape[1]
    swapped = jnp.where((lane % HEAD_DIM) < HEAD_DIM // 2,
                        pltpu.roll(x, n - HEAD_DIM // 2, axis=1), pltpu.roll(x, HEAD_DIM // 2, axis=1))
    return x * cos + swapped * sin


def _ret_kernel(lg_ref, qf_ref, kf_ref, vf_ref, qb_ref, kb_ref, vb_ref, cf_ref, sf_ref, cb_ref, sb_ref, s0_ref,
                of_ref, ob_ref, st_ref, s_scr, *, chunk, n_chunks, rope):
    c = pl.program_id(1)

    @pl.when(c == 0)
    def _():
        s_scr[...] = s0_ref[0].reshape(s_scr.shape)

    scale = HEAD_DIM ** -0.5
    qf, kf, qb, kb = qf_ref[0], kf_ref[0] * scale, qb_ref[0], kb_ref[0] * scale
    if rope:
        qf, kf = _rope(qf, cf_ref[...], sf_ref[...]), _rope(kf, cf_ref[...], sf_ref[...])
        qb, kb = _rope(qb, cb_ref[...], sb_ref[...]), _rope(kb, cb_ref[...], sb_ref[...])
    vf, vb = vf_ref[0], vb_ref[0]
    _interleave([_ret_chain(True, h, qf, kf, vf, lg_ref[0, h], of_ref, s_scr, chunk) for h in range(N_HEADS)]
                + [_ret_chain(False, h, qb, kb, vb, lg_ref[1, h], ob_ref, s_scr, chunk) for h in range(N_HEADS)])

    @pl.when(c == n_chunks - 1)
    def _():
        st_ref[0] = s_scr[...].reshape(2, N_HEADS, HEAD_DIM, HEAD_DIM)


def rope_tables(length):
    t = jnp.arange(length)
    n_freq = HEAD_DIM // 4
    inv = ROPE_BASE ** (-jnp.arange(n_freq, dtype=F32) / n_freq)
    ang = jnp.concatenate([(t // GRID_W).astype(F32)[:, None] * inv,
                           (t % GRID_W).astype(F32)[:, None] * inv], axis=-1)
    cos, sin = jnp.cos(ang), jnp.sin(ang)
    return (jnp.tile(jnp.concatenate([cos, cos], axis=-1), (1, N_HEADS)),
            jnp.tile(jnp.concatenate([-sin, sin], axis=-1), (1, N_HEADS)))


def retention(proj, ret_decay, s0, rope, chunk=RET_CHUNK):
    B, L, _ = proj.shape
    G = GROUP_W
    n = L // chunk
    assert L % chunk == 0
    log_g = jax.nn.log_sigmoid(ret_decay)
    cos, sin = rope_tables(L)
    fw = lambda j: pl.BlockSpec((1, chunk, G), lambda b, c: (b, c, j))
    bw = lambda j: pl.BlockSpec((1, chunk, G), lambda b, c: (b, n - 1 - c, j))
    tab_f = pl.BlockSpec((chunk, G), lambda b, c: (c, 0))
    tab_b = pl.BlockSpec((chunk, G), lambda b, c: (n - 1 - c, 0))
    state = pl.BlockSpec((1, 2, N_HEADS, HEAD_DIM, HEAD_DIM), lambda b, c: (b, 0, 0, 0, 0))
    return pl.pallas_call(
        functools.partial(_ret_kernel, chunk=chunk, n_chunks=n, rope=rope),
        grid=(B, n),
        in_specs=[pl.BlockSpec(memory_space=pltpu.SMEM), fw(5), fw(6), fw(7), bw(5), bw(6), bw(7),
                  tab_f, tab_f, tab_b, tab_b, state],
        out_specs=[pl.BlockSpec((1, chunk, G), lambda b, c: (b, c, 0)),
                   pl.BlockSpec((1, chunk, G), lambda b, c: (b, n - 1 - c, 0)), state],
        out_shape=[jax.ShapeDtypeStruct((B, L, G), F32), jax.ShapeDtypeStruct((B, L, G), F32),
                   jax.ShapeDtypeStruct((B, 2, N_HEADS, HEAD_DIM, HEAD_DIM), F32)],
        scratch_shapes=[pltpu.VMEM((2 * N_HEADS, HEAD_DIM, HEAD_DIM), F32)],
        compiler_params=pltpu.CompilerParams(
            dimension_semantics=("parallel", "arbitrary"), vmem_limit_bytes=VMEM_LIMIT_BYTES),
        name="retention",
    )(log_g, proj, proj, proj, proj, proj, proj, cos, sin, cos, sin, s0)


def _softmax_pv(s_list, v_list):
    m = s_list[0].max(axis=-1, keepdims=True)
    for s in s_list[1:]:
        m = jnp.maximum(m, s.max(axis=-1, keepdims=True))
    den = 0.0
    acc = 0.0
    for s, v in zip(s_list, v_list):
        p = jnp.exp(s - m)
        den = den + p.sum(axis=-1, keepdims=True)
        acc = acc + _dot(p.astype(BF16), v, _NN)
    yield
    return acc / den


def _na_kernel(q_ref, k_ref, v_ref, ck_ref, cv_ref, tb_ref, mask_ref, o_ref, *, rows, kh):
    r = pl.program_id(1)
    rs = jnp.clip(r - kh // 2, 0, rows - kh)
    pat = r - rs
    start = pl.multiple_of(rs * GRID_W, GRID_W)
    kwin = k_ref[0, pl.ds(start, kh * GRID_W), :].astype(BF16)
    vwin = v_ref[0, pl.ds(start, kh * GRID_W), :].astype(BF16)
    ck = ck_ref[0, 0].astype(BF16)
    cv = cv_ref[0, 0].astype(BF16)
    q = (q_ref[0] * (HEAD_DIM ** -0.5)).astype(BF16)
    mask = mask_ref[...] > 0.0
    def head(h):
        sl = slice(h * HEAD_DIM, (h + 1) * HEAD_DIM)
        qh = q[:, sl]
        s_loc = _dot(qh, kwin[:, sl], _NT)
        s_ctx = _dot(qh, ck[:, sl], _NT)
        yield
        s_loc = jnp.where(mask, s_loc + tb_ref[h, pat], NEG_INF)
        return (yield from _softmax_pv([s_loc, s_ctx], [vwin[:, sl], cv[:, sl]]))

    o_ref[0] = jnp.concatenate(_interleave([head(h) for h in range(N_HEADS)]), axis=1)


def na_bias_table(rpb, kh):
    col = np.arange(GRID_W)
    d_col = np.clip(col[None, :] - col[:, None], -(NA_WIN_W - 1), NA_WIN_W - 1) + (NA_WIN_W - 1)
    onehot = jnp.asarray(d_col[:, :, None] == np.arange(2 * NA_WIN_W - 1), F32)
    rows = jnp.stack([rpb[:, NA_WIN_H - 1 - p:NA_WIN_H - 1 - p + kh] for p in range(kh)], axis=1)
    tb = jnp.einsum("hpic,qkc->hpqik", rows, onehot, precision=lax.Precision.HIGHEST)
    col_start = np.clip(col - NA_WIN_W // 2, 0, GRID_W - NA_WIN_W)
    col_in = (col[None, :] >= col_start[:, None]) & (col[None, :] < col_start[:, None] + NA_WIN_W)
    mask = jnp.asarray(np.tile(col_in.astype(np.float32), (1, kh)))
    return tb.reshape(N_HEADS, kh, GRID_W, kh * GRID_W), mask


def neighbourhood_attention(proj, cache_k, cache_v, layer, rpb):
    B, L, _ = proj.shape
    rows = L // GRID_W
    kh = NA_WIN_H
    assert rows >= kh
    Lc = cache_k.shape[2]
    tb, mask = na_bias_table(rpb, kh)
    G = GROUP_W
    return pl.pallas_call(
        functools.partial(_na_kernel, rows=rows, kh=kh),
        grid=(B, rows),
        in_specs=[pl.BlockSpec((1, GRID_W, G), lambda b, r: (b, r, 0)),
                  pl.BlockSpec((1, L, G), lambda b, r: (b, 0, 1)),
                  pl.BlockSpec((1, L, G), lambda b, r: (b, 0, 2)),
                  pl.BlockSpec((1, 1, Lc, G), lambda b, r: (b, layer, 0, 0)),
                  pl.BlockSpec((1, 1, Lc, G), lambda b, r: (b, layer, 0, 0)),
                  pl.BlockSpec((N_HEADS, kh, GRID_W, kh * GRID_W), lambda b, r: (0, 0, 0, 0)),
                  pl.BlockSpec((GRID_W, kh * GRID_W), lambda b, r: (0, 0))],
        out_specs=pl.BlockSpec((1, GRID_W, G), lambda b, r: (b, r, 0)),
        out_shape=jax.ShapeDtypeStruct((B, L, G), F32),
        compiler_params=pltpu.CompilerParams(
            dimension_semantics=("parallel", "arbitrary"), vmem_limit_bytes=VMEM_LIMIT_BYTES),
        name="na_attention",
    )(proj, proj, proj, cache_k, cache_v, tb, mask)


def _ctx_attn_kernel(q_ref, k_ref, v_ref, o_ref):
    q = (q_ref[0] * (HEAD_DIM ** -0.5)).astype(BF16)
    k = k_ref[0].astype(BF16)
    v = v_ref[0].astype(BF16)
    def head(h):
        sl = slice(h * HEAD_DIM, (h + 1) * HEAD_DIM)
        s = _dot(q[:, sl], k[:, sl], _NT)
        yield
        return (yield from _softmax_pv([s], [v[:, sl]]))

    o_ref[0] = jnp.concatenate(_interleave([head(h) for h in range(N_HEADS)]), axis=1)


def context_attention(proj):
    B, L, _ = proj.shape
    G = GROUP_W
    return pl.pallas_call(
        _ctx_attn_kernel,
        grid=(B,),
        in_specs=[pl.BlockSpec((1, L, G), lambda b: (b, 0, 0)),
                  pl.BlockSpec((1, L, G), lambda b: (b, 0, 1)),
                  pl.BlockSpec((1, L, G), lambda b: (b, 0, 2))],
        out_specs=pl.BlockSpec((1, L, G), lambda b: (b, 0, 0)),
        out_shape=jax.ShapeDtypeStruct((B, L, G), F32),
        compiler_params=pltpu.CompilerParams(dimension_semantics=("parallel",), vmem_limit_bytes=VMEM_LIMIT_BYTES),
        name="ctx_attention",
    )(proj, proj, proj)


def rms_norm(x, g):
    return x * lax.rsqrt(jnp.mean(x * x, axis=-1, keepdims=True) + EPS) * g


def layer_norm(x, g, b):
    xc = x - jnp.mean(x, axis=-1, keepdims=True)
    return xc * lax.rsqrt(jnp.mean(xc * xc, axis=-1, keepdims=True) + EPS) * g + b


def head_norm(x, g, eps):
    xc = x - jnp.mean(x, axis=-1, keepdims=True)
    y = xc * lax.rsqrt(jnp.mean(xc * xc, axis=-1, keepdims=True) + eps)
    return y.reshape(x.shape[0], x.shape[1], -1) * g


def dwconv(x, w):
    k, ch = w.shape
    return lax.conv_general_dilated(x, w[:, None, :], window_strides=(1,), padding=[(k // 2, k // 2)],
                                    dimension_numbers=("NWC", "WIO", "NWC"), feature_group_count=ch)


def token_mixers(x, mod, rows_per_mod, p, cache):
    B, L, _ = x.shape
    latent = cache is not None
    x2d = x.reshape(B * L, D_MODEL)
    proj2d = norm_matmul(x2d, mod, rows_per_mod, p["norm_g"], p["w_in"])
    proj = proj2d.reshape(B, L, -1)
    a_k = proj[..., GROUP_W:2 * GROUP_W].reshape(B, L, N_HEADS, HEAD_DIM)
    a_v = proj[..., 2 * GROUP_W:3 * GROUP_W].reshape(B, L, N_HEADS, HEAD_DIM)
    if latent:
        o_a = neighbourhood_attention(proj, cache["na_k"], cache["na_v"], cache["layer"], p["na_rpb"])
    else:
        o_a = context_attention(proj)

    o_b = conv_module(proj2d, L, p["conv_dw"], p["conv_ln_g"], p["conv_ln_b"])

    ret0 = cache["ret"] if latent else jnp.zeros((B, 2, N_HEADS, HEAD_DIM, HEAD_DIM), F32)
    o_cf, o_cb, ret_state = retention(proj, p["ret_decay"], ret0, rope=latent)

    r, v, kk, lw2, k2, a2, bonus, gate = rwkv_prep(proj2d, L, p)
    rwkv0 = cache["rwkv"] if latent else jnp.zeros((B, 2, N_HEADS, HEAD_DIM, HEAD_DIM), F32)
    s0t = jnp.swapaxes(rwkv0, -1, -2).transpose(1, 0, 2, 3, 4)
    tok = lambda t: t.reshape(t.shape[:-2] + (B, L, GROUP_W))
    o_df, o_db, st = rwkv7_scan(tok(r), tok(v), tok(kk), tok(lw2), tok(k2), tok(a2), s0t)
    rwkv_state = jnp.swapaxes(st, -1, -2).transpose(1, 0, 2, 3, 4)

    flat = lambda t: t.reshape(B * L, GROUP_W)
    x2d = mix_out(flat(o_a), o_b, flat(o_cf), flat(o_cb), proj2d, flat(o_df), flat(o_db), bonus, gate, x2d, mod,
                  rows_per_mod, p["norm_g"], p["ret_gn"], p["rwkv_gn"], p["w_out"])
    return x2d, (None if latent else (a_k, a_v, ret_state, rwkv_state))


def trunk_layer(x, mod_vec, p, cache):
    B, L, _ = x.shape
    n_mod = mod_vec.shape[0]
    rows_per_mod = B * L // n_mod
    mod = jnp.pad(mod_vec.reshape(n_mod, 6, D_MODEL), ((0, 0), (0, MOD_ROWS - 6), (0, 0)))
    x2d, state = token_mixers(x, mod, rows_per_mod, p, cache)
    x2d = ffn_block(x2d, mod, rows_per_mod, L, p["norm_g"], p["ffn_up"], p["ffn_conv"], p["ffn_down"])
    return x2d.reshape(B, L, D_MODEL), state


def kernel(x_prompt, x_sample, cache_na_k, cache_na_v, state_retention, state_rwkv, c, c_ctx, ada_w, ada_b, norm_g, w_in, w_out, na_rpb, conv_dw, conv_ln_g, conv_ln_b, ret_decay, ret_gn, rwkv_shift, rwkv_w0, rwkv_w2, rwkv_a0, rwkv_a2, rwkv_g2, rwkv_kk, rwkv_ka, rwkv_rk, rwkv_gn, ffn_up, ffn_conv, ffn_down):
    y_prompt, y_sample = x_prompt, x_sample
    new_k, new_v, new_ret, new_rwkv = [], [], [], []
    for l in range(DEPTH):
        p = {"norm_g": norm_g[l], "w_in": w_in[l].astype(BF16), "w_out": w_out[l].astype(BF16), "na_rpb": na_rpb[l],
             "conv_dw": conv_dw[l], "conv_ln_g": conv_ln_g[l], "conv_ln_b": conv_ln_b[l],
             "ret_decay": ret_decay[l], "ret_gn": ret_gn[l], "rwkv_shift": rwkv_shift[l],
             "rwkv_w0": rwkv_w0[l], "rwkv_w2": rwkv_w2[l], "rwkv_a0": rwkv_a0[l], "rwkv_a2": rwkv_a2[l],
             "rwkv_g2": rwkv_g2[l], "rwkv_kk": rwkv_kk[l], "rwkv_ka": rwkv_ka[l], "rwkv_rk": rwkv_rk[l],
             "rwkv_gn": rwkv_gn[l], "ffn_up": ffn_up[l].astype(BF16), "ffn_conv": ffn_conv[l],
             "ffn_down": ffn_down[l].astype(BF16)}
        mod_ctx = (jax.nn.silu(c_ctx) @ ada_w[l] + ada_b[l])[None]
        y_prompt, (k_l, v_l, ret_l, rwkv_l) = trunk_layer(y_prompt, mod_ctx, p, None)
        new_k.append(k_l)
        new_v.append(v_l)
        new_ret.append(ret_l)
        new_rwkv.append(rwkv_l)
        mod_lat = jax.nn.silu(c) @ ada_w[l] + ada_b[l]
        cache_l = {"na_k": cache_na_k.reshape(cache_na_k.shape[:3] + (GROUP_W,)),
                   "na_v": cache_na_v.reshape(cache_na_v.shape[:3] + (GROUP_W,)), "layer": l,
                   "ret": state_retention[:, l], "rwkv": state_rwkv[:, l]}
        y_sample, _ = trunk_layer(y_sample, mod_lat, p, cache_l)
    return (y_prompt, y_sample, jnp.stack(new_k, axis=1), jnp.stack(new_v, axis=1),
            jnp.stack(new_ret, axis=1), jnp.stack(new_rwkv, axis=1))
```

```python
import functools

import jax
import jax.numpy as jnp
import numpy as np
from jax import lax
from jax.experimental import pallas as pl
from jax.experimental.pallas import tpu as pltpu

F32 = jnp.float32
BF16 = jnp.bfloat16

D_MODEL = 1024
DEPTH = 2
GRID_W = 64
N_MIXERS = 4
GROUP_W = D_MODEL // N_MIXERS
HEAD_DIM = 64
N_HEADS = GROUP_W // HEAD_DIM
NA_WIN_H = 8
NA_WIN_W = 16
Q_BLOCK = 128
CONV_W = 31
RET_CHUNK = 128
RWKV_LORA_W = 64
RWKV_LORA_A = 64
RWKV_LORA_G = 128
D_FF = 2816
ROPE_BASE = 10000.0
EPS = 1e-6
RWKV_GN_EPS = 64e-5
NEG_INF = -1e30
SPLIT_SIZES = (GROUP_W,) * 12 + (RWKV_LORA_W, RWKV_LORA_A, RWKV_LORA_G)
SPLIT_POINTS = tuple(int(s) for s in np.cumsum(SPLIT_SIZES)[:-1])

RWKV_CHUNK = 64
RET_REQUESTS_PER_STEP = 2
RWKV_REQUESTS_PER_STEP = 4
VMEM_LIMIT_BYTES = 48 * 1024 * 1024
SUBLANES = 8
MOD_ROWS = 8
NA_ROWS_PER_STEP = 8
ROW_TILE = 512
FFN_CHUNK = 256


def _mm_kernel(a_ref, b_ref, o_ref):
    o_ref[...] = jnp.dot(a_ref[...].astype(BF16), b_ref[...], preferred_element_type=F32)


def _pick_tile(n, target):
    best = 128
    for t in range(128, target + 1, 128):
        if n % t == 0:
            best = t
    return best


def matmul(a, b_bf16, tm=512, tn_target=1792):
    m, k = a.shape
    n = b_bf16.shape[1]
    tn = _pick_tile(n, tn_target)
    tm = min(tm, m)
    assert m % tm == 0 and n % tn == 0
    return pl.pallas_call(
        _mm_kernel,
        grid=(n // tn, m // tm),
        in_specs=[pl.BlockSpec((tm, k), lambda j, i: (i, 0)),
                  pl.BlockSpec((k, tn), lambda j, i: (0, j))],
        out_specs=pl.BlockSpec((tm, tn), lambda j, i: (i, j)),
        out_shape=jax.ShapeDtypeStruct((m, n), F32),
        compiler_params=pltpu.CompilerParams(
            dimension_semantics=("parallel", "parallel"), vmem_limit_bytes=VMEM_LIMIT_BYTES),
        name="matmul",
    )(a, b_bf16)


def _rms(x, g):
    return x * lax.rsqrt(jnp.mean(x * x, axis=-1, keepdims=True) + EPS) * g


def _resident(shape):
    return pl.BlockSpec(shape, lambda *_: (0,) * len(shape), pipeline_mode=pl.Buffered(1))


def _mod_spec(tm, rows_per_mod):
    return pl.BlockSpec((1, MOD_ROWS, D_MODEL), lambda i: ((i * tm) // rows_per_mod, 0, 0))


def _norm_mm_kernel(x_ref, mod_ref, g_ref, w_ref, o_ref):
    mod = mod_ref[0]
    h = _rms(x_ref[...], g_ref[0:1]) * (1.0 + mod[1:2]) + mod[0:1]
    o_ref[...] = jnp.dot(h.astype(BF16), w_ref[...], preferred_element_type=F32)


def norm_matmul(x, mod, rows_per_mod, norm_g, w_bf16, tm=ROW_TILE):
    m, n = x.shape[0], w_bf16.shape[1]
    tm = min(tm, rows_per_mod)
    return pl.pallas_call(
        _norm_mm_kernel,
        grid=(m // tm,),
        in_specs=[pl.BlockSpec((tm, D_MODEL), lambda i: (i, 0)), _mod_spec(tm, rows_per_mod),
                  _resident(norm_g.shape), _resident(w_bf16.shape)],
        out_specs=pl.BlockSpec((tm, n), lambda i: (i, 0)),
        out_shape=jax.ShapeDtypeStruct((m, n), F32),
        compiler_params=pltpu.CompilerParams(dimension_semantics=("parallel",), vmem_limit_bytes=VMEM_LIMIT_BYTES),
        name="norm_matmul",
    )(x, mod, norm_g, w_bf16)


def _head_sum(y):
    g = y.shape[1]
    row = lax.broadcasted_iota(jnp.int32, (g, g), 0) // HEAD_DIM
    col = lax.broadcasted_iota(jnp.int32, (g, g), 1) // HEAD_DIM
    ones = jnp.where(row == col, 1.0, 0.0).astype(BF16)
    hi = y.astype(BF16)
    lo = (y - hi.astype(F32)).astype(BF16)
    return jnp.dot(hi, ones, preferred_element_type=F32) + jnp.dot(lo, ones, preferred_element_type=F32)


def _head_norm(y, g, eps):
    yc = y - _head_sum(y) * (1.0 / HEAD_DIM)
    return yc * lax.rsqrt(_head_sum(yc * yc) * (1.0 / HEAD_DIM) + eps) * g


def _mix_out_kernel(oa_ref, ob_ref, cf_ref, cb_ref, rg_ref, df_ref, db_ref, bonus_ref, gate_ref, x_ref, mod_ref,
                    g_ref, gn_ref, w_ref, o_ref):
    G = GROUP_W
    rg = rg_ref[...]
    o_c = _head_norm(cf_ref[...] + cb_ref[...], gn_ref[0:1], EPS) * (rg * jax.nn.sigmoid(rg))
    o_d = (_head_norm(df_ref[...] + db_ref[...], gn_ref[1:2], RWKV_GN_EPS) + bonus_ref[...]) * gate_ref[...]
    m = 0.0
    for j, o in enumerate((oa_ref[...], ob_ref[...], o_c, o_d)):
        m = m + jnp.dot(o.astype(BF16), w_ref[j * G:(j + 1) * G, :], preferred_element_type=F32)
    o_ref[...] = x_ref[...] + mod_ref[0][2:3] * _rms(m, g_ref[1:2])


def mix_out(o_a, o_b, o_cf, o_cb, proj2d, o_df, o_db, bonus, gate, x, mod, rows_per_mod, norm_g, ret_gn, rwkv_gn,
            w_bf16, tm=ROW_TILE):
    m = x.shape[0]
    tm = min(tm, rows_per_mod)
    G = GROUP_W
    grp = pl.BlockSpec((tm, G), lambda i: (i, 0))
    gn = jnp.stack([ret_gn, rwkv_gn])
    return pl.pallas_call(
        _mix_out_kernel,
        grid=(m // tm,),
        in_specs=[grp, grp, grp, grp, pl.BlockSpec((tm, G), lambda i: (i, 8)), grp, grp, grp, grp,
                  pl.BlockSpec((tm, D_MODEL), lambda i: (i, 0)), _mod_spec(tm, rows_per_mod),
                  _resident(norm_g.shape), _resident(gn.shape), _resident(w_bf16.shape)],
        out_specs=pl.BlockSpec((tm, D_MODEL), lambda i: (i, 0)),
        out_shape=jax.ShapeDtypeStruct((m, D_MODEL), F32),
        compiler_params=pltpu.CompilerParams(dimension_semantics=("parallel",), vmem_limit_bytes=VMEM_LIMIT_BYTES),
        name="mix_out",
    )(o_a, o_b, o_cf, o_cb, proj2d, o_df, o_db, bonus, gate, x, mod, norm_g, gn, w_bf16)


def _rwkv_prep_kernel(rkv_ref, rkvp_ref, rkvn_ref, lora_ref, sh_ref, w0_ref, w2_ref, a0_ref, a2_ref, g2_ref, vec_ref,
                      r_ref, v_ref, kk_ref, lw_ref, k_ref, a_ref, bonus_ref, gate_ref, *, seq_tiles):
    i = pl.program_id(0)
    G = GROUP_W
    keep_prev = jnp.where(i % seq_tiles == 0, 0.0, 1.0)
    keep_next = jnp.where(i % seq_tiles == seq_tiles - 1, 0.0, 1.0)
    x = rkv_ref[...]
    x_prev, x_next = _shift_rows(x, rkvp_ref[SUBLANES - 1:SUBLANES] * keep_prev, rkvn_ref[0:1] * keep_next)
    rkv = sh_ref[0:1] * x_prev + sh_ref[1:2] * x + sh_ref[2:3] * x_next
    d_r, d_k, d_v = rkv[:, :G], rkv[:, G:2 * G], rkv[:, 2 * G:]
    lora = lora_ref[...]
    w_low = jnp.tanh(lora[:, :RWKV_LORA_W]).astype(BF16)
    a_low = lora[:, RWKV_LORA_W:RWKV_LORA_W + RWKV_LORA_A].astype(BF16)
    g_low = jax.nn.sigmoid(lora[:, RWKV_LORA_W + RWKV_LORA_A:]).astype(BF16)
    kk_w, ka, rk = vec_ref[0:1], vec_ref[1:2], vec_ref[2:3]
    k_sum = 0.0
    for d in range(2):
        z = w0_ref[d:d + 1] + jnp.dot(w_low, w2_ref[d].astype(BF16), preferred_element_type=F32)
        lw_ref[d] = -float(np.exp(-0.5)) * jax.nn.sigmoid(z)
        a = jax.nn.sigmoid(a0_ref[d:d + 1] + jnp.dot(a_low, a2_ref[d].astype(BF16), preferred_element_type=F32))
        k_d = d_k * (1.0 + (a - 1.0) * ka)
        a_ref[d] = a
        k_ref[d] = k_d
        k_sum = k_sum + k_d
    kk = d_k * kk_w
    r_ref[...] = d_r
    v_ref[...] = d_v
    kk_ref[...] = kk * lax.rsqrt(_head_sum(kk * kk) + 1e-12)
    bonus_ref[...] = _head_sum(d_r * k_sum * rk) * d_v
    gate_ref[...] = jnp.dot(g_low, g2_ref[...].astype(BF16), preferred_element_type=F32)


def rwkv_prep(proj2d, seq_len, p, tm=ROW_TILE):
    m = proj2d.shape[0]
    tm = min(tm, seq_len)
    G = GROUP_W
    assert seq_len % tm == 0
    prev_spec, next_spec = _halo_specs(tm, m, 3 * G, col=3)
    vec = jnp.stack([p["rwkv_kk"], p["rwkv_ka"], p["rwkv_rk"]])
    one = pl.BlockSpec((tm, G), lambda i: (i, 0))
    two = pl.BlockSpec((2, tm, G), lambda i: (0, i, 0))
    sds1 = jax.ShapeDtypeStruct((m, G), F32)
    sds2 = jax.ShapeDtypeStruct((2, m, G), F32)
    params = (p["rwkv_shift"], p["rwkv_w0"], p["rwkv_w2"], p["rwkv_a0"], p["rwkv_a2"], p["rwkv_g2"], vec)
    return pl.pallas_call(
        functools.partial(_rwkv_prep_kernel, seq_tiles=seq_len // tm),
        grid=(m // tm,),
        in_specs=[pl.BlockSpec((tm, 3 * G), lambda i: (i, 3)), prev_spec, next_spec,
                  pl.BlockSpec((tm, G), lambda i: (i, 12))] + [_resident(t.shape) for t in params],
        out_specs=[one, one, one, two, two, two, one, one],
        out_shape=[sds1, sds1, sds1, sds2, sds2, sds2, sds1, sds1],
        compiler_params=pltpu.CompilerParams(dimension_semantics=("parallel",), vmem_limit_bytes=VMEM_LIMIT_BYTES),
        name="rwkv_prep",
    )(proj2d, proj2d, proj2d, proj2d, *params)


def _halo_specs(tm, m, width, col=0):
    blocks = tm // SUBLANES
    return (pl.BlockSpec((SUBLANES, width), lambda i: (jnp.maximum(i * blocks - 1, 0), col)),
            pl.BlockSpec((SUBLANES, width), lambda i: (jnp.minimum((i + 1) * blocks, m // SUBLANES - 1), col)))


def _shift_rows(u, prev_row, next_row):
    tm = u.shape[0]
    row = lax.broadcasted_iota(jnp.int32, (tm, 1), 0)
    u_prev = jnp.where(row == 0, prev_row, pltpu.roll(u, 1, axis=0))
    u_next = jnp.where(row == tm - 1, next_row, pltpu.roll(u, tm - 1, axis=0))
    return u_prev, u_next


def _ffn_kernel(x_ref, xp_ref, xn_ref, mod_ref, g_ref, up_ref, cw_ref, down_ref, o_ref, *, seq_tiles, cw):
    i = pl.program_id(0)
    keep_prev = jnp.where(i % seq_tiles == 0, 0.0, 1.0)
    keep_next = jnp.where(i % seq_tiles == seq_tiles - 1, 0.0, 1.0)
    mod = mod_ref[0]
    g2, g3 = g_ref[2:3], g_ref[3:4]

    def pre(x):
        return (_rms(x, g2) * (1.0 + mod[4:5]) + mod[3:4]).astype(BF16)

    x = x_ref[...]
    h = pre(x)
    hh = pre(jnp.concatenate([xp_ref[...], xn_ref[...]], axis=0))
    def cols(j, half):
        return slice(half * D_FF + j * cw, half * D_FF + (j + 1) * cw)

    def up(j):
        return [(jnp.dot(h, up_ref[:, cols(j, half)], preferred_element_type=F32),
                 jnp.dot(hh, up_ref[:, cols(j, half)], preferred_element_type=F32)) for half in range(2)]

    def conv_act(j, ups):
        conv = []
        for half, (u, uh) in enumerate(ups):
            u_prev, u_next = _shift_rows(u, uh[SUBLANES - 1:SUBLANES] * keep_prev, uh[SUBLANES:SUBLANES + 1] * keep_next)
            wc = cw_ref[:, cols(j, half)]
            conv.append(wc[0:1] * u_prev + wc[1:2] * u + wc[2:3] * u_next)
        return (conv[0] * jax.nn.sigmoid(conv[0]) * conv[1]).astype(BF16)

    n = D_FF // cw
    acc = jnp.zeros(x.shape, F32)
    nxt = up(0)
    for j in range(n):
        cur, nxt = nxt, (up(j + 1) if j + 1 < n else None)
        acc = acc + jnp.dot(conv_act(j, cur), down_ref[j * cw:(j + 1) * cw, :], preferred_element_type=F32)
    o_ref[...] = x + mod[5:6] * _rms(acc, g3)


def ffn_block(x, mod, rows_per_mod, seq_len, norm_g, up_bf16, w_conv, down_bf16, tm=ROW_TILE, cw=FFN_CHUNK):
    m = x.shape[0]
    tm = min(tm, seq_len)
    assert seq_len % tm == 0 and D_FF % cw == 0
    prev_spec, next_spec = _halo_specs(tm, m, D_MODEL)
    return pl.pallas_call(
        functools.partial(_ffn_kernel, seq_tiles=seq_len // tm, cw=cw),
        grid=(m // tm,),
        in_specs=[pl.BlockSpec((tm, D_MODEL), lambda i: (i, 0)), prev_spec, next_spec, _mod_spec(tm, rows_per_mod),
                  _resident(norm_g.shape), _resident(up_bf16.shape), _resident(w_conv.shape),
                  _resident(down_bf16.shape)],
        out_specs=pl.BlockSpec((tm, D_MODEL), lambda i: (i, 0)),
        out_shape=jax.ShapeDtypeStruct((m, D_MODEL), F32),
        compiler_params=pltpu.CompilerParams(dimension_semantics=("parallel",), vmem_limit_bytes=VMEM_LIMIT_BYTES),
        name="ffn_block",
    )(x, x, x, mod, norm_g, up_bf16, w_conv, down_bf16)


def _conv_module_kernel(a_ref, b_ref, ap_ref, bp_ref, an_ref, bn_ref, w_ref, ln_ref, o_ref, pad_scr, *, seq_tiles):
    i = pl.program_id(0)
    tm = a_ref.shape[0]
    halo = 2 * SUBLANES
    keep_prev = jnp.where(i % seq_tiles == 0, 0.0, 1.0)
    keep_next = jnp.where(i % seq_tiles == seq_tiles - 1, 0.0, 1.0)

    def glu(a, b):
        return a * jax.nn.sigmoid(b)

    pad_scr[0:halo] = glu(ap_ref[...], bp_ref[...]) * keep_prev
    pad_scr[halo:halo + tm] = glu(a_ref[...], b_ref[...])
    pad_scr[halo + tm:2 * halo + tm] = glu(an_ref[...], bn_ref[...]) * keep_next
    acc = jnp.zeros((tm, GROUP_W), F32)
    for j in range(CONV_W):
        off = halo - CONV_W // 2 + j
        acc = acc + w_ref[j:j + 1] * pad_scr[off:off + tm]
    xc = acc - jnp.mean(acc, axis=-1, keepdims=True)
    y = xc * lax.rsqrt(jnp.mean(xc * xc, axis=-1, keepdims=True) + EPS) * ln_ref[0:1] + ln_ref[1:2]
    o_ref[...] = y * jax.nn.sigmoid(y)


def conv_module(proj2d, seq_len, w_dw, ln_g, ln_b, tm=ROW_TILE):
    m = proj2d.shape[0]
    tm = min(tm, seq_len)
    G = GROUP_W
    halo = 2 * SUBLANES
    assert CONV_W // 2 <= halo and seq_len % tm == 0
    blocks = tm // halo

    def prev(col):
        return pl.BlockSpec((halo, G), lambda i: (jnp.maximum(i * blocks - 1, 0), col))

    def nxt(col):
        return pl.BlockSpec((halo, G), lambda i: (jnp.minimum((i + 1) * blocks, m // halo - 1), col))

    ln = jnp.stack([ln_g, ln_b])
    return pl.pallas_call(
        functools.partial(_conv_module_kernel, seq_tiles=seq_len // tm),
        grid=(m // tm,),
        in_specs=[pl.BlockSpec((tm, G), lambda i: (i, 3)), pl.BlockSpec((tm, G), lambda i: (i, 4)),
                  prev(3), prev(4), nxt(3), nxt(4), _resident(w_dw.shape), _resident(ln.shape)],
        out_specs=pl.BlockSpec((tm, G), lambda i: (i, 0)),
        out_shape=jax.ShapeDtypeStruct((m, G), F32),
        scratch_shapes=[pltpu.VMEM((tm + 2 * halo, G), F32)],
        compiler_params=pltpu.CompilerParams(dimension_semantics=("parallel",), vmem_limit_bytes=VMEM_LIMIT_BYTES),
        name="conv_module",
    )(proj2d, proj2d, proj2d, proj2d, proj2d, proj2d, w_dw, ln)


def _dot(a, b, dims):
    return lax.dot_general(a, b, (dims, ((), ())), preferred_element_type=F32)


_NN = ((1,), (0,))
_NT = ((1,), (1,))
_TN = ((0,), (0,))


def _bdot(a, b, dims=_NN):
    return _dot(a.astype(BF16), b.astype(BF16), dims)


def _tri_inverse(n_mat, eye, row, col):
    C = n_mat.shape[0]
    nd = jnp.where((row // 8) == (col // 8), n_mat, 0.0)
    s1 = eye + nd
    p1 = _bdot(nd, nd)
    yield
    s2 = _bdot(p1, s1)
    p2 = _bdot(p1, p1)
    yield
    s2 = s1 + s2
    d = _bdot(p2, s2)
    yield
    d = s2 + d
    size = 8
    while size < C:
        inner = (row // size) == (col // size)
        outer = (row // (2 * size)) == (col // (2 * size))
        n_off = jnp.where(outer & jnp.logical_not(inner), n_mat, 0.0)
        t = _bdot(n_off, d)
        yield
        t = _bdot(d, t)
        yield
        d = d + t
        size *= 2
    return d


def _interleave(gens):
    results = [None] * len(gens)
    active = list(enumerate(gens))
    while active:
        still = []
        for i, g in active:
            try:
                next(g)
                still.append((i, g))
            except StopIteration as stop:
                results[i] = stop.value
        active = still
    return results


def _rwkv_chain(fwd, r, v, kk, lw, k, a, t0, C):
    v = v.astype(BF16)

    row = lax.broadcasted_iota(jnp.int32, (C, C), 0)
    col = lax.broadcasted_iota(jnp.int32, (C, C), 1)
    incl = (row >= col) if fwd else (row <= col)
    strict = (row > col) if fwd else (row < col)
    incl_bf = jnp.where(incl, 1.0, 0.0).astype(BF16)
    eye = jnp.where(row == col, 1.0, 0.0)

    l_hi = lw.astype(BF16)
    l_lo = (lw - l_hi.astype(F32)).astype(BF16)
    cs = _dot(incl_bf, l_hi, _NN) + _dot(incl_bf, l_lo, _NN)
    yield
    tot = jnp.sum(lw, axis=0, keepdims=True)

    beta = kk * a
    e_out = jnp.exp(-cs)
    e_rem = jnp.exp(tot - cs)
    at = -kk * jnp.exp(cs - lw)
    rt = r * jnp.exp(cs)
    ar = jnp.concatenate([at, rt], axis=0).astype(BF16)

    big_b = _bdot(ar, beta * e_out, _NT)
    big_k = _bdot(ar, k * e_out, _NT)
    x = _bdot(ar, t0)
    yield
    a_ab = jnp.where(strict, big_b[:C], 0.0)
    a_ak = jnp.where(strict, big_k[:C], 0.0)
    a_rb = jnp.where(incl, big_b[C:], 0.0)
    a_rk = jnp.where(incl, big_k[C:], 0.0)
    akv = _bdot(a_ak, v)
    ork = _bdot(a_rk, v)
    kv = _bdot(k * e_rem, v, _TN)
    minv = yield from _tri_inverse(a_ab, eye, row, col)

    u = _bdot(minv, x[:C] + akv)
    yield
    o = x[C:] + ork + _bdot(a_rb, u)

    drow = lax.broadcasted_iota(jnp.int32, (HEAD_DIM, HEAD_DIM), 0)
    dcol = lax.broadcasted_iota(jnp.int32, (HEAD_DIM, HEAD_DIM), 1)
    gcol = jnp.sum(jnp.where(drow == dcol, jnp.exp(tot), 0.0), axis=1, keepdims=True)
    return o, gcol * t0 + kv + _bdot(beta * e_rem, u, _TN)


def _rwkv_kernel(rf_ref, vf_ref, kkf_ref, lwf_ref, kf_ref, af_ref, rb_ref, vb_ref, kkb_ref, lwb_ref, kb_ref, ab_ref,
                 s0_ref, of_ref, ob_ref, st_ref, t_scr, *, chunk, n_chunks):
    c = pl.program_id(1)
    nb = rf_ref.shape[0]

    @pl.when(c == 0)
    def _():
        t_scr[...] = s0_ref[...]

    chains = []
    for b in range(nb):
        fw = [ref[b] for ref in (rf_ref, vf_ref, kkf_ref)] + [ref[0, b] for ref in (lwf_ref, kf_ref, af_ref)]
        bw = [ref[b] for ref in (rb_ref, vb_ref, kkb_ref)] + [ref[0, b] for ref in (lwb_ref, kb_ref, ab_ref)]
        for d, (fwd, tiles) in enumerate(((True, fw), (False, bw))):
            for h in range(N_HEADS):
                sl = slice(h * HEAD_DIM, (h + 1) * HEAD_DIM)
                chains.append(_rwkv_chain(fwd, *[t[:, sl] for t in tiles], t_scr[d, b, h], chunk))
    res = _interleave(chains)
    for b in range(nb):
        for d, o_ref in enumerate((of_ref, ob_ref)):
            first = (b * 2 + d) * N_HEADS
            o_ref[b] = jnp.concatenate([o for o, _ in res[first:first + N_HEADS]], axis=1)
            for h in range(N_HEADS):
                t_scr[d, b, h] = res[first + h][1]

    @pl.when(c == n_chunks - 1)
    def _():
        st_ref[...] = t_scr[...]


def rwkv7_scan(r, v, kk, lw, k, a, s0t, chunk=RWKV_CHUNK, nb=RWKV_REQUESTS_PER_STEP):
    B, L, G = r.shape
    n = L // chunk
    assert L % chunk == 0 and B % nb == 0
    sh_f = pl.BlockSpec((nb, chunk, G), lambda b, c: (b, c, 0))
    sh_b = pl.BlockSpec((nb, chunk, G), lambda b, c: (b, n - 1 - c, 0))
    pd_f = pl.BlockSpec((1, nb, chunk, G), lambda b, c: (0, b, c, 0))
    pd_b = pl.BlockSpec((1, nb, chunk, G), lambda b, c: (1, b, n - 1 - c, 0))
    state = pl.BlockSpec((2, nb, N_HEADS, HEAD_DIM, HEAD_DIM), lambda b, c: (0, b, 0, 0, 0))
    return pl.pallas_call(
        functools.partial(_rwkv_kernel, chunk=chunk, n_chunks=n),
        grid=(B // nb, n),
        in_specs=[sh_f, sh_f, sh_f, pd_f, pd_f, pd_f, sh_b, sh_b, sh_b, pd_b, pd_b, pd_b, state],
        out_specs=[sh_f, sh_b, state],
        out_shape=[jax.ShapeDtypeStruct((B, L, G), F32), jax.ShapeDtypeStruct((B, L, G), F32),
                   jax.ShapeDtypeStruct((2, B, N_HEADS, HEAD_DIM, HEAD_DIM), F32)],
        scratch_shapes=[pltpu.VMEM((2, nb, N_HEADS, HEAD_DIM, HEAD_DIM), F32)],
        compiler_params=pltpu.CompilerParams(
            dimension_semantics=("parallel", "arbitrary"), vmem_limit_bytes=VMEM_LIMIT_BYTES),
        name="rwkv7_scan",
    )(r, v, kk, lw, k, a, r, v, kk, lw, k, a, s0t)


def _ret_chain(fwd, b, h, q, k, v, lg, o_ref, s_scr, C):
    sl = slice(h * HEAD_DIM, (h + 1) * HEAD_DIM)
    row = lax.broadcasted_iota(jnp.int32, (C, C), 0)
    col = lax.broadcasted_iota(jnp.int32, (C, C), 1)
    dist = ((row - col) if fwd else (col - row)).astype(F32)
    decay_in = jnp.where(dist >= 0, jnp.exp(jnp.maximum(dist, 0.0) * lg), 0.0)
    pos = lax.broadcasted_iota(jnp.int32, (C, 1), 0).astype(F32)
    step = pos if fwd else (C - 1.0) - pos
    decay_q = jnp.exp((step + 1.0) * lg)
    decay_k = jnp.exp((C - 1.0 - step) * lg)
    qh, kh, vh = q[:, sl], k[:, sl], v[:, sl].astype(BF16)
    d = 0 if fwd else 1
    s0 = s_scr[b, d, h]
    qk = _bdot(qh, kh, _NT)
    qs = _bdot(qh, s0)
    kv = _bdot(kh * decay_k, vh, _TN)
    yield
    ov = _bdot(qk * decay_in, vh)
    yield
    o_ref[b, :, sl] = ov + qs * decay_q
    s_scr[b, d, h] = s0 * jnp.exp(C * lg) + kv


def _rope(x, cos, sin):
    lane = lax.broadcasted_iota(jnp.int32, x.shape, 1)
    n = x.shape[1]
    swapped = jnp.where((lane % HEAD_DIM) < HEAD_DIM // 2,
                        pltpu.roll(x, n - HEAD_DIM // 2, axis=1), pltpu.roll(x, HEAD_DIM // 2, axis=1))
    return x * cos + swapped * sin


def _ret_kernel(lg_ref, qf_ref, kf_ref, vf_ref, qb_ref, kb_ref, vb_ref, cf_ref, sf_ref, cb_ref, sb_ref, s0_ref,
                of_ref, ob_ref, st_ref, s_scr, *, chunk, n_chunks, rope):
    c = pl.program_id(1)

    @pl.when(c == 0)
    def _():
        s_scr[...] = s0_ref[...]

    scale = HEAD_DIM ** -0.5
    chains = []
    for b in range(qf_ref.shape[0]):
        qf, kf, qb, kb = qf_ref[b], kf_ref[b] * scale, qb_ref[b], kb_ref[b] * scale
        if rope:
            qf, kf = _rope(qf, cf_ref[...], sf_ref[...]), _rope(kf, cf_ref[...], sf_ref[...])
            qb, kb = _rope(qb, cb_ref[...], sb_ref[...]), _rope(kb, cb_ref[...], sb_ref[...])
        vf, vb = vf_ref[b], vb_ref[b]
        chains += [_ret_chain(True, b, h, qf, kf, vf, lg_ref[0, h], of_ref, s_scr, chunk) for h in range(N_HEADS)]
        chains += [_ret_chain(False, b, h, qb, kb, vb, lg_ref[1, h], ob_ref, s_scr, chunk) for h in range(N_HEADS)]
    _interleave(chains)

    @pl.when(c == n_chunks - 1)
    def _():
        st_ref[...] = s_scr[...]


def rope_tables(length):
    t = jnp.arange(length)
    n_freq = HEAD_DIM // 4
    inv = ROPE_BASE ** (-jnp.arange(n_freq, dtype=F32) / n_freq)
    ang = jnp.concatenate([(t // GRID_W).astype(F32)[:, None] * inv,
                           (t % GRID_W).astype(F32)[:, None] * inv], axis=-1)
    cos, sin = jnp.cos(ang), jnp.sin(ang)
    return (jnp.tile(jnp.concatenate([cos, cos], axis=-1), (1, N_HEADS)),
            jnp.tile(jnp.concatenate([-sin, sin], axis=-1), (1, N_HEADS)))


def retention(proj, ret_decay, s0, rope, chunk=RET_CHUNK, nb=RET_REQUESTS_PER_STEP):
    B, L, _ = proj.shape
    G = GROUP_W
    n = L // chunk
    assert L % chunk == 0 and B % nb == 0
    log_g = jax.nn.log_sigmoid(ret_decay)
    cos, sin = rope_tables(L)
    fw = lambda j: pl.BlockSpec((nb, chunk, G), lambda b, c: (b, c, j))
    bw = lambda j: pl.BlockSpec((nb, chunk, G), lambda b, c: (b, n - 1 - c, j))
    tab_f = pl.BlockSpec((chunk, G), lambda b, c: (c, 0))
    tab_b = pl.BlockSpec((chunk, G), lambda b, c: (n - 1 - c, 0))
    state = pl.BlockSpec((nb, 2, N_HEADS, HEAD_DIM, HEAD_DIM), lambda b, c: (b, 0, 0, 0, 0))
    return pl.pallas_call(
        functools.partial(_ret_kernel, chunk=chunk, n_chunks=n, rope=rope),
        grid=(B // nb, n),
        in_specs=[pl.BlockSpec(memory_space=pltpu.SMEM), fw(5), fw(6), fw(7), bw(5), bw(6), bw(7),
                  tab_f, tab_f, tab_b, tab_b, state],
        out_specs=[pl.BlockSpec((nb, chunk, G), lambda b, c: (b, c, 0)),
                   pl.BlockSpec((nb, chunk, G), lambda b, c: (b, n - 1 - c, 0)), state],
        out_shape=[jax.ShapeDtypeStruct((B, L, G), F32), jax.ShapeDtypeStruct((B, L, G), F32),
                   jax.ShapeDtypeStruct((B, 2, N_HEADS, HEAD_DIM, HEAD_DIM), F32)],
        scratch_shapes=[pltpu.VMEM((nb, 2, N_HEADS, HEAD_DIM, HEAD_DIM), F32)],
        compiler_params=pltpu.CompilerParams(
            dimension_semantics=("parallel", "arbitrary"), vmem_limit_bytes=VMEM_LIMIT_BYTES),
        name="retention",
    )(log_g, proj, proj, proj, proj, proj, proj, cos, sin, cos, sin, s0)


def _softmax_pv(s_list, v_list):
    m = s_list[0].max(axis=-1, keepdims=True)
    for s in s_list[1:]:
        m = jnp.maximum(m, s.max(axis=-1, keepdims=True))
    yield
    den = 0.0
    acc = 0.0
    for s, v in zip(s_list, v_list):
        p = jnp.exp(s - m)
        den = den + p.sum(axis=-1, keepdims=True)
        acc = acc + _dot(p.astype(BF16), v, _NN)
    yield
    return acc / den


def _na_kernel(q_ref, k_ref, v_ref, ck_ref, cv_ref, tb_ref, mask_ref, o_ref, *, rows, kh, rows_per_step):
    ck = ck_ref[0, 0].astype(BF16)
    cv = cv_ref[0, 0].astype(BF16)
    mask = mask_ref[...] > 0.0

    def head(j, h):
        r = pl.program_id(1) * rows_per_step + j
        rs = jnp.clip(r - kh // 2, 0, rows - kh)
        start = pl.multiple_of(rs * GRID_W, GRID_W)
        sl = slice(h * HEAD_DIM, (h + 1) * HEAD_DIM)
        kwin = k_ref[0, pl.ds(start, kh * GRID_W), sl].astype(BF16)
        vwin = v_ref[0, pl.ds(start, kh * GRID_W), sl].astype(BF16)
        qh = (q_ref[0, j * GRID_W:(j + 1) * GRID_W, sl] * (HEAD_DIM ** -0.5)).astype(BF16)
        s_loc = _dot(qh, kwin, _NT)
        s_ctx = _dot(qh, ck[:, sl], _NT)
        yield
        s_loc = jnp.where(mask, s_loc + tb_ref[h, r - rs], NEG_INF)
        return (yield from _softmax_pv([s_loc, s_ctx], [vwin, cv[:, sl]]))

    outs = _interleave([head(j, h) for j in range(rows_per_step) for h in range(N_HEADS)])
    for j in range(rows_per_step):
        o_ref[0, j * GRID_W:(j + 1) * GRID_W, :] = jnp.concatenate(outs[j * N_HEADS:(j + 1) * N_HEADS], axis=1)


def na_bias_table(rpb, kh):
    col = np.arange(GRID_W)
    d_col = np.clip(col[None, :] - col[:, None], -(NA_WIN_W - 1), NA_WIN_W - 1) + (NA_WIN_W - 1)
    onehot = jnp.asarray(d_col[:, :, None] == np.arange(2 * NA_WIN_W - 1), F32)
    rows = jnp.stack([rpb[:, NA_WIN_H - 1 - p:NA_WIN_H - 1 - p + kh] for p in range(kh)], axis=1)
    tb = jnp.einsum("hpic,qkc->hpqik", rows, onehot, precision=lax.Precision.HIGHEST)
    col_start = np.clip(col - NA_WIN_W // 2, 0, GRID_W - NA_WIN_W)
    col_in = (col[None, :] >= col_start[:, None]) & (col[None, :] < col_start[:, None] + NA_WIN_W)
    mask = jnp.asarray(np.tile(col_in.astype(np.float32), (1, kh)))
    return tb.reshape(N_HEADS, kh, GRID_W, kh * GRID_W), mask


def neighbourhood_attention(proj, cache_k, cache_v, layer, rpb):
    B, L, _ = proj.shape
    rows = L // GRID_W
    kh = NA_WIN_H
    rps = NA_ROWS_PER_STEP
    assert rows >= kh and rows % rps == 0
    Lc = cache_k.shape[2]
    tb, mask = na_bias_table(rpb, kh)
    G = GROUP_W
    return pl.pallas_call(
        functools.partial(_na_kernel, rows=rows, kh=kh, rows_per_step=rps),
        grid=(B, rows // rps),
        in_specs=[pl.BlockSpec((1, rps * GRID_W, G), lambda b, r: (b, r, 0)),
                  pl.BlockSpec((1, L, G), lambda b, r: (b, 0, 1)),
                  pl.BlockSpec((1, L, G), lambda b, r: (b, 0, 2)),
                  pl.BlockSpec((1, 1, Lc, G), lambda b, r: (b, layer, 0, 0)),
                  pl.BlockSpec((1, 1, Lc, G), lambda b, r: (b, layer, 0, 0)),
                  pl.BlockSpec((N_HEADS, kh, GRID_W, kh * GRID_W), lambda b, r: (0, 0, 0, 0)),
                  pl.BlockSpec((GRID_W, kh * GRID_W), lambda b, r: (0, 0))],
        out_specs=pl.BlockSpec((1, rps * GRID_W, G), lambda b, r: (b, r, 0)),
        out_shape=jax.ShapeDtypeStruct((B, L, G), F32),
        compiler_params=pltpu.CompilerParams(
            dimension_semantics=("parallel", "arbitrary"), vmem_limit_bytes=VMEM_LIMIT_BYTES),
        name="na_attention",
    )(proj, proj, proj, cache_k, cache_v, tb, mask)


def _ctx_attn_kernel(q_ref, k_ref, v_ref, o_ref):
    q = (q_ref[0] * (HEAD_DIM ** -0.5)).astype(BF16)
    k = k_ref[0].astype(BF16)
    v = v_ref[0].astype(BF16)
    def head(h):
        sl = slice(h * HEAD_DIM, (h + 1) * HEAD_DIM)
        s = _dot(q[:, sl], k[:, sl], _NT)
        yield
        return (yield from _softmax_pv([s], [v[:, sl]]))

    o_ref[0] = jnp.concatenate(_interleave([head(h) for h in range(N_HEADS)]), axis=1)


def context_attention(proj):
    B, L, _ = proj.shape
    G = GROUP_W
    return pl.pallas_call(
        _ctx_attn_kernel,
        grid=(B,),
        in_specs=[pl.BlockSpec((1, L, G), lambda b: (b, 0, 0)),
                  pl.BlockSpec((1, L, G), lambda b: (b, 0, 1)),
                  pl.BlockSpec((1, L, G), lambda b: (b, 0, 2))],
        out_specs=pl.BlockSpec((1, L, G), lambda b: (b, 0, 0)),
        out_shape=jax.ShapeDtypeStruct((B, L, G), F32),
        compiler_params=pltpu.CompilerParams(dimension_semantics=("parallel",), vmem_limit_bytes=VMEM_LIMIT_BYTES),
        name="ctx_attention",
    )(proj, proj, proj)


def rms_norm(x, g):
    return x * lax.rsqrt(jnp.mean(x * x, axis=-1, keepdims=True) + EPS) * g


def layer_norm(x, g, b):
    xc = x - jnp.mean(x, axis=-1, keepdims=True)
    return xc * lax.rsqrt(jnp.mean(xc * xc, axis=-1, keepdims=True) + EPS) * g + b


def head_norm(x, g, eps):
    xc = x - jnp.mean(x, axis=-1, keepdims=True)
    y = xc * lax.rsqrt(jnp.mean(xc * xc, axis=-1, keepdims=True) + eps)
    return y.reshape(x.shape[0], x.shape[1], -1) * g


def dwconv(x, w):
    k, ch = w.shape
    return lax.conv_general_dilated(x, w[:, None, :], window_strides=(1,), padding=[(k // 2, k // 2)],
                                    dimension_numbers=("NWC", "WIO", "NWC"), feature_group_count=ch)


def token_mixers(x, mod, rows_per_mod, p, cache):
    B, L, _ = x.shape
    latent = cache is not None
    x2d = x.reshape(B * L, D_MODEL)
    proj2d = norm_matmul(x2d, mod, rows_per_mod, p["norm_g"], p["w_in"])
    proj = proj2d.reshape(B, L, -1)
    a_k = proj[..., GROUP_W:2 * GROUP_W].reshape(B, L, N_HEADS, HEAD_DIM)
    a_v = proj[..., 2 * GROUP_W:3 * GROUP_W].reshape(B, L, N_HEADS, HEAD_DIM)
    if latent:
        o_a = neighbourhood_attention(proj, cache["na_k"], cache["na_v"], cache["layer"], p["na_rpb"])
    else:
        o_a = context_attention(proj)

    o_b = conv_module(proj2d, L, p["conv_dw"], p["conv_ln_g"], p["conv_ln_b"])

    ret0 = cache["ret"] if latent else jnp.zeros((B, 2, N_HEADS, HEAD_DIM, HEAD_DIM), F32)
    o_cf, o_cb, ret_state = retention(proj, p["ret_decay"], ret0, rope=latent)

    r, v, kk, lw2, k2, a2, bonus, gate = rwkv_prep(proj2d, L, p)
    rwkv0 = cache["rwkv"] if latent else jnp.zeros((B, 2, N_HEADS, HEAD_DIM, HEAD_DIM), F32)
    s0t = jnp.swapaxes(rwkv0, -1, -2).transpose(1, 0, 2, 3, 4)
    tok = lambda t: t.reshape(t.shape[:-2] + (B, L, GROUP_W))
    o_df, o_db, st = rwkv7_scan(tok(r), tok(v), tok(kk), tok(lw2), tok(k2), tok(a2), s0t)
    rwkv_state = jnp.swapaxes(st, -1, -2).transpose(1, 0, 2, 3, 4)

    flat = lambda t: t.reshape(B * L, GROUP_W)
    x2d = mix_out(flat(o_a), o_b, flat(o_cf), flat(o_cb), proj2d, flat(o_df), flat(o_db), bonus, gate, x2d, mod,
                  rows_per_mod, p["norm_g"], p["ret_gn"], p["rwkv_gn"], p["w_out"])
    return x2d, (None if latent else (a_k, a_v, ret_state, rwkv_state))


def trunk_layer(x, mod_vec, p, cache):
    B, L, _ = x.shape
    n_mod = mod_vec.shape[0]
    rows_per_mod = B * L // n_mod
    mod = jnp.pad(mod_vec.reshape(n_mod, 6, D_MODEL), ((0, 0), (0, MOD_ROWS - 6), (0, 0)))
    x2d, state = token_mixers(x, mod, rows_per_mod, p, cache)
    x2d = ffn_block(x2d, mod, rows_per_mod, L, p["norm_g"], p["ffn_up"], p["ffn_conv"], p["ffn_down"])
    return x2d.reshape(B, L, D_MODEL), state


def kernel(x_prompt, x_sample, cache_na_k, cache_na_v, state_retention, state_rwkv, c, c_ctx, ada_w, ada_b, norm_g, w_in, w_out, na_rpb, conv_dw, conv_ln_g, conv_ln_b, ret_decay, ret_gn, rwkv_shift, rwkv_w0, rwkv_w2, rwkv_a0, rwkv_a2, rwkv_g2, rwkv_kk, rwkv_ka, rwkv_rk, rwkv_gn, ffn_up, ffn_conv, ffn_down):
    y_prompt, y_sample = x_prompt, x_sample
    new_k, new_v, new_ret, new_rwkv = [], [], [], []
    for l in range(DEPTH):
        p = {"norm_g": norm_g[l], "w_in": w_in[l].astype(BF16), "w_out": w_out[l].astype(BF16), "na_rpb": na_rpb[l],
             "conv_dw": conv_dw[l], "conv_ln_g": conv_ln_g[l], "conv_ln_b": conv_ln_b[l],
             "ret_decay": ret_decay[l], "ret_gn": ret_gn[l], "rwkv_shift": rwkv_shift[l],
             "rwkv_w0": rwkv_w0[l], "rwkv_w2": rwkv_w2[l], "rwkv_a0": rwkv_a0[l], "rwkv_a2": rwkv_a2[l],
             "rwkv_g2": rwkv_g2[l], "rwkv_kk": rwkv_kk[l], "rwkv_ka": rwkv_ka[l], "rwkv_rk": rwkv_rk[l],
             "rwkv_gn": rwkv_gn[l], "ffn_up": ffn_up[l].astype(BF16), "ffn_conv": ffn_conv[l],
             "ffn_down": ffn_down[l].astype(BF16)}
        mod_ctx = (jax.nn.silu(c_ctx) @ ada_w[l] + ada_b[l])[None]
        y_prompt, (k_l, v_l, ret_l, rwkv_l) = trunk_layer(y_prompt, mod_ctx, p, None)
        new_k.append(k_l)
        new_v.append(v_l)
        new_ret.append(ret_l)
        new_rwkv.append(rwkv_l)
        mod_lat = jax.nn.silu(c) @ ada_w[l] + ada_b[l]
        cache_l = {"na_k": cache_na_k.reshape(cache_na_k.shape[:3] + (GROUP_W,)),
                   "na_v": cache_na_v.reshape(cache_na_v.shape[:3] + (GROUP_W,)), "layer": l,
                   "ret": state_retention[:, l], "rwkv": state_rwkv[:, l]}
        y_sample, _ = trunk_layer(y_sample, mod_lat, p, cache_l)
    return (y_prompt, y_sample, jnp.stack(new_k, axis=1), jnp.stack(new_v, axis=1),
            jnp.stack(new_ret, axis=1), jnp.stack(new_rwkv, axis=1))
```

```python
import functools

import jax
import jax.numpy as jnp
import numpy as np
from jax import lax
from jax.experimental import pallas as pl
from jax.experimental.pallas import tpu as pltpu

F32 = jnp.float32
BF16 = jnp.bfloat16

D_MODEL = 1024
DEPTH = 2
GRID_W = 64
N_MIXERS = 4
GROUP_W = D_MODEL // N_MIXERS
HEAD_DIM = 64
N_HEADS = GROUP_W // HEAD_DIM
NA_WIN_H = 8
NA_WIN_W = 16
Q_BLOCK = 128
CONV_W = 31
RET_CHUNK = 128
RWKV_LORA_W = 64
RWKV_LORA_A = 64
RWKV_LORA_G = 128
D_FF = 2816
ROPE_BASE = 10000.0
EPS = 1e-6
RWKV_GN_EPS = 64e-5
NEG_INF = -1e30
SPLIT_SIZES = (GROUP_W,) * 12 + (RWKV_LORA_W, RWKV_LORA_A, RWKV_LORA_G)
SPLIT_POINTS = tuple(int(s) for s in np.cumsum(SPLIT_SIZES)[:-1])

RWKV_CHUNK = 64
RET_REQUESTS_PER_STEP = 2
RWKV_REQUESTS_PER_STEP = 4
VMEM_LIMIT_BYTES = 48 * 1024 * 1024
SUBLANES = 8
MOD_ROWS = 8
NA_ROWS_PER_STEP = 8
ROW_TILE = 512
FFN_CHUNK = 256


def _mm_kernel(a_ref, b_ref, o_ref):
    o_ref[...] = jnp.dot(a_ref[...].astype(BF16), b_ref[...], preferred_element_type=F32)


def _pick_tile(n, target):
    best = 128
    for t in range(128, target + 1, 128):
        if n % t == 0:
            best = t
    return best


def matmul(a, b_bf16, tm=512, tn_target=1792):
    m, k = a.shape
    n = b_bf16.shape[1]
    tn = _pick_tile(n, tn_target)
    tm = min(tm, m)
    assert m % tm == 0 and n % tn == 0
    return pl.pallas_call(
        _mm_kernel,
        grid=(n // tn, m // tm),
        in_specs=[pl.BlockSpec((tm, k), lambda j, i: (i, 0)),
                  pl.BlockSpec((k, tn), lambda j, i: (0, j))],
        out_specs=pl.BlockSpec((tm, tn), lambda j, i: (i, j)),
        out_shape=jax.ShapeDtypeStruct((m, n), F32),
        compiler_params=pltpu.CompilerParams(
            dimension_semantics=("parallel", "parallel"), vmem_limit_bytes=VMEM_LIMIT_BYTES),
        name="matmul",
    )(a, b_bf16)


def _rms(x, g):
    return x * lax.rsqrt(jnp.mean(x * x, axis=-1, keepdims=True) + EPS) * g


def _resident(shape):
    return pl.BlockSpec(shape, lambda *_: (0,) * len(shape), pipeline_mode=pl.Buffered(1))


def _mod_spec(tm, rows_per_mod):
    return pl.BlockSpec((1, MOD_ROWS, D_MODEL), lambda i: ((i * tm) // rows_per_mod, 0, 0))


def _norm_mm_kernel(x_ref, mod_ref, g_ref, w_ref, o_ref):
    mod = mod_ref[0]
    h = _rms(x_ref[...], g_ref[0:1]) * (1.0 + mod[1:2]) + mod[0:1]
    o_ref[...] = jnp.dot(h.astype(BF16), w_ref[...], preferred_element_type=F32)


def norm_matmul(x, mod, rows_per_mod, norm_g, w_bf16, tm=ROW_TILE):
    m, n = x.shape[0], w_bf16.shape[1]
    tm = min(tm, rows_per_mod)
    return pl.pallas_call(
        _norm_mm_kernel,
        grid=(m // tm,),
        in_specs=[pl.BlockSpec((tm, D_MODEL), lambda i: (i, 0)), _mod_spec(tm, rows_per_mod),
                  _resident(norm_g.shape), _resident(w_bf16.shape)],
        out_specs=pl.BlockSpec((tm, n), lambda i: (i, 0)),
        out_shape=jax.ShapeDtypeStruct((m, n), F32),
        compiler_params=pltpu.CompilerParams(dimension_semantics=("parallel",), vmem_limit_bytes=VMEM_LIMIT_BYTES),
        name="norm_matmul",
    )(x, mod, norm_g, w_bf16)


def _head_sum(y):
    g = y.shape[1]
    row = lax.broadcasted_iota(jnp.int32, (g, g), 0) // HEAD_DIM
    col = lax.broadcasted_iota(jnp.int32, (g, g), 1) // HEAD_DIM
    ones = jnp.where(row == col, 1.0, 0.0).astype(BF16)
    hi = y.astype(BF16)
    lo = (y - hi.astype(F32)).astype(BF16)
    return jnp.dot(hi, ones, preferred_element_type=F32) + jnp.dot(lo, ones, preferred_element_type=F32)


def _head_norm(y, g, eps):
    yc = y - _head_sum(y) * (1.0 / HEAD_DIM)
    return yc * lax.rsqrt(_head_sum(yc * yc) * (1.0 / HEAD_DIM) + eps) * g


def _mix_out_kernel(oa_ref, ob_ref, cf_ref, cb_ref, rg_ref, df_ref, db_ref, bonus_ref, gate_ref, x_ref, mod_ref,
                    g_ref, gn_ref, w_ref, o_ref):
    G = GROUP_W
    rg = rg_ref[...]
    o_c = _head_norm(cf_ref[...] + cb_ref[...], gn_ref[0:1], EPS) * (rg * jax.nn.sigmoid(rg))
    o_d = (_head_norm(df_ref[...] + db_ref[...], gn_ref[1:2], RWKV_GN_EPS) + bonus_ref[...]) * gate_ref[...]
    m = 0.0
    for j, o in enumerate((oa_ref[...], ob_ref[...], o_c, o_d)):
        m = m + jnp.dot(o.astype(BF16), w_ref[j * G:(j + 1) * G, :], preferred_element_type=F32)
    o_ref[...] = x_ref[...] + mod_ref[0][2:3] * _rms(m, g_ref[1:2])


def mix_out(o_a, o_b, o_cf, o_cb, proj2d, o_df, o_db, bonus, gate, x, mod, rows_per_mod, norm_g, ret_gn, rwkv_gn,
            w_bf16, tm=ROW_TILE):
    m = x.shape[0]
    tm = min(tm, rows_per_mod)
    G = GROUP_W
    grp = pl.BlockSpec((tm, G), lambda i: (i, 0))
    gn = jnp.stack([ret_gn, rwkv_gn])
    return pl.pallas_call(
        _mix_out_kernel,
        grid=(m // tm,),
        in_specs=[grp, grp, grp, grp, pl.BlockSpec((tm, G), lambda i: (i, 8)), grp, grp, grp, grp,
                  pl.BlockSpec((tm, D_MODEL), lambda i: (i, 0)), _mod_spec(tm, rows_per_mod),
                  _resident(norm_g.shape), _resident(gn.shape), _resident(w_bf16.shape)],
        out_specs=pl.BlockSpec((tm, D_MODEL), lambda i: (i, 0)),
        out_shape=jax.ShapeDtypeStruct((m, D_MODEL), F32),
        compiler_params=pltpu.CompilerParams(dimension_semantics=("parallel",), vmem_limit_bytes=VMEM_LIMIT_BYTES),
        name="mix_out",
    )(o_a, o_b, o_cf, o_cb, proj2d, o_df, o_db, bonus, gate, x, mod, norm_g, gn, w_bf16)


def _rwkv_prep_kernel(rkv_ref, rkvp_ref, rkvn_ref, lora_ref, sh_ref, w0_ref, w2_ref, a0_ref, a2_ref, g2_ref, vec_ref,
                      r_ref, v_ref, kk_ref, lw_ref, k_ref, a_ref, bonus_ref, gate_ref, *, seq_tiles):
    i = pl.program_id(0)
    G = GROUP_W
    keep_prev = jnp.where(i % seq_tiles == 0, 0.0, 1.0)
    keep_next = jnp.where(i % seq_tiles == seq_tiles - 1, 0.0, 1.0)
    x = rkv_ref[...]
    x_prev, x_next = _shift_rows(x, rkvp_ref[SUBLANES - 1:SUBLANES] * keep_prev, rkvn_ref[0:1] * keep_next)
    rkv = sh_ref[0:1] * x_prev + sh_ref[1:2] * x + sh_ref[2:3] * x_next
    d_r, d_k, d_v = rkv[:, :G], rkv[:, G:2 * G], rkv[:, 2 * G:]
    lora = lora_ref[...]
    w_low = jnp.tanh(lora[:, :RWKV_LORA_W]).astype(BF16)
    a_low = lora[:, RWKV_LORA_W:RWKV_LORA_W + RWKV_LORA_A].astype(BF16)
    g_low = jax.nn.sigmoid(lora[:, RWKV_LORA_W + RWKV_LORA_A:]).astype(BF16)
    kk_w, ka, rk = vec_ref[0:1], vec_ref[1:2], vec_ref[2:3]
    k_sum = 0.0
    for d in range(2):
        z = w0_ref[d:d + 1] + jnp.dot(w_low, w2_ref[d].astype(BF16), preferred_element_type=F32)
        lw_ref[d] = -float(np.exp(-0.5)) * jax.nn.sigmoid(z)
        a = jax.nn.sigmoid(a0_ref[d:d + 1] + jnp.dot(a_low, a2_ref[d].astype(BF16), preferred_element_type=F32))
        k_d = d_k * (1.0 + (a - 1.0) * ka)
        a_ref[d] = a
        k_ref[d] = k_d
        k_sum = k_sum + k_d
    kk = d_k * kk_w
    r_ref[...] = d_r
    v_ref[...] = d_v
    kk_ref[...] = kk * lax.rsqrt(_head_sum(kk * kk) + 1e-12)
    bonus_ref[...] = _head_sum(d_r * k_sum * rk) * d_v
    gate_ref[...] = jnp.dot(g_low, g2_ref[...].astype(BF16), preferred_element_type=F32)


def rwkv_prep(proj2d, seq_len, p, tm=ROW_TILE):
    m = proj2d.shape[0]
    tm = min(tm, seq_len)
    G = GROUP_W
    assert seq_len % tm == 0
    prev_spec, next_spec = _halo_specs(tm, m, 3 * G, col=3)
    vec = jnp.stack([p["rwkv_kk"], p["rwkv_ka"], p["rwkv_rk"]])
    one = pl.BlockSpec((tm, G), lambda i: (i, 0))
    two = pl.BlockSpec((2, tm, G), lambda i: (0, i, 0))
    sds1 = jax.ShapeDtypeStruct((m, G), F32)
    sds2 = jax.ShapeDtypeStruct((2, m, G), F32)
    params = (p["rwkv_shift"], p["rwkv_w0"], p["rwkv_w2"], p["rwkv_a0"], p["rwkv_a2"], p["rwkv_g2"], vec)
    return pl.pallas_call(
        functools.partial(_rwkv_prep_kernel, seq_tiles=seq_len // tm),
        grid=(m // tm,),
        in_specs=[pl.BlockSpec((tm, 3 * G), lambda i: (i, 3)), prev_spec, next_spec,
                  pl.BlockSpec((tm, G), lambda i: (i, 12))] + [_resident(t.shape) for t in params],
        out_specs=[one, one, one, two, two, two, one, one],
        out_shape=[sds1, sds1, sds1, sds2, sds2, sds2, sds1, sds1],
        compiler_params=pltpu.CompilerParams(dimension_semantics=("parallel",), vmem_limit_bytes=VMEM_LIMIT_BYTES),
        name="rwkv_prep",
    )(proj2d, proj2d, proj2d, proj2d, *params)


def _halo_specs(tm, m, width, col=0):
    blocks = tm // SUBLANES
    return (pl.BlockSpec((SUBLANES, width), lambda i: (jnp.maximum(i * blocks - 1, 0), col)),
            pl.BlockSpec((SUBLANES, width), lambda i: (jnp.minimum((i + 1) * blocks, m // SUBLANES - 1), col)))


def _shift_rows(u, prev_row, next_row):
    tm = u.shape[0]
    row = lax.broadcasted_iota(jnp.int32, (tm, 1), 0)
    u_prev = jnp.where(row == 0, prev_row, pltpu.roll(u, 1, axis=0))
    u_next = jnp.where(row == tm - 1, next_row, pltpu.roll(u, tm - 1, axis=0))
    return u_prev, u_next


def _ffn_kernel(x_ref, xp_ref, xn_ref, mod_ref, g_ref, up_ref, cw_ref, down_ref, o_ref, act_scr, *, seq_tiles, cw):
    i = pl.program_id(0)
    keep_prev = jnp.where(i % seq_tiles == 0, 0.0, 1.0)
    keep_next = jnp.where(i % seq_tiles == seq_tiles - 1, 0.0, 1.0)
    mod = mod_ref[0]
    g2, g3 = g_ref[2:3], g_ref[3:4]

    def pre(x):
        return (_rms(x, g2) * (1.0 + mod[4:5]) + mod[3:4]).astype(BF16)

    x = x_ref[...]
    h = pre(x)
    hh = pre(jnp.concatenate([xp_ref[...], xn_ref[...]], axis=0))
    def cols(j, half):
        return slice(half * D_FF + j * cw, half * D_FF + (j + 1) * cw)

    def up(j):
        return [(jnp.dot(h, up_ref[:, cols(j, half)], preferred_element_type=F32),
                 jnp.dot(hh, up_ref[:, cols(j, half)], preferred_element_type=F32)) for half in range(2)]

    def conv_act(j, ups):
        conv = []
        for half, (u, uh) in enumerate(ups):
            u_prev, u_next = _shift_rows(u, uh[SUBLANES - 1:SUBLANES] * keep_prev, uh[SUBLANES:SUBLANES + 1] * keep_next)
            wc = cw_ref[:, cols(j, half)]
            conv.append(wc[0:1] * u_prev + wc[1:2] * u + wc[2:3] * u_next)
        return (conv[0] * jax.nn.sigmoid(conv[0]) * conv[1]).astype(BF16)

    n = D_FF // cw
    ahead = 2
    ups = [up(j) for j in range(min(ahead, n))]
    for j in range(n):
        if j + ahead < n:
            ups.append(up(j + ahead))
        act_scr[:, j * cw:(j + 1) * cw] = conv_act(j, ups[j])
    f = jnp.dot(act_scr[...], down_ref[...], preferred_element_type=F32)
    o_ref[...] = x + mod[5:6] * _rms(f, g3)


def ffn_block(x, mod, rows_per_mod, seq_len, norm_g, up_bf16, w_conv, down_bf16, tm=ROW_TILE, cw=FFN_CHUNK):
    m = x.shape[0]
    tm = min(tm, seq_len)
    assert seq_len % tm == 0 and D_FF % cw == 0
    prev_spec, next_spec = _halo_specs(tm, m, D_MODEL)
    return pl.pallas_call(
        functools.partial(_ffn_kernel, seq_tiles=seq_len // tm, cw=cw),
        grid=(m // tm,),
        in_specs=[pl.BlockSpec((tm, D_MODEL), lambda i: (i, 0)), prev_spec, next_spec, _mod_spec(tm, rows_per_mod),
                  _resident(norm_g.shape), _resident(up_bf16.shape), _resident(w_conv.shape),
                  _resident(down_bf16.shape)],
        out_specs=pl.BlockSpec((tm, D_MODEL), lambda i: (i, 0)),
        out_shape=jax.ShapeDtypeStruct((m, D_MODEL), F32),
        scratch_shapes=[pltpu.VMEM((tm, D_FF), BF16)],
        compiler_params=pltpu.CompilerParams(dimension_semantics=("parallel",), vmem_limit_bytes=VMEM_LIMIT_BYTES),
        name="ffn_block",
    )(x, x, x, mod, norm_g, up_bf16, w_conv, down_bf16)


def _conv_module_kernel(a_ref, b_ref, ap_ref, bp_ref, an_ref, bn_ref, w_ref, ln_ref, o_ref, pad_scr, sh_scr, *,
                        seq_tiles):
    i = pl.program_id(0)
    tm = a_ref.shape[0]
    halo = 2 * SUBLANES
    keep_prev = jnp.where(i % seq_tiles == 0, 0.0, 1.0)
    keep_next = jnp.where(i % seq_tiles == seq_tiles - 1, 0.0, 1.0)

    def glu(a, b):
        return a * jax.nn.sigmoid(b)

    pad_scr[0:halo] = glu(ap_ref[...], bp_ref[...]) * keep_prev
    pad_scr[halo:halo + tm] = glu(a_ref[...], b_ref[...])
    pad_scr[halo + tm:2 * halo + tm] = glu(an_ref[...], bn_ref[...]) * keep_next
    offs = [halo - CONV_W // 2 + j for j in range(CONV_W)]
    span = tm + (max(offs) // SUBLANES) * SUBLANES
    for b in range(SUBLANES):
        sh_scr[b] = pad_scr[b:b + span]
    acc = jnp.zeros((tm, GROUP_W), F32)
    for j, off in enumerate(offs):
        a, b = divmod(off, SUBLANES)
        acc = acc + w_ref[j:j + 1] * sh_scr[b, a * SUBLANES:a * SUBLANES + tm]
    xc = acc - jnp.mean(acc, axis=-1, keepdims=True)
    y = xc * lax.rsqrt(jnp.mean(xc * xc, axis=-1, keepdims=True) + EPS) * ln_ref[0:1] + ln_ref[1:2]
    o_ref[...] = y * jax.nn.sigmoid(y)


def conv_module(proj2d, seq_len, w_dw, ln_g, ln_b, tm=ROW_TILE):
    m = proj2d.shape[0]
    tm = min(tm, seq_len)
    G = GROUP_W
    halo = 2 * SUBLANES
    assert CONV_W // 2 <= halo and seq_len % tm == 0
    blocks = tm // halo

    def prev(col):
        return pl.BlockSpec((halo, G), lambda i: (jnp.maximum(i * blocks - 1, 0), col))

    def nxt(col):
        return pl.BlockSpec((halo, G), lambda i: (jnp.minimum((i + 1) * blocks, m // halo - 1), col))

    ln = jnp.stack([ln_g, ln_b])
    return pl.pallas_call(
        functools.partial(_conv_module_kernel, seq_tiles=seq_len // tm),
        grid=(m // tm,),
        in_specs=[pl.BlockSpec((tm, G), lambda i: (i, 3)), pl.BlockSpec((tm, G), lambda i: (i, 4)),
                  prev(3), prev(4), nxt(3), nxt(4), _resident(w_dw.shape), _resident(ln.shape)],
        out_specs=pl.BlockSpec((tm, G), lambda i: (i, 0)),
        out_shape=jax.ShapeDtypeStruct((m, G), F32),
        scratch_shapes=[pltpu.VMEM((tm + 2 * halo, G), F32),
                        pltpu.VMEM((SUBLANES, tm + ((halo + CONV_W // 2) // SUBLANES) * SUBLANES, G), F32)],
        compiler_params=pltpu.CompilerParams(dimension_semantics=("parallel",), vmem_limit_bytes=VMEM_LIMIT_BYTES),
        name="conv_module",
    )(proj2d, proj2d, proj2d, proj2d, proj2d, proj2d, w_dw, ln)


def _dot(a, b, dims):
    return lax.dot_general(a, b, (dims, ((), ())), preferred_element_type=F32)


_NN = ((1,), (0,))
_NT = ((1,), (1,))
_TN = ((0,), (0,))


def _bdot(a, b, dims=_NN):
    return _dot(a.astype(BF16), b.astype(BF16), dims)


def _tri_inverse(n_mat, eye, row, col):
    C = n_mat.shape[0]
    nd = jnp.where((row // 8) == (col // 8), n_mat, 0.0)
    s1 = eye + nd
    p1 = _bdot(nd, nd)
    yield
    s2 = _bdot(p1, s1)
    p2 = _bdot(p1, p1)
    yield
    s2 = s1 + s2
    d = _bdot(p2, s2)
    yield
    d = s2 + d
    size = 8
    while size < C:
        inner = (row // size) == (col // size)
        outer = (row // (2 * size)) == (col // (2 * size))
        n_off = jnp.where(outer & jnp.logical_not(inner), n_mat, 0.0)
        t = _bdot(n_off, d)
        yield
        t = _bdot(d, t)
        yield
        d = d + t
        size *= 2
    return d


def _interleave(gens):
    results = [None] * len(gens)
    active = list(enumerate(gens))
    while active:
        still = []
        for i, g in active:
            try:
                next(g)
                still.append((i, g))
            except StopIteration as stop:
                results[i] = stop.value
        active = still
    return results


def _rwkv_chain(fwd, r, v, kk, lw, k, a, t0, C):
    v = v.astype(BF16)

    row = lax.broadcasted_iota(jnp.int32, (C, C), 0)
    col = lax.broadcasted_iota(jnp.int32, (C, C), 1)
    incl = (row >= col) if fwd else (row <= col)
    strict = (row > col) if fwd else (row < col)
    incl_bf = jnp.where(incl, 1.0, 0.0).astype(BF16)
    eye = jnp.where(row == col, 1.0, 0.0)

    l_hi = lw.astype(BF16)
    l_lo = (lw - l_hi.astype(F32)).astype(BF16)
    cs = _dot(incl_bf, l_hi, _NN) + _dot(incl_bf, l_lo, _NN)
    yield
    tot = jnp.sum(lw, axis=0, keepdims=True)

    beta = kk * a
    e_out = jnp.exp(-cs)
    e_rem = jnp.exp(tot - cs)
    at = -kk * jnp.exp(cs - lw)
    rt = r * jnp.exp(cs)
    ar = jnp.concatenate([at, rt], axis=0).astype(BF16)

    big_b = _bdot(ar, beta * e_out, _NT)
    big_k = _bdot(ar, k * e_out, _NT)
    x = _bdot(ar, t0)
    yield
    a_ab = jnp.where(strict, big_b[:C], 0.0)
    a_ak = jnp.where(strict, big_k[:C], 0.0)
    a_rb = jnp.where(incl, big_b[C:], 0.0)
    a_rk = jnp.where(incl, big_k[C:], 0.0)
    akv = _bdot(a_ak, v)
    ork = _bdot(a_rk, v)
    kv = _bdot(k * e_rem, v, _TN)
    minv = yield from _tri_inverse(a_ab, eye, row, col)

    u = _bdot(minv, x[:C] + akv)
    yield
    o = x[C:] + ork + _bdot(a_rb, u)

    drow = lax.broadcasted_iota(jnp.int32, (HEAD_DIM, HEAD_DIM), 0)
    dcol = lax.broadcasted_iota(jnp.int32, (HEAD_DIM, HEAD_DIM), 1)
    gcol = jnp.sum(jnp.where(drow == dcol, jnp.exp(tot), 0.0), axis=1, keepdims=True)
    return o, gcol * t0 + kv + _bdot(beta * e_rem, u, _TN)


def _rwkv_kernel(rf_ref, vf_ref, kkf_ref, lwf_ref, kf_ref, af_ref, rb_ref, vb_ref, kkb_ref, lwb_ref, kb_ref, ab_ref,
                 s0_ref, of_ref, ob_ref, st_ref, t_scr, *, chunk, n_chunks):
    c = pl.program_id(1)
    nb = rf_ref.shape[0]

    @pl.when(c == 0)
    def _():
        t_scr[...] = s0_ref[...]

    chains = []
    for b in range(nb):
        fw = [ref[b] for ref in (rf_ref, vf_ref, kkf_ref)] + [ref[0, b] for ref in (lwf_ref, kf_ref, af_ref)]
        bw = [ref[b] for ref in (rb_ref, vb_ref, kkb_ref)] + [ref[0, b] for ref in (lwb_ref, kb_ref, ab_ref)]
        for d, (fwd, tiles) in enumerate(((True, fw), (False, bw))):
            for h in range(N_HEADS):
                sl = slice(h * HEAD_DIM, (h + 1) * HEAD_DIM)
                chains.append(_rwkv_chain(fwd, *[t[:, sl] for t in tiles], t_scr[d, b, h], chunk))
    res = _interleave(chains)
    for b in range(nb):
        for d, o_ref in enumerate((of_ref, ob_ref)):
            first = (b * 2 + d) * N_HEADS
            o_ref[b] = jnp.concatenate([o for o, _ in res[first:first + N_HEADS]], axis=1)
            for h in range(N_HEADS):
                t_scr[d, b, h] = res[first + h][1]

    @pl.when(c == n_chunks - 1)
    def _():
        st_ref[...] = t_scr[...]


def rwkv7_scan(r, v, kk, lw, k, a, s0t, chunk=RWKV_CHUNK, nb=RWKV_REQUESTS_PER_STEP):
    B, L, G = r.shape
    n = L // chunk
    assert L % chunk == 0 and B % nb == 0
    sh_f = pl.BlockSpec((nb, chunk, G), lambda b, c: (b, c, 0))
    sh_b = pl.BlockSpec((nb, chunk, G), lambda b, c: (b, n - 1 - c, 0))
    pd_f = pl.BlockSpec((1, nb, chunk, G), lambda b, c: (0, b, c, 0))
    pd_b = pl.BlockSpec((1, nb, chunk, G), lambda b, c: (1, b, n - 1 - c, 0))
    state = pl.BlockSpec((2, nb, N_HEADS, HEAD_DIM, HEAD_DIM), lambda b, c: (0, b, 0, 0, 0))
    return pl.pallas_call(
        functools.partial(_rwkv_kernel, chunk=chunk, n_chunks=n),
        grid=(B // nb, n),
        in_specs=[sh_f, sh_f, sh_f, pd_f, pd_f, pd_f, sh_b, sh_b, sh_b, pd_b, pd_b, pd_b, state],
        out_specs=[sh_f, sh_b, state],
        out_shape=[jax.ShapeDtypeStruct((B, L, G), F32), jax.ShapeDtypeStruct((B, L, G), F32),
                   jax.ShapeDtypeStruct((2, B, N_HEADS, HEAD_DIM, HEAD_DIM), F32)],
        scratch_shapes=[pltpu.VMEM((2, nb, N_HEADS, HEAD_DIM, HEAD_DIM), F32)],
        compiler_params=pltpu.CompilerParams(
            dimension_semantics=("parallel", "arbitrary"), vmem_limit_bytes=VMEM_LIMIT_BYTES),
        name="rwkv7_scan",
    )(r, v, kk, lw, k, a, r, v, kk, lw, k, a, s0t)


def _ret_chain(fwd, b, h, q, k, v, lg, o_ref, s_scr, C):
    sl = slice(h * HEAD_DIM, (h + 1) * HEAD_DIM)
    row = lax.broadcasted_iota(jnp.int32, (C, C), 0)
    col = lax.broadcasted_iota(jnp.int32, (C, C), 1)
    dist = ((row - col) if fwd else (col - row)).astype(F32)
    decay_in = jnp.where(dist >= 0, jnp.exp(jnp.maximum(dist, 0.0) * lg), 0.0)
    pos = lax.broadcasted_iota(jnp.int32, (C, 1), 0).astype(F32)
    step = pos if fwd else (C - 1.0) - pos
    decay_q = jnp.exp((step + 1.0) * lg)
    decay_k = jnp.exp((C - 1.0 - step) * lg)
    qh, kh, vh = q[:, sl], k[:, sl], v[:, sl].astype(BF16)
    d = 0 if fwd else 1
    s0 = s_scr[b, d, h]
    qk = _bdot(qh, kh, _NT)
    qs = _bdot(qh, s0)
    kv = _bdot(kh * decay_k, vh, _TN)
    yield
    ov = _bdot(qk * decay_in, vh)
    yield
    o_ref[b, :, sl] = ov + qs * decay_q
    s_scr[b, d, h] = s0 * jnp.exp(C * lg) + kv


def _rope(x, cos, sin):
    lane = lax.broadcasted_iota(jnp.int32, x.shape, 1)
    n = x.shape[1]
    swapped = jnp.where((lane % HEAD_DIM) < HEAD_DIM // 2,
                        pltpu.roll(x, n - HEAD_DIM // 2, axis=1), pltpu.roll(x, HEAD_DIM // 2, axis=1))
    return x * cos + swapped * sin


def _ret_kernel(lg_ref, qf_ref, kf_ref, vf_ref, qb_ref, kb_ref, vb_ref, cf_ref, sf_ref, cb_ref, sb_ref, s0_ref,
                of_ref, ob_ref, st_ref, s_scr, *, chunk, n_chunks, rope):
    c = pl.program_id(1)

    @pl.when(c == 0)
    def _():
        s_scr[...] = s0_ref[...]

    scale = HEAD_DIM ** -0.5
    chains = []
    for b in range(qf_ref.shape[0]):
        qf, kf, qb, kb = qf_ref[b], kf_ref[b] * scale, qb_ref[b], kb_ref[b] * scale
        if rope:
            qf, kf = _rope(qf, cf_ref[...], sf_ref[...]), _rope(kf, cf_ref[...], sf_ref[...])
            qb, kb = _rope(qb, cb_ref[...], sb_ref[...]), _rope(kb, cb_ref[...], sb_ref[...])
        vf, vb = vf_ref[b], vb_ref[b]
        chains += [_ret_chain(True, b, h, qf, kf, vf, lg_ref[0, h], of_ref, s_scr, chunk) for h in range(N_HEADS)]
        chains += [_ret_chain(False, b, h, qb, kb, vb, lg_ref[1, h], ob_ref, s_scr, chunk) for h in range(N_HEADS)]
    _interleave(chains)

    @pl.when(c == n_chunks - 1)
    def _():
        st_ref[...] = s_scr[...]


def rope_tables(length):
    t = jnp.arange(length)
    n_freq = HEAD_DIM // 4
    inv = ROPE_BASE ** (-jnp.arange(n_freq, dtype=F32) / n_freq)
    ang = jnp.concatenate([(t // GRID_W).astype(F32)[:, None] * inv,
                           (t % GRID_W).astype(F32)[:, None] * inv], axis=-1)
    cos, sin = jnp.cos(ang), jnp.sin(ang)
    return (jnp.tile(jnp.concatenate([cos, cos], axis=-1), (1, N_HEADS)),
            jnp.tile(jnp.concatenate([-sin, sin], axis=-1), (1, N_HEADS)))


def retention(proj, ret_decay, s0, rope, chunk=RET_CHUNK, nb=RET_REQUESTS_PER_STEP):
    B, L, _ = proj.shape
    G = GROUP_W
    n = L // chunk
    assert L % chunk == 0 and B % nb == 0
    log_g = jax.nn.log_sigmoid(ret_decay)
    cos, sin = rope_tables(L)
    fw = lambda j: pl.BlockSpec((nb, chunk, G), lambda b, c: (b, c, j))
    bw = lambda j: pl.BlockSpec((nb, chunk, G), lambda b, c: (b, n - 1 - c, j))
    tab_f = pl.BlockSpec((chunk, G), lambda b, c: (c, 0))
    tab_b = pl.BlockSpec((chunk, G), lambda b, c: (n - 1 - c, 0))
    state = pl.BlockSpec((nb, 2, N_HEADS, HEAD_DIM, HEAD_DIM), lambda b, c: (b, 0, 0, 0, 0))
    return pl.pallas_call(
        functools.partial(_ret_kernel, chunk=chunk, n_chunks=n, rope=rope),
        grid=(B // nb, n),
        in_specs=[pl.BlockSpec(memory_space=pltpu.SMEM), fw(5), fw(6), fw(7), bw(5), bw(6), bw(7),
                  tab_f, tab_f, tab_b, tab_b, state],
        out_specs=[pl.BlockSpec((nb, chunk, G), lambda b, c: (b, c, 0)),
                   pl.BlockSpec((nb, chunk, G), lambda b, c: (b, n - 1 - c, 0)), state],
        out_shape=[jax.ShapeDtypeStruct((B, L, G), F32), jax.ShapeDtypeStruct((B, L, G), F32),
                   jax.ShapeDtypeStruct((B, 2, N_HEADS, HEAD_DIM, HEAD_DIM), F32)],
        scratch_shapes=[pltpu.VMEM((nb, 2, N_HEADS, HEAD_DIM, HEAD_DIM), F32)],
        compiler_params=pltpu.CompilerParams(
            dimension_semantics=("parallel", "arbitrary"), vmem_limit_bytes=VMEM_LIMIT_BYTES),
        name="retention",
    )(log_g, proj, proj, proj, proj, proj, proj, cos, sin, cos, sin, s0)


def _softmax_pv(s_list, v_list):
    m = s_list[0].max(axis=-1, keepdims=True)
    for s in s_list[1:]:
        m = jnp.maximum(m, s.max(axis=-1, keepdims=True))
    yield
    den = 0.0
    acc = 0.0
    for s, v in zip(s_list, v_list):
        p = jnp.exp(s - m)
        den = den + p.sum(axis=-1, keepdims=True)
        acc = acc + _dot(p.astype(BF16), v, _NN)
    yield
    return acc / den


def _na_kernel(q_ref, k_ref, v_ref, ck_ref, cv_ref, tb_ref, mask_ref, o_ref, *, rows, kh, rows_per_step):
    ck = ck_ref[0, 0].astype(BF16)
    cv = cv_ref[0, 0].astype(BF16)
    mask = mask_ref[...] > 0.0

    def head(j, h):
        r = pl.program_id(1) * rows_per_step + j
        rs = jnp.clip(r - kh // 2, 0, rows - kh)
        start = pl.multiple_of(rs * GRID_W, GRID_W)
        sl = slice(h * HEAD_DIM, (h + 1) * HEAD_DIM)
        kwin = k_ref[0, pl.ds(start, kh * GRID_W), sl].astype(BF16)
        vwin = v_ref[0, pl.ds(start, kh * GRID_W), sl].astype(BF16)
        qh = (q_ref[0, j * GRID_W:(j + 1) * GRID_W, sl] * (HEAD_DIM ** -0.5)).astype(BF16)
        s_loc = _dot(qh, kwin, _NT)
        s_ctx = _dot(qh, ck[:, sl], _NT)
        yield
        s_loc = jnp.where(mask, s_loc + tb_ref[h, r - rs], NEG_INF)
        return (yield from _softmax_pv([s_loc, s_ctx], [vwin, cv[:, sl]]))

    outs = _interleave([head(j, h) for j in range(rows_per_step) for h in range(N_HEADS)])
    for j in range(rows_per_step):
        o_ref[0, j * GRID_W:(j + 1) * GRID_W, :] = jnp.concatenate(outs[j * N_HEADS:(j + 1) * N_HEADS], axis=1)


def na_bias_table(rpb, kh):
    col = np.arange(GRID_W)
    d_col = np.clip(col[None, :] - col[:, None], -(NA_WIN_W - 1), NA_WIN_W - 1) + (NA_WIN_W - 1)
    onehot = jnp.asarray(d_col[:, :, None] == np.arange(2 * NA_WIN_W - 1), F32)
    rows = jnp.stack([rpb[:, NA_WIN_H - 1 - p:NA_WIN_H - 1 - p + kh] for p in range(kh)], axis=1)
    tb = jnp.einsum("hpic,qkc->hpqik", rows, onehot, precision=lax.Precision.HIGHEST)
    col_start = np.clip(col - NA_WIN_W // 2, 0, GRID_W - NA_WIN_W)
    col_in = (col[None, :] >= col_start[:, None]) & (col[None, :] < col_start[:, None] + NA_WIN_W)
    mask = jnp.asarray(np.tile(col_in.astype(np.float32), (1, kh)))
    return tb.reshape(N_HEADS, kh, GRID_W, kh * GRID_W), mask


def neighbourhood_attention(proj, cache_k, cache_v, layer, rpb):
    B, L, _ = proj.shape
    rows = L // GRID_W
    kh = NA_WIN_H
    rps = NA_ROWS_PER_STEP
    assert rows >= kh and rows % rps == 0
    Lc = cache_k.shape[2]
    tb, mask = na_bias_table(rpb, kh)
    G = GROUP_W
    return pl.pallas_call(
        functools.partial(_na_kernel, rows=rows, kh=kh, rows_per_step=rps),
        grid=(B, rows // rps),
        in_specs=[pl.BlockSpec((1, rps * GRID_W, G), lambda b, r: (b, r, 0)),
                  pl.BlockSpec((1, L, G), lambda b, r: (b, 0, 1)),
                  pl.BlockSpec((1, L, G), lambda b, r: (b, 0, 2)),
                  pl.BlockSpec((1, 1, Lc, G), lambda b, r: (b, layer, 0, 0)),
                  pl.BlockSpec((1, 1, Lc, G), lambda b, r: (b, layer, 0, 0)),
                  pl.BlockSpec((N_HEADS, kh, GRID_W, kh * GRID_W), lambda b, r: (0, 0, 0, 0)),
                  pl.BlockSpec((GRID_W, kh * GRID_W), lambda b, r: (0, 0))],
        out_specs=pl.BlockSpec((1, rps * GRID_W, G), lambda b, r: (b, r, 0)),
        out_shape=jax.ShapeDtypeStruct((B, L, G), F32),
        compiler_params=pltpu.CompilerParams(
            dimension_semantics=("parallel", "arbitrary"), vmem_limit_bytes=VMEM_LIMIT_BYTES),
        name="na_attention",
    )(proj, proj, proj, cache_k, cache_v, tb, mask)


def _ctx_attn_kernel(q_ref, k_ref, v_ref, o_ref):
    q = (q_ref[0] * (HEAD_DIM ** -0.5)).astype(BF16)
    k = k_ref[0].astype(BF16)
    v = v_ref[0].astype(BF16)
    def head(h):
        sl = slice(h * HEAD_DIM, (h + 1) * HEAD_DIM)
        s = _dot(q[:, sl], k[:, sl], _NT)
        yield
        return (yield from _softmax_pv([s], [v[:, sl]]))

    o_ref[0] = jnp.concatenate(_interleave([head(h) for h in range(N_HEADS)]), axis=1)


def context_attention(proj):
    B, L, _ = proj.shape
    G = GROUP_W
    return pl.pallas_call(
        _ctx_attn_kernel,
        grid=(B,),
        in_specs=[pl.BlockSpec((1, L, G), lambda b: (b, 0, 0)),
                  pl.BlockSpec((1, L, G), lambda b: (b, 0, 1)),
                  pl.BlockSpec((1, L, G), lambda b: (b, 0, 2))],
        out_specs=pl.BlockSpec((1, L, G), lambda b: (b, 0, 0)),
        out_shape=jax.ShapeDtypeStruct((B, L, G), F32),
        compiler_params=pltpu.CompilerParams(dimension_semantics=("parallel",), vmem_limit_bytes=VMEM_LIMIT_BYTES),
        name="ctx_attention",
    )(proj, proj, proj)


def rms_norm(x, g):
    return x * lax.rsqrt(jnp.mean(x * x, axis=-1, keepdims=True) + EPS) * g


def layer_norm(x, g, b):
    xc = x - jnp.mean(x, axis=-1, keepdims=True)
    return xc * lax.rsqrt(jnp.mean(xc * xc, axis=-1, keepdims=True) + EPS) * g + b


def head_norm(x, g, eps):
    xc = x - jnp.mean(x, axis=-1, keepdims=True)
    y = xc * lax.rsqrt(jnp.mean(xc * xc, axis=-1, keepdims=True) + eps)
    return y.reshape(x.shape[0], x.shape[1], -1) * g


def dwconv(x, w):
    k, ch = w.shape
    return lax.conv_general_dilated(x, w[:, None, :], window_strides=(1,), padding=[(k // 2, k // 2)],
                                    dimension_numbers=("NWC", "WIO", "NWC"), feature_group_count=ch)


def token_mixers(x, mod, rows_per_mod, p, cache):
    B, L, _ = x.shape
    latent = cache is not None
    x2d = x.reshape(B * L, D_MODEL)
    proj2d = norm_matmul(x2d, mod, rows_per_mod, p["norm_g"], p["w_in"])
    proj = proj2d.reshape(B, L, -1)
    a_k = proj[..., GROUP_W:2 * GROUP_W].reshape(B, L, N_HEADS, HEAD_DIM)
    a_v = proj[..., 2 * GROUP_W:3 * GROUP_W].reshape(B, L, N_HEADS, HEAD_DIM)
    if latent:
        o_a = neighbourhood_attention(proj, cache["na_k"], cache["na_v"], cache["layer"], p["na_rpb"])
    else:
        o_a = context_attention(proj)

    o_b = conv_module(proj2d, L, p["conv_dw"], p["conv_ln_g"], p["conv_ln_b"])

    ret0 = cache["ret"] if latent else jnp.zeros((B, 2, N_HEADS, HEAD_DIM, HEAD_DIM), F32)
    o_cf, o_cb, ret_state = retention(proj, p["ret_decay"], ret0, rope=latent)

    r, v, kk, lw2, k2, a2, bonus, gate = rwkv_prep(proj2d, L, p)
    rwkv0 = cache["rwkv"] if latent else jnp.zeros((B, 2, N_HEADS, HEAD_DIM, HEAD_DIM), F32)
    s0t = jnp.swapaxes(rwkv0, -1, -2).transpose(1, 0, 2, 3, 4)
    tok = lambda t: t.reshape(t.shape[:-2] + (B, L, GROUP_W))
    o_df, o_db, st = rwkv7_scan(tok(r), tok(v), tok(kk), tok(lw2), tok(k2), tok(a2), s0t)
    rwkv_state = jnp.swapaxes(st, -1, -2).transpose(1, 0, 2, 3, 4)

    flat = lambda t: t.reshape(B * L, GROUP_W)
    x2d = mix_out(flat(o_a), o_b, flat(o_cf), flat(o_cb), proj2d, flat(o_df), flat(o_db), bonus, gate, x2d, mod,
                  rows_per_mod, p["norm_g"], p["ret_gn"], p["rwkv_gn"], p["w_out"])
    return x2d, (None if latent else (a_k, a_v, ret_state, rwkv_state))


def trunk_layer(x, mod_vec, p, cache):
    B, L, _ = x.shape
    n_mod = mod_vec.shape[0]
    rows_per_mod = B * L // n_mod
    mod = jnp.pad(mod_vec.reshape(n_mod, 6, D_MODEL), ((0, 0), (0, MOD_ROWS - 6), (0, 0)))
    x2d, state = token_mixers(x, mod, rows_per_mod, p, cache)
    x2d = ffn_block(x2d, mod, rows_per_mod, L, p["norm_g"], p["ffn_up"], p["ffn_conv"], p["ffn_down"])
    return x2d.reshape(B, L, D_MODEL), state


def kernel(x_prompt, x_sample, cache_na_k, cache_na_v, state_retention, state_rwkv, c, c_ctx, ada_w, ada_b, norm_g, w_in, w_out, na_rpb, conv_dw, conv_ln_g, conv_ln_b, ret_decay, ret_gn, rwkv_shift, rwkv_w0, rwkv_w2, rwkv_a0, rwkv_a2, rwkv_g2, rwkv_kk, rwkv_ka, rwkv_rk, rwkv_gn, ffn_up, ffn_conv, ffn_down):
    y_prompt, y_sample = x_prompt, x_sample
    new_k, new_v, new_ret, new_rwkv = [], [], [], []
    for l in range(DEPTH):
        p = {"norm_g": norm_g[l], "w_in": w_in[l].astype(BF16), "w_out": w_out[l].astype(BF16), "na_rpb": na_rpb[l],
             "conv_dw": conv_dw[l], "conv_ln_g": conv_ln_g[l], "conv_ln_b": conv_ln_b[l],
             "ret_decay": ret_decay[l], "ret_gn": ret_gn[l], "rwkv_shift": rwkv_shift[l],
             "rwkv_w0": rwkv_w0[l], "rwkv_w2": rwkv_w2[l], "rwkv_a0": rwkv_a0[l], "rwkv_a2": rwkv_a2[l],
             "rwkv_g2": rwkv_g2[l], "rwkv_kk": rwkv_kk[l], "rwkv_ka": rwkv_ka[l], "rwkv_rk": rwkv_rk[l],
             "rwkv_gn": rwkv_gn[l], "ffn_up": ffn_up[l].astype(BF16), "ffn_conv": ffn_conv[l],
             "ffn_down": ffn_down[l].astype(BF16)}
        mod_ctx = (jax.nn.silu(c_ctx) @ ada_w[l] + ada_b[l])[None]
        y_prompt, (k_l, v_l, ret_l, rwkv_l) = trunk_layer(y_prompt, mod_ctx, p, None)
        new_k.append(k_l)
        new_v.append(v_l)
        new_ret.append(ret_l)
        new_rwkv.append(rwkv_l)
        mod_lat = jax.nn.silu(c) @ ada_w[l] + ada_b[l]
        cache_l = {"na_k": cache_na_k.reshape(cache_na_k.shape[:3] + (GROUP_W,)),
                   "na_v": cache_na_v.reshape(cache_na_v.shape[:3] + (GROUP_W,)), "layer": l,
                   "ret": state_retention[:, l], "rwkv": state_rwkv[:, l]}
        y_sample, _ = trunk_layer(y_sample, mod_lat, p, cache_l)
    return (y_prompt, y_sample, jnp.stack(new_k, axis=1), jnp.stack(new_v, axis=1),
            jnp.stack(new_ret, axis=1), jnp.stack(new_rwkv, axis=1))
```

```python
import functools

import jax
import jax.numpy as jnp
import numpy as np
from jax import lax
from jax.experimental import pallas as pl
from jax.experimental.pallas import tpu as pltpu

F32 = jnp.float32
BF16 = jnp.bfloat16

D_MODEL = 1024
DEPTH = 2
GRID_W = 64
N_MIXERS = 4
GROUP_W = D_MODEL // N_MIXERS
HEAD_DIM = 64
N_HEADS = GROUP_W // HEAD_DIM
NA_WIN_H = 8
NA_WIN_W = 16
Q_BLOCK = 128
CONV_W = 31
RET_CHUNK = 128
RWKV_LORA_W = 64
RWKV_LORA_A = 64
RWKV_LORA_G = 128
D_FF = 2816
ROPE_BASE = 10000.0
EPS = 1e-6
RWKV_GN_EPS = 64e-5
NEG_INF = -1e30
SPLIT_SIZES = (GROUP_W,) * 12 + (RWKV_LORA_W, RWKV_LORA_A, RWKV_LORA_G)
SPLIT_POINTS = tuple(int(s) for s in np.cumsum(SPLIT_SIZES)[:-1])

RWKV_CHUNK = 64
RET_REQUESTS_PER_STEP = 2
RWKV_REQUESTS_PER_STEP = 4
VMEM_LIMIT_BYTES = 48 * 1024 * 1024
SUBLANES = 8
MOD_ROWS = 8
NA_ROWS_PER_STEP = 8
ROW_TILE = 512
FFN_CHUNK = 256


def _mm_kernel(a_ref, b_ref, o_ref):
    o_ref[...] = jnp.dot(a_ref[...].astype(BF16), b_ref[...], preferred_element_type=F32)


def _pick_tile(n, target):
    best = 128
    for t in range(128, target + 1, 128):
        if n % t == 0:
            best = t
    return best


def matmul(a, b_bf16, tm=512, tn_target=1792):
    m, k = a.shape
    n = b_bf16.shape[1]
    tn = _pick_tile(n, tn_target)
    tm = min(tm, m)
    assert m % tm == 0 and n % tn == 0
    return pl.pallas_call(
        _mm_kernel,
        grid=(n // tn, m // tm),
        in_specs=[pl.BlockSpec((tm, k), lambda j, i: (i, 0)),
                  pl.BlockSpec((k, tn), lambda j, i: (0, j))],
        out_specs=pl.BlockSpec((tm, tn), lambda j, i: (i, j)),
        out_shape=jax.ShapeDtypeStruct((m, n), F32),
        compiler_params=pltpu.CompilerParams(
            dimension_semantics=("parallel", "parallel"), vmem_limit_bytes=VMEM_LIMIT_BYTES),
        name="matmul",
    )(a, b_bf16)


def _rms(x, g):
    return x * lax.rsqrt(jnp.mean(x * x, axis=-1, keepdims=True) + EPS) * g


def _resident(shape):
    return pl.BlockSpec(shape, lambda *_: (0,) * len(shape), pipeline_mode=pl.Buffered(1))


def _mod_spec(tm, rows_per_mod):
    return pl.BlockSpec((1, MOD_ROWS, D_MODEL), lambda i: ((i * tm) // rows_per_mod, 0, 0))


def _norm_mm_kernel(x_ref, mod_ref, g_ref, w_ref, o_ref):
    mod = mod_ref[0]
    h = _rms(x_ref[...], g_ref[0:1]) * (1.0 + mod[1:2]) + mod[0:1]
    o_ref[...] = jnp.dot(h.astype(BF16), w_ref[...], preferred_element_type=F32)


def norm_matmul(x, mod, rows_per_mod, norm_g, w_bf16, tm=ROW_TILE):
    m, n = x.shape[0], w_bf16.shape[1]
    tm = min(tm, rows_per_mod)
    return pl.pallas_call(
        _norm_mm_kernel,
        grid=(m // tm,),
        in_specs=[pl.BlockSpec((tm, D_MODEL), lambda i: (i, 0)), _mod_spec(tm, rows_per_mod),
                  _resident(norm_g.shape), _resident(w_bf16.shape)],
        out_specs=pl.BlockSpec((tm, n), lambda i: (i, 0)),
        out_shape=jax.ShapeDtypeStruct((m, n), F32),
        compiler_params=pltpu.CompilerParams(dimension_semantics=("parallel",), vmem_limit_bytes=VMEM_LIMIT_BYTES),
        name="norm_matmul",
    )(x, mod, norm_g, w_bf16)


def _head_sum(y):
    g = y.shape[1]
    row = lax.broadcasted_iota(jnp.int32, (g, g), 0) // HEAD_DIM
    col = lax.broadcasted_iota(jnp.int32, (g, g), 1) // HEAD_DIM
    ones = jnp.where(row == col, 1.0, 0.0).astype(BF16)
    hi = y.astype(BF16)
    lo = (y - hi.astype(F32)).astype(BF16)
    return jnp.dot(hi, ones, preferred_element_type=F32) + jnp.dot(lo, ones, preferred_element_type=F32)


def _head_norm(y, g, eps):
    yc = y - _head_sum(y) * (1.0 / HEAD_DIM)
    return yc * lax.rsqrt(_head_sum(yc * yc) * (1.0 / HEAD_DIM) + eps) * g


def _mix_out_kernel(oa_ref, ob_ref, cf_ref, cb_ref, rg_ref, df_ref, db_ref, bonus_ref, gate_ref, x_ref, mod_ref,
                    g_ref, gn_ref, w_ref, o_ref):
    G = GROUP_W
    rg = rg_ref[...]
    o_c = _head_norm(cf_ref[...] + cb_ref[...], gn_ref[0:1], EPS) * (rg * jax.nn.sigmoid(rg))
    o_d = (_head_norm(df_ref[...] + db_ref[...], gn_ref[1:2], RWKV_GN_EPS) + bonus_ref[...]) * gate_ref[...]
    m = 0.0
    for j, o in enumerate((oa_ref[...], ob_ref[...], o_c, o_d)):
        m = m + jnp.dot(o.astype(BF16), w_ref[j * G:(j + 1) * G, :], preferred_element_type=F32)
    o_ref[...] = x_ref[...] + mod_ref[0][2:3] * _rms(m, g_ref[1:2])


def mix_out(o_a, o_b, o_cf, o_cb, proj2d, o_df, o_db, bonus, gate, x, mod, rows_per_mod, norm_g, ret_gn, rwkv_gn,
            w_bf16, tm=ROW_TILE):
    m = x.shape[0]
    tm = min(tm, rows_per_mod)
    G = GROUP_W
    grp = pl.BlockSpec((tm, G), lambda i: (i, 0))
    gn = jnp.stack([ret_gn, rwkv_gn])
    return pl.pallas_call(
        _mix_out_kernel,
        grid=(m // tm,),
        in_specs=[grp, grp, grp, grp, pl.BlockSpec((tm, G), lambda i: (i, 8)), grp, grp, grp, grp,
                  pl.BlockSpec((tm, D_MODEL), lambda i: (i, 0)), _mod_spec(tm, rows_per_mod),
                  _resident(norm_g.shape), _resident(gn.shape), _resident(w_bf16.shape)],
        out_specs=pl.BlockSpec((tm, D_MODEL), lambda i: (i, 0)),
        out_shape=jax.ShapeDtypeStruct((m, D_MODEL), F32),
        compiler_params=pltpu.CompilerParams(dimension_semantics=("parallel",), vmem_limit_bytes=VMEM_LIMIT_BYTES),
        name="mix_out",
    )(o_a, o_b, o_cf, o_cb, proj2d, o_df, o_db, bonus, gate, x, mod, norm_g, gn, w_bf16)


def _rwkv_prep_kernel(rkv_ref, rkvp_ref, rkvn_ref, lora_ref, sh_ref, w0_ref, w2_ref, a0_ref, a2_ref, g2_ref, vec_ref,
                      r_ref, v_ref, kk_ref, lw_ref, k_ref, a_ref, bonus_ref, gate_ref, *, seq_len):
    G = GROUP_W
    x = rkv_ref[...]
    x_prev, x_next = _shift_rows(x, rkvp_ref[SUBLANES - 1:SUBLANES], rkvn_ref[0:1], *_seq_ends(x.shape[0], seq_len))
    rkv = sh_ref[0:1] * x_prev + sh_ref[1:2] * x + sh_ref[2:3] * x_next
    d_r, d_k, d_v = rkv[:, :G], rkv[:, G:2 * G], rkv[:, 2 * G:]
    lora = lora_ref[...]
    w_low = jnp.tanh(lora[:, :RWKV_LORA_W]).astype(BF16)
    a_low = lora[:, RWKV_LORA_W:RWKV_LORA_W + RWKV_LORA_A].astype(BF16)
    g_low = jax.nn.sigmoid(lora[:, RWKV_LORA_W + RWKV_LORA_A:]).astype(BF16)
    kk_w, ka, rk = vec_ref[0:1], vec_ref[1:2], vec_ref[2:3]
    k_sum = 0.0
    for d in range(2):
        z = w0_ref[d:d + 1] + jnp.dot(w_low, w2_ref[d].astype(BF16), preferred_element_type=F32)
        lw_ref[d] = -float(np.exp(-0.5)) * jax.nn.sigmoid(z)
        a = jax.nn.sigmoid(a0_ref[d:d + 1] + jnp.dot(a_low, a2_ref[d].astype(BF16), preferred_element_type=F32))
        k_d = d_k * (1.0 + (a - 1.0) * ka)
        a_ref[d] = a
        k_ref[d] = k_d
        k_sum = k_sum + k_d
    kk = d_k * kk_w
    r_ref[...] = d_r
    v_ref[...] = d_v
    kk_ref[...] = kk * lax.rsqrt(_head_sum(kk * kk) + 1e-12)
    bonus_ref[...] = _head_sum(d_r * k_sum * rk) * d_v
    gate_ref[...] = jnp.dot(g_low, g2_ref[...].astype(BF16), preferred_element_type=F32)


def rwkv_prep(proj2d, seq_len, p, tm=ROW_TILE):
    m = proj2d.shape[0]
    tm = _tile_rows(tm, seq_len)
    G = GROUP_W
    prev_spec, next_spec = _halo_specs(tm, m, 3 * G, col=3)
    vec = jnp.stack([p["rwkv_kk"], p["rwkv_ka"], p["rwkv_rk"]])
    one = pl.BlockSpec((tm, G), lambda i: (i, 0))
    two = pl.BlockSpec((2, tm, G), lambda i: (0, i, 0))
    sds1 = jax.ShapeDtypeStruct((m, G), F32)
    sds2 = jax.ShapeDtypeStruct((2, m, G), F32)
    params = (p["rwkv_shift"], p["rwkv_w0"], p["rwkv_w2"], p["rwkv_a0"], p["rwkv_a2"], p["rwkv_g2"], vec)
    return pl.pallas_call(
        functools.partial(_rwkv_prep_kernel, seq_len=seq_len),
        grid=(m // tm,),
        in_specs=[pl.BlockSpec((tm, 3 * G), lambda i: (i, 3)), prev_spec, next_spec,
                  pl.BlockSpec((tm, G), lambda i: (i, 12))] + [_resident(t.shape) for t in params],
        out_specs=[one, one, one, two, two, two, one, one],
        out_shape=[sds1, sds1, sds1, sds2, sds2, sds2, sds1, sds1],
        compiler_params=pltpu.CompilerParams(dimension_semantics=("parallel",), vmem_limit_bytes=VMEM_LIMIT_BYTES),
        name="rwkv_prep",
    )(proj2d, proj2d, proj2d, proj2d, *params)


def _halo_specs(tm, m, width, col=0):
    blocks = tm // SUBLANES
    return (pl.BlockSpec((SUBLANES, width), lambda i: (jnp.maximum(i * blocks - 1, 0), col)),
            pl.BlockSpec((SUBLANES, width), lambda i: (jnp.minimum((i + 1) * blocks, m // SUBLANES - 1), col)))


def _seq_ends(tm, seq_len):
    if seq_len >= tm:
        t = pl.program_id(0) % (seq_len // tm)
        return t == 0, t == seq_len // tm - 1
    assert tm % seq_len == 0 and seq_len & (seq_len - 1) == 0
    pos = lax.broadcasted_iota(jnp.int32, (tm, 1), 0) & (seq_len - 1)
    return pos == 0, pos == seq_len - 1


def _tile_rows(tm, seq_len):
    return tm if (seq_len % tm == 0 or tm % seq_len == 0) else seq_len


def _shift_rows(u, prev_row, next_row, first, last):
    tm = u.shape[0]
    if first.ndim:
        return (jnp.where(first, 0.0, pltpu.roll(u, 1, axis=0)), jnp.where(last, 0.0, pltpu.roll(u, tm - 1, axis=0)))
    row = lax.broadcasted_iota(jnp.int32, (tm, 1), 0)
    u_prev = jnp.where(row == 0, prev_row * jnp.where(first, 0.0, 1.0), pltpu.roll(u, 1, axis=0))
    u_next = jnp.where(row == tm - 1, next_row * jnp.where(last, 0.0, 1.0), pltpu.roll(u, tm - 1, axis=0))
    return u_prev, u_next


def _ffn_kernel(x_ref, xp_ref, xn_ref, mod_ref, g_ref, up_ref, cw_ref, down_ref, o_ref, act_scr, *, seq_len, cw):
    first, last = _seq_ends(x_ref.shape[0], seq_len)
    mod = mod_ref[0]
    g2, g3 = g_ref[2:3], g_ref[3:4]

    def pre(x):
        return (_rms(x, g2) * (1.0 + mod[4:5]) + mod[3:4]).astype(BF16)

    x = x_ref[...]
    h = pre(x)
    hh = pre(jnp.concatenate([xp_ref[...], xn_ref[...]], axis=0))
    def cols(j, half):
        return slice(half * D_FF + j * cw, half * D_FF + (j + 1) * cw)

    def up(j):
        return [(jnp.dot(h, up_ref[:, cols(j, half)], preferred_element_type=F32),
                 jnp.dot(hh, up_ref[:, cols(j, half)], preferred_element_type=F32)) for half in range(2)]

    def conv_act(j, ups):
        conv = []
        for half, (u, uh) in enumerate(ups):
            u_prev, u_next = _shift_rows(u, uh[SUBLANES - 1:SUBLANES], uh[SUBLANES:SUBLANES + 1], first, last)
            wc = cw_ref[:, cols(j, half)]
            conv.append(wc[0:1] * u_prev + wc[1:2] * u + wc[2:3] * u_next)
        return (conv[0] * jax.nn.sigmoid(conv[0]) * conv[1]).astype(BF16)

    n = D_FF // cw
    ahead = 2
    ups = [up(j) for j in range(min(ahead, n))]
    for j in range(n):
        if j + ahead < n:
            ups.append(up(j + ahead))
        act_scr[:, j * cw:(j + 1) * cw] = conv_act(j, ups[j])
    f = jnp.dot(act_scr[...], down_ref[...], preferred_element_type=F32)
    o_ref[...] = x + mod[5:6] * _rms(f, g3)


def ffn_block(x, mod, rows_per_mod, seq_len, norm_g, up_bf16, w_conv, down_bf16, tm=ROW_TILE, cw=FFN_CHUNK):
    m = x.shape[0]
    tm = _tile_rows(tm, seq_len)
    assert D_FF % cw == 0
    prev_spec, next_spec = _halo_specs(tm, m, D_MODEL)
    return pl.pallas_call(
        functools.partial(_ffn_kernel, seq_len=seq_len, cw=cw),
        grid=(m // tm,),
        in_specs=[pl.BlockSpec((tm, D_MODEL), lambda i: (i, 0)), prev_spec, next_spec, _mod_spec(tm, rows_per_mod),
                  _resident(norm_g.shape), _resident(up_bf16.shape), _resident(w_conv.shape),
                  _resident(down_bf16.shape)],
        out_specs=pl.BlockSpec((tm, D_MODEL), lambda i: (i, 0)),
        out_shape=jax.ShapeDtypeStruct((m, D_MODEL), F32),
        scratch_shapes=[pltpu.VMEM((tm, D_FF), BF16)],
        compiler_params=pltpu.CompilerParams(dimension_semantics=("parallel",), vmem_limit_bytes=VMEM_LIMIT_BYTES),
        name="ffn_block",
    )(x, x, x, mod, norm_g, up_bf16, w_conv, down_bf16)


def _conv_module_kernel(a_ref, b_ref, ap_ref, bp_ref, an_ref, bn_ref, w_ref, ln_ref, o_ref, pad_scr, sh_scr, *,
                        seq_tiles):
    i = pl.program_id(0)
    tm = a_ref.shape[0]
    halo = 2 * SUBLANES
    keep_prev = jnp.where(i % seq_tiles == 0, 0.0, 1.0)
    keep_next = jnp.where(i % seq_tiles == seq_tiles - 1, 0.0, 1.0)

    def glu(a, b):
        return a * jax.nn.sigmoid(b)

    pad_scr[0:halo] = glu(ap_ref[...], bp_ref[...]) * keep_prev
    pad_scr[halo:halo + tm] = glu(a_ref[...], b_ref[...])
    pad_scr[halo + tm:2 * halo + tm] = glu(an_ref[...], bn_ref[...]) * keep_next
    offs = [halo - CONV_W // 2 + j for j in range(CONV_W)]
    span = tm + (max(offs) // SUBLANES) * SUBLANES
    for b in range(SUBLANES):
        sh_scr[b] = pad_scr[b:b + span]
    acc = jnp.zeros((tm, GROUP_W), F32)
    for j, off in enumerate(offs):
        a, b = divmod(off, SUBLANES)
        acc = acc + w_ref[j:j + 1] * sh_scr[b, a * SUBLANES:a * SUBLANES + tm]
    xc = acc - jnp.mean(acc, axis=-1, keepdims=True)
    y = xc * lax.rsqrt(jnp.mean(xc * xc, axis=-1, keepdims=True) + EPS) * ln_ref[0:1] + ln_ref[1:2]
    o_ref[...] = y * jax.nn.sigmoid(y)


def conv_module(proj2d, seq_len, w_dw, ln_g, ln_b, tm=ROW_TILE):
    m = proj2d.shape[0]
    tm = min(tm, seq_len)
    G = GROUP_W
    halo = 2 * SUBLANES
    assert CONV_W // 2 <= halo and seq_len % tm == 0
    blocks = tm // halo

    def prev(col):
        return pl.BlockSpec((halo, G), lambda i: (jnp.maximum(i * blocks - 1, 0), col))

    def nxt(col):
        return pl.BlockSpec((halo, G), lambda i: (jnp.minimum((i + 1) * blocks, m // halo - 1), col))

    ln = jnp.stack([ln_g, ln_b])
    return pl.pallas_call(
        functools.partial(_conv_module_kernel, seq_tiles=seq_len // tm),
        grid=(m // tm,),
        in_specs=[pl.BlockSpec((tm, G), lambda i: (i, 3)), pl.BlockSpec((tm, G), lambda i: (i, 4)),
                  prev(3), prev(4), nxt(3), nxt(4), _resident(w_dw.shape), _resident(ln.shape)],
        out_specs=pl.BlockSpec((tm, G), lambda i: (i, 0)),
        out_shape=jax.ShapeDtypeStruct((m, G), F32),
        scratch_shapes=[pltpu.VMEM((tm + 2 * halo, G), F32),
                        pltpu.VMEM((SUBLANES, tm + ((halo + CONV_W // 2) // SUBLANES) * SUBLANES, G), F32)],
        compiler_params=pltpu.CompilerParams(dimension_semantics=("parallel",), vmem_limit_bytes=VMEM_LIMIT_BYTES),
        name="conv_module",
    )(proj2d, proj2d, proj2d, proj2d, proj2d, proj2d, w_dw, ln)


def _dot(a, b, dims):
    return lax.dot_general(a, b, (dims, ((), ())), preferred_element_type=F32)


_NN = ((1,), (0,))
_NT = ((1,), (1,))
_TN = ((0,), (0,))


def _bdot(a, b, dims=_NN):
    return _dot(a.astype(BF16), b.astype(BF16), dims)


def _tri_inverse(n_mat, eye, row, col):
    C = n_mat.shape[0]
    nd = jnp.where((row // 8) == (col // 8), n_mat, 0.0)
    s1 = eye + nd
    p1 = _bdot(nd, nd)
    yield
    s2 = _bdot(p1, s1)
    p2 = _bdot(p1, p1)
    yield
    s2 = s1 + s2
    d = _bdot(p2, s2)
    yield
    d = s2 + d
    size = 8
    while size < C:
        inner = (row // size) == (col // size)
        outer = (row // (2 * size)) == (col // (2 * size))
        n_off = jnp.where(outer & jnp.logical_not(inner), n_mat, 0.0)
        t = _bdot(n_off, d)
        yield
        t = _bdot(d, t)
        yield
        d = d + t
        size *= 2
    return d


def _interleave(gens):
    results = [None] * len(gens)
    active = list(enumerate(gens))
    while active:
        still = []
        for i, g in active:
            try:
                next(g)
                still.append((i, g))
            except StopIteration as stop:
                results[i] = stop.value
        active = still
    return results


def _rwkv_masks(fwd, C):
    row = lax.broadcasted_iota(jnp.int32, (C, C), 0)
    col = lax.broadcasted_iota(jnp.int32, (C, C), 1)
    incl = (row >= col) if fwd else (row <= col)
    strict = (row > col) if fwd else (row < col)
    return row, col, incl, strict


def _rwkv_cumsum(fwd, lw, C):
    incl_bf = jnp.where(_rwkv_masks(fwd, C)[2], 1.0, 0.0).astype(BF16)
    l_hi = lw.astype(BF16)
    l_lo = (lw - l_hi.astype(F32)).astype(BF16)
    return _dot(incl_bf, l_hi, _NN) + _dot(incl_bf, l_lo, _NN)


def _rwkv_chain(fwd, r, v, kk, lw, k, a, cs, s0, C):
    v = v.astype(BF16)
    row, col, incl, strict = _rwkv_masks(fwd, C)
    eye = jnp.where(row == col, 1.0, 0.0)
    tot = jnp.sum(lw, axis=0, keepdims=True)

    beta = kk * a
    e_out = jnp.exp(-cs)
    e_rem = jnp.exp(tot - cs)
    at = -kk * jnp.exp(cs - lw)
    rt = r * jnp.exp(cs)
    ar = jnp.concatenate([at, rt], axis=0)

    big = _bdot(ar, jnp.concatenate([beta * e_out, k * e_out, s0], axis=0), _NT)
    yield
    big_b, big_k, x = big[:, :C], big[:, C:2 * C], big[:, 2 * C:]
    a_ab = jnp.where(strict, big_b[:C], 0.0)
    a_ak = jnp.where(strict, big_k[:C], 0.0)
    a_rb = jnp.where(incl, big_b[C:], 0.0)
    a_rk = jnp.where(incl, big_k[C:], 0.0)
    akv = _bdot(a_ak, v)
    ork = _bdot(a_rk, v)
    vk = _bdot(v, k * e_rem, _TN)
    minv = yield from _tri_inverse(a_ab, eye, row, col)

    u = _bdot(minv, x[:C] + akv)
    yield
    o = x[C:] + ork + _bdot(a_rb, u)
    return o, s0 * jnp.exp(tot) + vk + _bdot(u, beta * e_rem, _TN)


def _rwkv_kernel(rf_ref, vf_ref, kkf_ref, lwf_ref, kf_ref, af_ref, rb_ref, vb_ref, kkb_ref, lwb_ref, kb_ref, ab_ref,
                 s0_ref, of_ref, ob_ref, st_ref, t_scr, *, chunk, n_chunks):
    c = pl.program_id(1)
    nb = rf_ref.shape[0]

    @pl.when(c == 0)
    def _():
        t_scr[...] = s0_ref[...]

    chains = []
    for b in range(nb):
        fw = [ref[b] for ref in (rf_ref, vf_ref, kkf_ref)] + [ref[0, b] for ref in (lwf_ref, kf_ref, af_ref)]
        bw = [ref[b] for ref in (rb_ref, vb_ref, kkb_ref)] + [ref[0, b] for ref in (lwb_ref, kb_ref, ab_ref)]
        for d, (fwd, tiles) in enumerate(((True, fw), (False, bw))):
            tiles = tiles + [_rwkv_cumsum(fwd, tiles[3], chunk)]
            for h in range(N_HEADS):
                sl = slice(h * HEAD_DIM, (h + 1) * HEAD_DIM)
                chains.append(_rwkv_chain(fwd, *[t[:, sl] for t in tiles], t_scr[d, b, h], chunk))
    res = _interleave(chains)
    for b in range(nb):
        for d, o_ref in enumerate((of_ref, ob_ref)):
            first = (b * 2 + d) * N_HEADS
            o_ref[b] = jnp.concatenate([o for o, _ in res[first:first + N_HEADS]], axis=1)
            for h in range(N_HEADS):
                t_scr[d, b, h] = res[first + h][1]

    @pl.when(c == n_chunks - 1)
    def _():
        st_ref[...] = t_scr[...]


def rwkv7_scan(r, v, kk, lw, k, a, s0, chunk=RWKV_CHUNK, nb=RWKV_REQUESTS_PER_STEP):
    B, L, G = r.shape
    n = L // chunk
    assert L % chunk == 0 and B % nb == 0
    sh_f = pl.BlockSpec((nb, chunk, G), lambda b, c: (b, c, 0))
    sh_b = pl.BlockSpec((nb, chunk, G), lambda b, c: (b, n - 1 - c, 0))
    pd_f = pl.BlockSpec((1, nb, chunk, G), lambda b, c: (0, b, c, 0))
    pd_b = pl.BlockSpec((1, nb, chunk, G), lambda b, c: (1, b, n - 1 - c, 0))
    state = pl.BlockSpec((2, nb, N_HEADS, HEAD_DIM, HEAD_DIM), lambda b, c: (0, b, 0, 0, 0))
    return pl.pallas_call(
        functools.partial(_rwkv_kernel, chunk=chunk, n_chunks=n),
        grid=(B // nb, n),
        in_specs=[sh_f, sh_f, sh_f, pd_f, pd_f, pd_f, sh_b, sh_b, sh_b, pd_b, pd_b, pd_b, state],
        out_specs=[sh_f, sh_b, state],
        out_shape=[jax.ShapeDtypeStruct((B, L, G), F32), jax.ShapeDtypeStruct((B, L, G), F32),
                   jax.ShapeDtypeStruct((2, B, N_HEADS, HEAD_DIM, HEAD_DIM), F32)],
        scratch_shapes=[pltpu.VMEM((2, nb, N_HEADS, HEAD_DIM, HEAD_DIM), F32)],
        compiler_params=pltpu.CompilerParams(
            dimension_semantics=("parallel", "arbitrary"), vmem_limit_bytes=VMEM_LIMIT_BYTES),
        name="rwkv7_scan",
    )(r, v, kk, lw, k, a, r, v, kk, lw, k, a, s0)


def _ret_chain(fwd, b, h, q, k, v, lg, o_ref, s_scr, C):
    sl = slice(h * HEAD_DIM, (h + 1) * HEAD_DIM)
    row = lax.broadcasted_iota(jnp.int32, (C, C), 0)
    col = lax.broadcasted_iota(jnp.int32, (C, C), 1)
    dist = ((row - col) if fwd else (col - row)).astype(F32)
    decay_in = jnp.where(dist >= 0, jnp.exp(jnp.maximum(dist, 0.0) * lg), 0.0)
    pos = lax.broadcasted_iota(jnp.int32, (C, 1), 0).astype(F32)
    step = pos if fwd else (C - 1.0) - pos
    decay_q = jnp.exp((step + 1.0) * lg)
    decay_k = jnp.exp((C - 1.0 - step) * lg)
    qh, kh, vh = q[:, sl], k[:, sl], v[:, sl].astype(BF16)
    d = 0 if fwd else 1
    s0 = s_scr[b, d, h]
    qk = _bdot(qh, kh, _NT)
    qs = _bdot(qh, s0)
    kv = _bdot(kh * decay_k, vh, _TN)
    yield
    ov = _bdot(qk * decay_in, vh)
    yield
    o_ref[b, :, sl] = ov + qs * decay_q
    s_scr[b, d, h] = s0 * jnp.exp(C * lg) + kv


def _rope(x, cos, sin):
    lane = lax.broadcasted_iota(jnp.int32, x.shape, 1)
    n = x.shape[1]
    swapped = jnp.where((lane % HEAD_DIM) < HEAD_DIM // 2,
                        pltpu.roll(x, n - HEAD_DIM // 2, axis=1), pltpu.roll(x, HEAD_DIM // 2, axis=1))
    return x * cos + swapped * sin


def _ret_kernel(lg_ref, qf_ref, kf_ref, vf_ref, qb_ref, kb_ref, vb_ref, cf_ref, sf_ref, cb_ref, sb_ref, s0_ref,
                of_ref, ob_ref, st_ref, s_scr, *, chunk, n_chunks, rope):
    c = pl.program_id(1)

    @pl.when(c == 0)
    def _():
        s_scr[...] = s0_ref[...]

    scale = HEAD_DIM ** -0.5
    chains = []
    for b in range(qf_ref.shape[0]):
        qf, kf, qb, kb = qf_ref[b], kf_ref[b] * scale, qb_ref[b], kb_ref[b] * scale
        if rope:
            qf, kf = _rope(qf, cf_ref[...], sf_ref[...]), _rope(kf, cf_ref[...], sf_ref[...])
            qb, kb = _rope(qb, cb_ref[...], sb_ref[...]), _rope(kb, cb_ref[...], sb_ref[...])
        vf, vb = vf_ref[b], vb_ref[b]
        chains += [_ret_chain(True, b, h, qf, kf, vf, lg_ref[0, h], of_ref, s_scr, chunk) for h in range(N_HEADS)]
        chains += [_ret_chain(False, b, h, qb, kb, vb, lg_ref[1, h], ob_ref, s_scr, chunk) for h in range(N_HEADS)]
    _interleave(chains)

    @pl.when(c == n_chunks - 1)
    def _():
        st_ref[...] = s_scr[...]


def rope_tables(length):
    t = jnp.arange(length)
    n_freq = HEAD_DIM // 4
    inv = ROPE_BASE ** (-jnp.arange(n_freq, dtype=F32) / n_freq)
    ang = jnp.concatenate([(t // GRID_W).astype(F32)[:, None] * inv,
                           (t % GRID_W).astype(F32)[:, None] * inv], axis=-1)
    cos, sin = jnp.cos(ang), jnp.sin(ang)
    return (jnp.tile(jnp.concatenate([cos, cos], axis=-1), (1, N_HEADS)),
            jnp.tile(jnp.concatenate([-sin, sin], axis=-1), (1, N_HEADS)))


def retention(proj, ret_decay, s0, rope, chunk=RET_CHUNK, nb=RET_REQUESTS_PER_STEP):
    B, L, _ = proj.shape
    G = GROUP_W
    n = L // chunk
    assert L % chunk == 0 and B % nb == 0
    log_g = jax.nn.log_sigmoid(ret_decay)
    cos, sin = rope_tables(L)
    fw = lambda j: pl.BlockSpec((nb, chunk, G), lambda b, c: (b, c, j))
    bw = lambda j: pl.BlockSpec((nb, chunk, G), lambda b, c: (b, n - 1 - c, j))
    tab_f = pl.BlockSpec((chunk, G), lambda b, c: (c, 0))
    tab_b = pl.BlockSpec((chunk, G), lambda b, c: (n - 1 - c, 0))
    state = pl.BlockSpec((nb, 2, N_HEADS, HEAD_DIM, HEAD_DIM), lambda b, c: (b, 0, 0, 0, 0))
    return pl.pallas_call(
        functools.partial(_ret_kernel, chunk=chunk, n_chunks=n, rope=rope),
        grid=(B // nb, n),
        in_specs=[pl.BlockSpec(memory_space=pltpu.SMEM), fw(5), fw(6), fw(7), bw(5), bw(6), bw(7),
                  tab_f, tab_f, tab_b, tab_b, state],
        out_specs=[pl.BlockSpec((nb, chunk, G), lambda b, c: (b, c, 0)),
                   pl.BlockSpec((nb, chunk, G), lambda b, c: (b, n - 1 - c, 0)), state],
        out_shape=[jax.ShapeDtypeStruct((B, L, G), F32), jax.ShapeDtypeStruct((B, L, G), F32),
                   jax.ShapeDtypeStruct((B, 2, N_HEADS, HEAD_DIM, HEAD_DIM), F32)],
        scratch_shapes=[pltpu.VMEM((nb, 2, N_HEADS, HEAD_DIM, HEAD_DIM), F32)],
        compiler_params=pltpu.CompilerParams(
            dimension_semantics=("parallel", "arbitrary"), vmem_limit_bytes=VMEM_LIMIT_BYTES),
        name="retention",
    )(log_g, proj, proj, proj, proj, proj, proj, cos, sin, cos, sin, s0)


def _softmax_pv(s_list, v_list):
    m = s_list[0].max(axis=-1, keepdims=True)
    for s in s_list[1:]:
        m = jnp.maximum(m, s.max(axis=-1, keepdims=True))
    yield
    den = 0.0
    acc = 0.0
    for s, v in zip(s_list, v_list):
        p = jnp.exp(s - m)
        den = den + p.sum(axis=-1, keepdims=True)
        acc = acc + _dot(p.astype(BF16), v, _NN)
    yield
    return acc / den


def _na_kernel(q_ref, k_ref, v_ref, ck_ref, cv_ref, tb_ref, mask_ref, o_ref, *, rows, kh, rows_per_step):
    ck = ck_ref[0, 0].astype(BF16)
    cv = cv_ref[0, 0].astype(BF16)
    mask = mask_ref[...] > 0.0

    def head(j, h):
        r = pl.program_id(1) * rows_per_step + j
        rs = jnp.clip(r - kh // 2, 0, rows - kh)
        start = pl.multiple_of(rs * GRID_W, GRID_W)
        sl = slice(h * HEAD_DIM, (h + 1) * HEAD_DIM)
        kwin = k_ref[0, pl.ds(start, kh * GRID_W), sl].astype(BF16)
        vwin = v_ref[0, pl.ds(start, kh * GRID_W), sl].astype(BF16)
        qh = (q_ref[0, j * GRID_W:(j + 1) * GRID_W, sl] * (HEAD_DIM ** -0.5)).astype(BF16)
        s_loc = _dot(qh, kwin, _NT)
        s_ctx = _dot(qh, ck[:, sl], _NT)
        yield
        s_loc = jnp.where(mask, s_loc + tb_ref[h, r - rs], NEG_INF)
        return (yield from _softmax_pv([s_loc, s_ctx], [vwin, cv[:, sl]]))

    outs = _interleave([head(j, h) for j in range(rows_per_step) for h in range(N_HEADS)])
    for j in range(rows_per_step):
        o_ref[0, j * GRID_W:(j + 1) * GRID_W, :] = jnp.concatenate(outs[j * N_HEADS:(j + 1) * N_HEADS], axis=1)


def na_bias_table(rpb, kh):
    col = np.arange(GRID_W)
    d_col = np.clip(col[None, :] - col[:, None], -(NA_WIN_W - 1), NA_WIN_W - 1) + (NA_WIN_W - 1)
    onehot = jnp.asarray(d_col[:, :, None] == np.arange(2 * NA_WIN_W - 1), F32)
    rows = jnp.stack([rpb[:, NA_WIN_H - 1 - p:NA_WIN_H - 1 - p + kh] for p in range(kh)], axis=1)
    tb = jnp.einsum("hpic,qkc->hpqik", rows, onehot, precision=lax.Precision.HIGHEST)
    col_start = np.clip(col - NA_WIN_W // 2, 0, GRID_W - NA_WIN_W)
    col_in = (col[None, :] >= col_start[:, None]) & (col[None, :] < col_start[:, None] + NA_WIN_W)
    mask = jnp.asarray(np.tile(col_in.astype(np.float32), (1, kh)))
    return tb.reshape(N_HEADS, kh, GRID_W, kh * GRID_W), mask


def neighbourhood_attention(proj, cache_k, cache_v, layer, rpb):
    B, L, _ = proj.shape
    rows = L // GRID_W
    kh = NA_WIN_H
    rps = NA_ROWS_PER_STEP
    assert rows >= kh and rows % rps == 0
    Lc = cache_k.shape[2]
    tb, mask = na_bias_table(rpb, kh)
    G = GROUP_W
    return pl.pallas_call(
        functools.partial(_na_kernel, rows=rows, kh=kh, rows_per_step=rps),
        grid=(B, rows // rps),
        in_specs=[pl.BlockSpec((1, rps * GRID_W, G), lambda b, r: (b, r, 0)),
                  pl.BlockSpec((1, L, G), lambda b, r: (b, 0, 1)),
                  pl.BlockSpec((1, L, G), lambda b, r: (b, 0, 2)),
                  pl.BlockSpec((1, 1, Lc, G), lambda b, r: (b, layer, 0, 0)),
                  pl.BlockSpec((1, 1, Lc, G), lambda b, r: (b, layer, 0, 0)),
                  pl.BlockSpec((N_HEADS, kh, GRID_W, kh * GRID_W), lambda b, r: (0, 0, 0, 0)),
                  pl.BlockSpec((GRID_W, kh * GRID_W), lambda b, r: (0, 0))],
        out_specs=pl.BlockSpec((1, rps * GRID_W, G), lambda b, r: (b, r, 0)),
        out_shape=jax.ShapeDtypeStruct((B, L, G), F32),
        compiler_params=pltpu.CompilerParams(
            dimension_semantics=("parallel", "arbitrary"), vmem_limit_bytes=VMEM_LIMIT_BYTES),
        name="na_attention",
    )(proj, proj, proj, cache_k, cache_v, tb, mask)


def _ctx_attn_kernel(q_ref, k_ref, v_ref, o_ref):
    q = (q_ref[0] * (HEAD_DIM ** -0.5)).astype(BF16)
    k = k_ref[0].astype(BF16)
    v = v_ref[0].astype(BF16)
    def head(h):
        sl = slice(h * HEAD_DIM, (h + 1) * HEAD_DIM)
        s = _dot(q[:, sl], k[:, sl], _NT)
        yield
        return (yield from _softmax_pv([s], [v[:, sl]]))

    o_ref[0] = jnp.concatenate(_interleave([head(h) for h in range(N_HEADS)]), axis=1)


def context_attention(proj):
    B, L, _ = proj.shape
    G = GROUP_W
    return pl.pallas_call(
        _ctx_attn_kernel,
        grid=(B,),
        in_specs=[pl.BlockSpec((1, L, G), lambda b: (b, 0, 0)),
                  pl.BlockSpec((1, L, G), lambda b: (b, 0, 1)),
                  pl.BlockSpec((1, L, G), lambda b: (b, 0, 2))],
        out_specs=pl.BlockSpec((1, L, G), lambda b: (b, 0, 0)),
        out_shape=jax.ShapeDtypeStruct((B, L, G), F32),
        compiler_params=pltpu.CompilerParams(dimension_semantics=("parallel",), vmem_limit_bytes=VMEM_LIMIT_BYTES),
        name="ctx_attention",
    )(proj, proj, proj)


def rms_norm(x, g):
    return x * lax.rsqrt(jnp.mean(x * x, axis=-1, keepdims=True) + EPS) * g


def layer_norm(x, g, b):
    xc = x - jnp.mean(x, axis=-1, keepdims=True)
    return xc * lax.rsqrt(jnp.mean(xc * xc, axis=-1, keepdims=True) + EPS) * g + b


def head_norm(x, g, eps):
    xc = x - jnp.mean(x, axis=-1, keepdims=True)
    y = xc * lax.rsqrt(jnp.mean(xc * xc, axis=-1, keepdims=True) + eps)
    return y.reshape(x.shape[0], x.shape[1], -1) * g


def dwconv(x, w):
    k, ch = w.shape
    return lax.conv_general_dilated(x, w[:, None, :], window_strides=(1,), padding=[(k // 2, k // 2)],
                                    dimension_numbers=("NWC", "WIO", "NWC"), feature_group_count=ch)


def token_mixers(x, mod, rows_per_mod, p, cache):
    B, L, _ = x.shape
    latent = cache is not None
    x2d = x.reshape(B * L, D_MODEL)
    proj2d = norm_matmul(x2d, mod, rows_per_mod, p["norm_g"], p["w_in"])
    proj = proj2d.reshape(B, L, -1)
    a_k = proj[..., GROUP_W:2 * GROUP_W].reshape(B, L, N_HEADS, HEAD_DIM)
    a_v = proj[..., 2 * GROUP_W:3 * GROUP_W].reshape(B, L, N_HEADS, HEAD_DIM)
    if latent:
        o_a = neighbourhood_attention(proj, cache["na_k"], cache["na_v"], cache["layer"], p["na_rpb"])
    else:
        o_a = context_attention(proj)

    o_b = conv_module(proj2d, L, p["conv_dw"], p["conv_ln_g"], p["conv_ln_b"])

    ret0 = cache["ret"] if latent else jnp.zeros((B, 2, N_HEADS, HEAD_DIM, HEAD_DIM), F32)
    o_cf, o_cb, ret_state = retention(proj, p["ret_decay"], ret0, rope=latent)

    r, v, kk, lw2, k2, a2, bonus, gate = rwkv_prep(proj2d, L, p)
    rwkv0 = cache["rwkv"] if latent else jnp.zeros((B, 2, N_HEADS, HEAD_DIM, HEAD_DIM), F32)
    tok = lambda t: t.reshape(t.shape[:-2] + (B, L, GROUP_W))
    o_df, o_db, st = rwkv7_scan(tok(r), tok(v), tok(kk), tok(lw2), tok(k2), tok(a2), rwkv0.transpose(1, 0, 2, 3, 4))
    rwkv_state = st.transpose(1, 0, 2, 3, 4)

    flat = lambda t: t.reshape(B * L, GROUP_W)
    x2d = mix_out(flat(o_a), o_b, flat(o_cf), flat(o_cb), proj2d, flat(o_df), flat(o_db), bonus, gate, x2d, mod,
                  rows_per_mod, p["norm_g"], p["ret_gn"], p["rwkv_gn"], p["w_out"])
    return x2d, (None if latent else (a_k, a_v, ret_state, rwkv_state))


def trunk_layer(x, mod_vec, p, cache):
    B, L, _ = x.shape
    n_mod = mod_vec.shape[0]
    rows_per_mod = B * L // n_mod
    mod = jnp.pad(mod_vec.reshape(n_mod, 6, D_MODEL), ((0, 0), (0, MOD_ROWS - 6), (0, 0)))
    x2d, state = token_mixers(x, mod, rows_per_mod, p, cache)
    x2d = ffn_block(x2d, mod, rows_per_mod, L, p["norm_g"], p["ffn_up"], p["ffn_conv"], p["ffn_down"])
    return x2d.reshape(B, L, D_MODEL), state


def kernel(x_prompt, x_sample, cache_na_k, cache_na_v, state_retention, state_rwkv, c, c_ctx, ada_w, ada_b, norm_g, w_in, w_out, na_rpb, conv_dw, conv_ln_g, conv_ln_b, ret_decay, ret_gn, rwkv_shift, rwkv_w0, rwkv_w2, rwkv_a0, rwkv_a2, rwkv_g2, rwkv_kk, rwkv_ka, rwkv_rk, rwkv_gn, ffn_up, ffn_conv, ffn_down):
    y_prompt, y_sample = x_prompt, x_sample
    new_k, new_v, new_ret, new_rwkv = [], [], [], []
    for l in range(DEPTH):
        p = {"norm_g": norm_g[l], "w_in": w_in[l].astype(BF16), "w_out": w_out[l].astype(BF16), "na_rpb": na_rpb[l],
             "conv_dw": conv_dw[l], "conv_ln_g": conv_ln_g[l], "conv_ln_b": conv_ln_b[l],
             "ret_decay": ret_decay[l], "ret_gn": ret_gn[l], "rwkv_shift": rwkv_shift[l],
             "rwkv_w0": rwkv_w0[l], "rwkv_w2": rwkv_w2[l], "rwkv_a0": rwkv_a0[l], "rwkv_a2": rwkv_a2[l],
             "rwkv_g2": rwkv_g2[l], "rwkv_kk": rwkv_kk[l], "rwkv_ka": rwkv_ka[l], "rwkv_rk": rwkv_rk[l],
             "rwkv_gn": rwkv_gn[l], "ffn_up": ffn_up[l].astype(BF16), "ffn_conv": ffn_conv[l],
             "ffn_down": ffn_down[l].astype(BF16)}
        mod_ctx = (jax.nn.silu(c_ctx) @ ada_w[l] + ada_b[l])[None]
        y_prompt, (k_l, v_l, ret_l, rwkv_l) = trunk_layer(y_prompt, mod_ctx, p, None)
        new_k.append(k_l)
        new_v.append(v_l)
        new_ret.append(ret_l)
        new_rwkv.append(rwkv_l)
        mod_lat = jax.nn.silu(c) @ ada_w[l] + ada_b[l]
        cache_l = {"na_k": cache_na_k.reshape(cache_na_k.shape[:3] + (GROUP_W,)),
                   "na_v": cache_na_v.reshape(cache_na_v.shape[:3] + (GROUP_W,)), "layer": l,
                   "ret": state_retention[:, l], "rwkv": state_rwkv[:, l]}
        y_sample, _ = trunk_layer(y_sample, mod_lat, p, cache_l)
    return (y_prompt, y_sample, jnp.stack(new_k, axis=1), jnp.stack(new_v, axis=1),
            jnp.stack(new_ret, axis=1), jnp.stack(new_rwkv, axis=1))
```

```python
import functools

import jax
import jax.numpy as jnp
import numpy as np
from jax import lax
from jax.experimental import pallas as pl
from jax.experimental.pallas import tpu as pltpu

F32 = jnp.float32
BF16 = jnp.bfloat16

D_MODEL = 1024
DEPTH = 2
GRID_W = 64
N_MIXERS = 4
GROUP_W = D_MODEL // N_MIXERS
HEAD_DIM = 64
N_HEADS = GROUP_W // HEAD_DIM
NA_WIN_H = 8
NA_WIN_W = 16
CONV_W = 31
RET_CHUNK = 128
RWKV_LORA_W = 64
RWKV_LORA_A = 64
RWKV_LORA_G = 128
D_FF = 2816
ROPE_BASE = 10000.0
EPS = 1e-6
RWKV_GN_EPS = 64e-5
NEG_INF = -1e30

RWKV_CHUNK = 64
INV_BASE = 8
RET_REQUESTS_PER_STEP = 2
RWKV_REQUESTS_PER_STEP = 4
VMEM_LIMIT_BYTES = 48 * 1024 * 1024
SUBLANES = 8
MOD_ROWS = 8
NA_ROWS_PER_STEP = 8
ROW_TILE = 512
FFN_CHUNK = 256


def _rms(x, g):
    return x * lax.rsqrt(jnp.mean(x * x, axis=-1, keepdims=True) + EPS) * g


def _resident(shape):
    return pl.BlockSpec(shape, lambda *_: (0,) * len(shape), pipeline_mode=pl.Buffered(1))


def _mod_spec(tm, rows_per_mod):
    return pl.BlockSpec((1, MOD_ROWS, D_MODEL), lambda i: ((i * tm) // rows_per_mod, 0, 0))


def _norm_mm_kernel(x_ref, mod_ref, g_ref, w_ref, o_ref):
    mod = mod_ref[0]
    h = _rms(x_ref[...], g_ref[0:1]) * (1.0 + mod[1:2]) + mod[0:1]
    o_ref[...] = jnp.dot(h.astype(BF16), w_ref[...], preferred_element_type=F32)


def norm_matmul(x, mod, rows_per_mod, norm_g, w_bf16, tm=ROW_TILE):
    m, n = x.shape[0], w_bf16.shape[1]
    tm = min(tm, rows_per_mod)
    return pl.pallas_call(
        _norm_mm_kernel,
        grid=(m // tm,),
        in_specs=[pl.BlockSpec((tm, D_MODEL), lambda i: (i, 0)), _mod_spec(tm, rows_per_mod),
                  _resident(norm_g.shape), _resident(w_bf16.shape)],
        out_specs=pl.BlockSpec((tm, n), lambda i: (i, 0)),
        out_shape=jax.ShapeDtypeStruct((m, n), F32),
        compiler_params=pltpu.CompilerParams(dimension_semantics=("parallel",), vmem_limit_bytes=VMEM_LIMIT_BYTES),
        name="norm_matmul",
    )(x, mod, norm_g, w_bf16)


def _head_sum(y):
    g = y.shape[1]
    row = lax.broadcasted_iota(jnp.int32, (g, g), 0) // HEAD_DIM
    col = lax.broadcasted_iota(jnp.int32, (g, g), 1) // HEAD_DIM
    ones = jnp.where(row == col, 1.0, 0.0).astype(BF16)
    hi = y.astype(BF16)
    lo = (y - hi.astype(F32)).astype(BF16)
    return jnp.dot(hi, ones, preferred_element_type=F32) + jnp.dot(lo, ones, preferred_element_type=F32)


def _head_norm(y, g, eps):
    yc = y - _head_sum(y) * (1.0 / HEAD_DIM)
    return yc * lax.rsqrt(_head_sum(yc * yc) * (1.0 / HEAD_DIM) + eps) * g


def _mix_out_kernel(oa_ref, ob_ref, cf_ref, cb_ref, rg_ref, df_ref, db_ref, bonus_ref, gate_ref, x_ref, mod_ref,
                    g_ref, gn_ref, w_ref, o_ref):
    G = GROUP_W
    rg = rg_ref[...]
    o_c = _head_norm(cf_ref[...] + cb_ref[...], gn_ref[0:1], EPS) * (rg * jax.nn.sigmoid(rg))
    o_d = (_head_norm(df_ref[...] + db_ref[...], gn_ref[1:2], RWKV_GN_EPS) + bonus_ref[...]) * gate_ref[...]
    m = 0.0
    for j, o in enumerate((oa_ref[...], ob_ref[...], o_c, o_d)):
        m = m + jnp.dot(o.astype(BF16), w_ref[j * G:(j + 1) * G, :], preferred_element_type=F32)
    o_ref[...] = x_ref[...] + mod_ref[0][2:3] * _rms(m, g_ref[1:2])


def mix_out(o_a, o_b, o_cf, o_cb, proj2d, o_df, o_db, bonus, gate, x, mod, rows_per_mod, norm_g, ret_gn, rwkv_gn,
            w_bf16, tm=ROW_TILE):
    m = x.shape[0]
    tm = min(tm, rows_per_mod)
    G = GROUP_W
    grp = pl.BlockSpec((tm, G), lambda i: (i, 0))
    gn = jnp.stack([ret_gn, rwkv_gn])
    return pl.pallas_call(
        _mix_out_kernel,
        grid=(m // tm,),
        in_specs=[grp, grp, grp, grp, pl.BlockSpec((tm, G), lambda i: (i, 8)), grp, grp, grp, grp,
                  pl.BlockSpec((tm, D_MODEL), lambda i: (i, 0)), _mod_spec(tm, rows_per_mod),
                  _resident(norm_g.shape), _resident(gn.shape), _resident(w_bf16.shape)],
        out_specs=pl.BlockSpec((tm, D_MODEL), lambda i: (i, 0)),
        out_shape=jax.ShapeDtypeStruct((m, D_MODEL), F32),
        compiler_params=pltpu.CompilerParams(dimension_semantics=("parallel",), vmem_limit_bytes=VMEM_LIMIT_BYTES),
        name="mix_out",
    )(o_a, o_b, o_cf, o_cb, proj2d, o_df, o_db, bonus, gate, x, mod, norm_g, gn, w_bf16)


def _rwkv_prep_kernel(rkv_ref, rkvp_ref, rkvn_ref, lora_ref, sh_ref, w0_ref, w2_ref, a0_ref, a2_ref, g2_ref, vec_ref,
                      r_ref, v_ref, kk_ref, lw_ref, k_ref, a_ref, bonus_ref, gate_ref, *, seq_len):
    G = GROUP_W
    x = rkv_ref[...]
    x_prev, x_next = _shift_rows(x, rkvp_ref[SUBLANES - 1:SUBLANES], rkvn_ref[0:1], *_seq_ends(x.shape[0], seq_len))
    rkv = sh_ref[0:1] * x_prev + sh_ref[1:2] * x + sh_ref[2:3] * x_next
    d_r, d_k, d_v = rkv[:, :G], rkv[:, G:2 * G], rkv[:, 2 * G:]
    lora = lora_ref[...]
    w_low = jnp.tanh(lora[:, :RWKV_LORA_W]).astype(BF16)
    a_low = lora[:, RWKV_LORA_W:RWKV_LORA_W + RWKV_LORA_A].astype(BF16)
    g_low = jax.nn.sigmoid(lora[:, RWKV_LORA_W + RWKV_LORA_A:]).astype(BF16)
    kk_w, ka, rk = vec_ref[0:1], vec_ref[1:2], vec_ref[2:3]
    k_sum = 0.0
    for d in range(2):
        z = w0_ref[d:d + 1] + jnp.dot(w_low, w2_ref[d].astype(BF16), preferred_element_type=F32)
        lw_ref[d] = -float(np.exp(-0.5)) * jax.nn.sigmoid(z)
        a = jax.nn.sigmoid(a0_ref[d:d + 1] + jnp.dot(a_low, a2_ref[d].astype(BF16), preferred_element_type=F32))
        k_d = d_k * (1.0 + (a - 1.0) * ka)
        a_ref[d] = a
        k_ref[d] = k_d
        k_sum = k_sum + k_d
    kk = d_k * kk_w
    r_ref[...] = d_r
    v_ref[...] = d_v
    kk_ref[...] = kk * lax.rsqrt(_head_sum(kk * kk) + 1e-12)
    bonus_ref[...] = _head_sum(d_r * k_sum * rk) * d_v
    gate_ref[...] = jnp.dot(g_low, g2_ref[...].astype(BF16), preferred_element_type=F32)


def rwkv_prep(proj2d, seq_len, p, tm=ROW_TILE):
    m = proj2d.shape[0]
    tm = _tile_rows(tm, seq_len)
    G = GROUP_W
    prev_spec, next_spec = _halo_specs(tm, m, 3 * G, col=3)
    vec = jnp.stack([p["rwkv_kk"], p["rwkv_ka"], p["rwkv_rk"]])
    one = pl.BlockSpec((tm, G), lambda i: (i, 0))
    two = pl.BlockSpec((2, tm, G), lambda i: (0, i, 0))
    sds1 = jax.ShapeDtypeStruct((m, G), F32)
    sds2 = jax.ShapeDtypeStruct((2, m, G), F32)
    params = (p["rwkv_shift"], p["rwkv_w0"], p["rwkv_w2"], p["rwkv_a0"], p["rwkv_a2"], p["rwkv_g2"], vec)
    return pl.pallas_call(
        functools.partial(_rwkv_prep_kernel, seq_len=seq_len),
        grid=(m // tm,),
        in_specs=[pl.BlockSpec((tm, 3 * G), lambda i: (i, 3)), prev_spec, next_spec,
                  pl.BlockSpec((tm, G), lambda i: (i, 12))] + [_resident(t.shape) for t in params],
        out_specs=[one, one, one, two, two, two, one, one],
        out_shape=[sds1, sds1, sds1, sds2, sds2, sds2, sds1, sds1],
        compiler_params=pltpu.CompilerParams(dimension_semantics=("parallel",), vmem_limit_bytes=VMEM_LIMIT_BYTES),
        name="rwkv_prep",
    )(proj2d, proj2d, proj2d, proj2d, *params)


def _halo_specs(tm, m, width, col=0):
    blocks = tm // SUBLANES
    return (pl.BlockSpec((SUBLANES, width), lambda i: (jnp.maximum(i * blocks - 1, 0), col)),
            pl.BlockSpec((SUBLANES, width), lambda i: (jnp.minimum((i + 1) * blocks, m // SUBLANES - 1), col)))


def _seq_ends(tm, seq_len):
    if seq_len >= tm:
        t = pl.program_id(0) % (seq_len // tm)
        return t == 0, t == seq_len // tm - 1
    assert tm % seq_len == 0 and seq_len & (seq_len - 1) == 0
    pos = lax.broadcasted_iota(jnp.int32, (tm, 1), 0) & (seq_len - 1)
    return pos == 0, pos == seq_len - 1


def _tile_rows(tm, seq_len):
    return tm if (seq_len % tm == 0 or tm % seq_len == 0) else seq_len


def _shift_rows(u, prev_row, next_row, first, last):
    tm = u.shape[0]
    if first.ndim:
        return (jnp.where(first, 0.0, pltpu.roll(u, 1, axis=0)), jnp.where(last, 0.0, pltpu.roll(u, tm - 1, axis=0)))
    row = lax.broadcasted_iota(jnp.int32, (tm, 1), 0)
    u_prev = jnp.where(row == 0, prev_row * jnp.where(first, 0.0, 1.0), pltpu.roll(u, 1, axis=0))
    u_next = jnp.where(row == tm - 1, next_row * jnp.where(last, 0.0, 1.0), pltpu.roll(u, tm - 1, axis=0))
    return u_prev, u_next


def _ffn_kernel(x_ref, xp_ref, xn_ref, mod_ref, g_ref, up_ref, cw_ref, down_ref, o_ref, act_scr, *, seq_len, cw):
    first, last = _seq_ends(x_ref.shape[0], seq_len)
    mod = mod_ref[0]
    g2, g3 = g_ref[2:3], g_ref[3:4]

    def pre(x):
        return (_rms(x, g2) * (1.0 + mod[4:5]) + mod[3:4]).astype(BF16)

    x = x_ref[...]
    h = pre(x)
    hh = pre(jnp.concatenate([xp_ref[...], xn_ref[...]], axis=0))
    def cols(j, half):
        return slice(half * D_FF + j * cw, half * D_FF + (j + 1) * cw)

    def up(j):
        return [(jnp.dot(h, up_ref[:, cols(j, half)], preferred_element_type=F32),
                 jnp.dot(hh, up_ref[:, cols(j, half)], preferred_element_type=F32)) for half in range(2)]

    def conv_act(j, ups):
        conv = []
        for half, (u, uh) in enumerate(ups):
            u_prev, u_next = _shift_rows(u, uh[SUBLANES - 1:SUBLANES], uh[SUBLANES:SUBLANES + 1], first, last)
            wc = cw_ref[:, cols(j, half)]
            conv.append(wc[0:1] * u_prev + wc[1:2] * u + wc[2:3] * u_next)
        return (conv[0] * jax.nn.sigmoid(conv[0]) * conv[1]).astype(BF16)

    n = D_FF // cw
    ahead = 2
    ups = [up(j) for j in range(min(ahead, n))]
    for j in range(n):
        if j + ahead < n:
            ups.append(up(j + ahead))
        act_scr[:, j * cw:(j + 1) * cw] = conv_act(j, ups[j])
    f = jnp.dot(act_scr[...], down_ref[...], preferred_element_type=F32)
    o_ref[...] = x + mod[5:6] * _rms(f, g3)


def ffn_block(x, mod, rows_per_mod, seq_len, norm_g, up_bf16, w_conv, down_bf16, tm=ROW_TILE, cw=FFN_CHUNK):
    m = x.shape[0]
    tm = _tile_rows(tm, seq_len)
    assert D_FF % cw == 0
    prev_spec, next_spec = _halo_specs(tm, m, D_MODEL)
    return pl.pallas_call(
        functools.partial(_ffn_kernel, seq_len=seq_len, cw=cw),
        grid=(m // tm,),
        in_specs=[pl.BlockSpec((tm, D_MODEL), lambda i: (i, 0)), prev_spec, next_spec, _mod_spec(tm, rows_per_mod),
                  _resident(norm_g.shape), _resident(up_bf16.shape), _resident(w_conv.shape),
                  _resident(down_bf16.shape)],
        out_specs=pl.BlockSpec((tm, D_MODEL), lambda i: (i, 0)),
        out_shape=jax.ShapeDtypeStruct((m, D_MODEL), F32),
        scratch_shapes=[pltpu.VMEM((tm, D_FF), BF16)],
        compiler_params=pltpu.CompilerParams(dimension_semantics=("parallel",), vmem_limit_bytes=VMEM_LIMIT_BYTES),
        name="ffn_block",
    )(x, x, x, mod, norm_g, up_bf16, w_conv, down_bf16)


def _conv_module_kernel(a_ref, b_ref, ap_ref, bp_ref, an_ref, bn_ref, w_ref, ln_ref, o_ref, pad_scr, sh_scr, *,
                        seq_tiles):
    i = pl.program_id(0)
    tm = a_ref.shape[0]
    halo = 2 * SUBLANES
    keep_prev = jnp.where(i % seq_tiles == 0, 0.0, 1.0)
    keep_next = jnp.where(i % seq_tiles == seq_tiles - 1, 0.0, 1.0)

    def glu(a, b):
        return a * jax.nn.sigmoid(b)

    pad_scr[0:halo] = glu(ap_ref[...], bp_ref[...]) * keep_prev
    pad_scr[halo:halo + tm] = glu(a_ref[...], b_ref[...])
    pad_scr[halo + tm:2 * halo + tm] = glu(an_ref[...], bn_ref[...]) * keep_next
    offs = [halo - CONV_W // 2 + j for j in range(CONV_W)]
    span = tm + (max(offs) // SUBLANES) * SUBLANES
    for b in range(SUBLANES):
        sh_scr[b] = pad_scr[b:b + span]
    acc = jnp.zeros((tm, GROUP_W), F32)
    for j, off in enumerate(offs):
        a, b = divmod(off, SUBLANES)
        acc = acc + w_ref[j:j + 1] * sh_scr[b, a * SUBLANES:a * SUBLANES + tm]
    xc = acc - jnp.mean(acc, axis=-1, keepdims=True)
    y = xc * lax.rsqrt(jnp.mean(xc * xc, axis=-1, keepdims=True) + EPS) * ln_ref[0:1] + ln_ref[1:2]
    o_ref[...] = y * jax.nn.sigmoid(y)


def conv_module(proj2d, seq_len, w_dw, ln_g, ln_b, tm=ROW_TILE):
    m = proj2d.shape[0]
    tm = min(tm, seq_len)
    G = GROUP_W
    halo = 2 * SUBLANES
    assert CONV_W // 2 <= halo and seq_len % tm == 0
    blocks = tm // halo

    def prev(col):
        return pl.BlockSpec((halo, G), lambda i: (jnp.maximum(i * blocks - 1, 0), col))

    def nxt(col):
        return pl.BlockSpec((halo, G), lambda i: (jnp.minimum((i + 1) * blocks, m // halo - 1), col))

    ln = jnp.stack([ln_g, ln_b])
    return pl.pallas_call(
        functools.partial(_conv_module_kernel, seq_tiles=seq_len // tm),
        grid=(m // tm,),
        in_specs=[pl.BlockSpec((tm, G), lambda i: (i, 3)), pl.BlockSpec((tm, G), lambda i: (i, 4)),
                  prev(3), prev(4), nxt(3), nxt(4), _resident(w_dw.shape), _resident(ln.shape)],
        out_specs=pl.BlockSpec((tm, G), lambda i: (i, 0)),
        out_shape=jax.ShapeDtypeStruct((m, G), F32),
        scratch_shapes=[pltpu.VMEM((tm + 2 * halo, G), F32),
                        pltpu.VMEM((SUBLANES, tm + ((halo + CONV_W // 2) // SUBLANES) * SUBLANES, G), F32)],
        compiler_params=pltpu.CompilerParams(dimension_semantics=("parallel",), vmem_limit_bytes=VMEM_LIMIT_BYTES),
        name="conv_module",
    )(proj2d, proj2d, proj2d, proj2d, proj2d, proj2d, w_dw, ln)


def _dot(a, b, dims):
    return lax.dot_general(a, b, (dims, ((), ())), preferred_element_type=F32)


_NN = ((1,), (0,))
_NT = ((1,), (1,))
_TN = ((0,), (0,))


def _bdot(a, b, dims=_NN):
    return _dot(a.astype(BF16), b.astype(BF16), dims)


def _tri_inverse(n_mat, eye, row, col):
    C = n_mat.shape[0]
    nd = jnp.where((row // INV_BASE) == (col // INV_BASE), n_mat, 0.0)
    s1 = eye + nd
    p1 = _bdot(nd, nd)
    yield
    s2 = _bdot(p1, s1)
    p2 = _bdot(p1, p1)
    yield
    s2 = s1 + s2
    d = _bdot(p2, s2)
    yield
    d = s2 + d
    size = INV_BASE
    while size < C:
        inner = (row // size) == (col // size)
        outer = (row // (2 * size)) == (col // (2 * size))
        n_off = jnp.where(outer & jnp.logical_not(inner), n_mat, 0.0)
        t = _bdot(n_off, d)
        yield
        t = _bdot(d, t)
        yield
        d = d + t
        size *= 2
    return d


def _interleave(gens):
    results = [None] * len(gens)
    active = list(enumerate(gens))
    while active:
        still = []
        for i, g in active:
            try:
                next(g)
                still.append((i, g))
            except StopIteration as stop:
                results[i] = stop.value
        active = still
    return results


def _rwkv_masks(fwd, C):
    row = lax.broadcasted_iota(jnp.int32, (C, C), 0)
    col = lax.broadcasted_iota(jnp.int32, (C, C), 1)
    incl = (row >= col) if fwd else (row <= col)
    strict = (row > col) if fwd else (row < col)
    return row, col, incl, strict


def _rwkv_cumsum(fwd, lw, C):
    incl_bf = jnp.where(_rwkv_masks(fwd, C)[2], 1.0, 0.0).astype(BF16)
    l_hi = lw.astype(BF16)
    l_lo = (lw - l_hi.astype(F32)).astype(BF16)
    return _dot(incl_bf, l_hi, _NN) + _dot(incl_bf, l_lo, _NN)


def _rwkv_chain(fwd, r, v, kk, lw, k, a, cs, s0, C):
    v = v.astype(BF16)
    row, col, incl, strict = _rwkv_masks(fwd, C)
    eye = jnp.where(row == col, 1.0, 0.0)
    tot = jnp.sum(lw, axis=0, keepdims=True)

    beta = kk * a
    e_out = jnp.exp(-cs)
    e_rem = jnp.exp(tot - cs)
    at = -kk * jnp.exp(cs - lw)
    rt = r * jnp.exp(cs)
    ar = jnp.concatenate([at, rt], axis=0)

    big = _bdot(ar, jnp.concatenate([beta * e_out, k * e_out, s0], axis=0), _NT)
    yield
    big_b, big_k, x = big[:, :C], big[:, C:2 * C], big[:, 2 * C:]
    a_ab = jnp.where(strict, big_b[:C], 0.0)
    a_ak = jnp.where(strict, big_k[:C], 0.0)
    a_rb = jnp.where(incl, big_b[C:], 0.0)
    a_rk = jnp.where(incl, big_k[C:], 0.0)
    akv = _bdot(a_ak, v)
    ork = _bdot(a_rk, v)
    vk = _bdot(v, k * e_rem, _TN)
    minv = yield from _tri_inverse(a_ab, eye, row, col)

    u = _bdot(minv, x[:C] + akv)
    yield
    o = x[C:] + ork + _bdot(a_rb, u)
    return o, s0 * jnp.exp(tot) + vk + _bdot(u, beta * e_rem, _TN)


def _rwkv_kernel(rf_ref, vf_ref, kkf_ref, lwf_ref, kf_ref, af_ref, rb_ref, vb_ref, kkb_ref, lwb_ref, kb_ref, ab_ref,
                 s0_ref, of_ref, ob_ref, st_ref, t_scr, *, chunk, n_chunks):
    c = pl.program_id(1)
    nb = rf_ref.shape[0]

    @pl.when(c == 0)
    def _():
        t_scr[...] = s0_ref[...]

    chains = []
    for b in range(nb):
        fw = [ref[b] for ref in (rf_ref, vf_ref, kkf_ref)] + [ref[0, b] for ref in (lwf_ref, kf_ref, af_ref)]
        bw = [ref[b] for ref in (rb_ref, vb_ref, kkb_ref)] + [ref[0, b] for ref in (lwb_ref, kb_ref, ab_ref)]
        for d, (fwd, tiles) in enumerate(((True, fw), (False, bw))):
            tiles = tiles + [_rwkv_cumsum(fwd, tiles[3], chunk)]
            for h in range(N_HEADS):
                sl = slice(h * HEAD_DIM, (h + 1) * HEAD_DIM)
                chains.append(_rwkv_chain(fwd, *[t[:, sl] for t in tiles], t_scr[d, b, h], chunk))
    res = _interleave(chains)
    for b in range(nb):
        for d, o_ref in enumerate((of_ref, ob_ref)):
            first = (b * 2 + d) * N_HEADS
            o_ref[b] = jnp.concatenate([o for o, _ in res[first:first + N_HEADS]], axis=1)
            for h in range(N_HEADS):
                t_scr[d, b, h] = res[first + h][1]

    @pl.when(c == n_chunks - 1)
    def _():
        st_ref[...] = t_scr[...]


def rwkv7_scan(r, v, kk, lw, k, a, s0, chunk=RWKV_CHUNK, nb=RWKV_REQUESTS_PER_STEP):
    B, L, G = r.shape
    n = L // chunk
    assert L % chunk == 0 and B % nb == 0
    sh_f = pl.BlockSpec((nb, chunk, G), lambda b, c: (b, c, 0))
    sh_b = pl.BlockSpec((nb, chunk, G), lambda b, c: (b, n - 1 - c, 0))
    pd_f = pl.BlockSpec((1, nb, chunk, G), lambda b, c: (0, b, c, 0))
    pd_b = pl.BlockSpec((1, nb, chunk, G), lambda b, c: (1, b, n - 1 - c, 0))
    state = pl.BlockSpec((2, nb, N_HEADS, HEAD_DIM, HEAD_DIM), lambda b, c: (0, b, 0, 0, 0))
    return pl.pallas_call(
        functools.partial(_rwkv_kernel, chunk=chunk, n_chunks=n),
        grid=(B // nb, n),
        in_specs=[sh_f, sh_f, sh_f, pd_f, pd_f, pd_f, sh_b, sh_b, sh_b, pd_b, pd_b, pd_b, state],
        out_specs=[sh_f, sh_b, state],
        out_shape=[jax.ShapeDtypeStruct((B, L, G), F32), jax.ShapeDtypeStruct((B, L, G), F32),
                   jax.ShapeDtypeStruct((2, B, N_HEADS, HEAD_DIM, HEAD_DIM), F32)],
        scratch_shapes=[pltpu.VMEM((2, nb, N_HEADS, HEAD_DIM, HEAD_DIM), F32)],
        compiler_params=pltpu.CompilerParams(
            dimension_semantics=("parallel", "arbitrary"), vmem_limit_bytes=VMEM_LIMIT_BYTES),
        name="rwkv7_scan",
    )(r, v, kk, lw, k, a, r, v, kk, lw, k, a, s0)


def _ret_chain(fwd, b, h, q, k, v, lg, o_ref, s_scr, C):
    sl = slice(h * HEAD_DIM, (h + 1) * HEAD_DIM)
    row = lax.broadcasted_iota(jnp.int32, (C, C), 0)
    col = lax.broadcasted_iota(jnp.int32, (C, C), 1)
    dist = ((row - col) if fwd else (col - row)).astype(F32)
    decay_in = jnp.where(dist >= 0, jnp.exp(jnp.maximum(dist, 0.0) * lg), 0.0)
    pos = lax.broadcasted_iota(jnp.int32, (C, 1), 0).astype(F32)
    step = pos if fwd else (C - 1.0) - pos
    decay_q = jnp.exp((step + 1.0) * lg)
    decay_k = jnp.exp((C - 1.0 - step) * lg)
    qh, kh, vh = q[:, sl], k[:, sl], v[:, sl].astype(BF16)
    d = 0 if fwd else 1
    s0 = s_scr[b, d, h]
    qk = _bdot(qh, kh, _NT)
    qs = _bdot(qh, s0)
    kv = _bdot(kh * decay_k, vh, _TN)
    yield
    ov = _bdot(qk * decay_in, vh)
    yield
    o_ref[b, :, sl] = ov + qs * decay_q
    s_scr[b, d, h] = s0 * jnp.exp(C * lg) + kv


def _rope(x, cos, sin):
    lane = lax.broadcasted_iota(jnp.int32, x.shape, 1)
    n = x.shape[1]
    swapped = jnp.where((lane % HEAD_DIM) < HEAD_DIM // 2,
                        pltpu.roll(x, n - HEAD_DIM // 2, axis=1), pltpu.roll(x, HEAD_DIM // 2, axis=1))
    return x * cos + swapped * sin


def _ret_kernel(lg_ref, qf_ref, kf_ref, vf_ref, qb_ref, kb_ref, vb_ref, cf_ref, sf_ref, cb_ref, sb_ref, s0_ref,
                of_ref, ob_ref, st_ref, s_scr, *, chunk, n_chunks, rope):
    c = pl.program_id(1)

    @pl.when(c == 0)
    def _():
        s_scr[...] = s0_ref[...]

    scale = HEAD_DIM ** -0.5
    chains = []
    for b in range(qf_ref.shape[0]):
        qf, kf, qb, kb = qf_ref[b], kf_ref[b] * scale, qb_ref[b], kb_ref[b] * scale
        if rope:
            qf, kf = _rope(qf, cf_ref[...], sf_ref[...]), _rope(kf, cf_ref[...], sf_ref[...])
            qb, kb = _rope(qb, cb_ref[...], sb_ref[...]), _rope(kb, cb_ref[...], sb_ref[...])
        vf, vb = vf_ref[b], vb_ref[b]
        chains += [_ret_chain(True, b, h, qf, kf, vf, lg_ref[0, h], of_ref, s_scr, chunk) for h in range(N_HEADS)]
        chains += [_ret_chain(False, b, h, qb, kb, vb, lg_ref[1, h], ob_ref, s_scr, chunk) for h in range(N_HEADS)]
    _interleave(chains)

    @pl.when(c == n_chunks - 1)
    def _():
        st_ref[...] = s_scr[...]


def rope_tables(length):
    t = jnp.arange(length)
    n_freq = HEAD_DIM // 4
    inv = ROPE_BASE ** (-jnp.arange(n_freq, dtype=F32) / n_freq)
    ang = jnp.concatenate([(t // GRID_W).astype(F32)[:, None] * inv,
                           (t % GRID_W).astype(F32)[:, None] * inv], axis=-1)
    cos, sin = jnp.cos(ang), jnp.sin(ang)
    return (jnp.tile(jnp.concatenate([cos, cos], axis=-1), (1, N_HEADS)),
            jnp.tile(jnp.concatenate([-sin, sin], axis=-1), (1, N_HEADS)))


def retention(proj, ret_decay, s0, rope, chunk=RET_CHUNK, nb=RET_REQUESTS_PER_STEP):
    B, L, _ = proj.shape
    G = GROUP_W
    n = L // chunk
    assert L % chunk == 0 and B % nb == 0
    log_g = jax.nn.log_sigmoid(ret_decay)
    cos, sin = rope_tables(L)
    fw = lambda j: pl.BlockSpec((nb, chunk, G), lambda b, c: (b, c, j))
    bw = lambda j: pl.BlockSpec((nb, chunk, G), lambda b, c: (b, n - 1 - c, j))
    tab_f = pl.BlockSpec((chunk, G), lambda b, c: (c, 0))
    tab_b = pl.BlockSpec((chunk, G), lambda b, c: (n - 1 - c, 0))
    state = pl.BlockSpec((nb, 2, N_HEADS, HEAD_DIM, HEAD_DIM), lambda b, c: (b, 0, 0, 0, 0))
    return pl.pallas_call(
        functools.partial(_ret_kernel, chunk=chunk, n_chunks=n, rope=rope),
        grid=(B // nb, n),
        in_specs=[pl.BlockSpec(memory_space=pltpu.SMEM), fw(5), fw(6), fw(7), bw(5), bw(6), bw(7),
                  tab_f, tab_f, tab_b, tab_b, state],
        out_specs=[pl.BlockSpec((nb, chunk, G), lambda b, c: (b, c, 0)),
                   pl.BlockSpec((nb, chunk, G), lambda b, c: (b, n - 1 - c, 0)), state],
        out_shape=[jax.ShapeDtypeStruct((B, L, G), F32), jax.ShapeDtypeStruct((B, L, G), F32),
                   jax.ShapeDtypeStruct((B, 2, N_HEADS, HEAD_DIM, HEAD_DIM), F32)],
        scratch_shapes=[pltpu.VMEM((nb, 2, N_HEADS, HEAD_DIM, HEAD_DIM), F32)],
        compiler_params=pltpu.CompilerParams(
            dimension_semantics=("parallel", "arbitrary"), vmem_limit_bytes=VMEM_LIMIT_BYTES),
        name="retention",
    )(log_g, proj, proj, proj, proj, proj, proj, cos, sin, cos, sin, s0)


def _softmax_pv(s_list, v_list):
    m = s_list[0].max(axis=-1, keepdims=True)
    for s in s_list[1:]:
        m = jnp.maximum(m, s.max(axis=-1, keepdims=True))
    yield
    den = 0.0
    acc = 0.0
    for s, v in zip(s_list, v_list):
        p = jnp.exp(s - m)
        den = den + p.sum(axis=-1, keepdims=True)
        acc = acc + _dot(p.astype(BF16), v, _NN)
    yield
    return acc / den


def _na_kernel(q_ref, k_ref, v_ref, ck_ref, cv_ref, tb_ref, mask_ref, o_ref, *, rows, kh, rows_per_step):
    ck = ck_ref[0, 0].astype(BF16)
    cv = cv_ref[0, 0].astype(BF16)
    mask = mask_ref[...] > 0.0

    def head(j, h):
        r = pl.program_id(1) * rows_per_step + j
        rs = jnp.clip(r - kh // 2, 0, rows - kh)
        start = pl.multiple_of(rs * GRID_W, GRID_W)
        sl = slice(h * HEAD_DIM, (h + 1) * HEAD_DIM)
        kwin = k_ref[0, pl.ds(start, kh * GRID_W), sl].astype(BF16)
        vwin = v_ref[0, pl.ds(start, kh * GRID_W), sl].astype(BF16)
        qh = (q_ref[0, j * GRID_W:(j + 1) * GRID_W, sl] * (HEAD_DIM ** -0.5)).astype(BF16)
        s_loc = _dot(qh, kwin, _NT)
        s_ctx = _dot(qh, ck[:, sl], _NT)
        yield
        s_loc = jnp.where(mask, s_loc + tb_ref[h, r - rs], NEG_INF)
        return (yield from _softmax_pv([s_loc, s_ctx], [vwin, cv[:, sl]]))

    outs = _interleave([head(j, h) for j in range(rows_per_step) for h in range(N_HEADS)])
    for j in range(rows_per_step):
        o_ref[0, j * GRID_W:(j + 1) * GRID_W, :] = jnp.concatenate(outs[j * N_HEADS:(j + 1) * N_HEADS], axis=1)


def na_bias_table(rpb, kh):
    col = np.arange(GRID_W)
    d_col = np.clip(col[None, :] - col[:, None], -(NA_WIN_W - 1), NA_WIN_W - 1) + (NA_WIN_W - 1)
    onehot = jnp.asarray(d_col[:, :, None] == np.arange(2 * NA_WIN_W - 1), F32)
    rows = jnp.stack([rpb[:, NA_WIN_H - 1 - p:NA_WIN_H - 1 - p + kh] for p in range(kh)], axis=1)
    tb = jnp.einsum("hpic,qkc->hpqik", rows, onehot, precision=lax.Precision.HIGHEST)
    col_start = np.clip(col - NA_WIN_W // 2, 0, GRID_W - NA_WIN_W)
    col_in = (col[None, :] >= col_start[:, None]) & (col[None, :] < col_start[:, None] + NA_WIN_W)
    mask = jnp.asarray(np.tile(col_in.astype(np.float32), (1, kh)))
    return tb.reshape(N_HEADS, kh, GRID_W, kh * GRID_W), mask


def neighbourhood_attention(proj, cache_k, cache_v, layer, rpb):
    B, L, _ = proj.shape
    rows = L // GRID_W
    kh = NA_WIN_H
    rps = NA_ROWS_PER_STEP
    assert rows >= kh and rows % rps == 0
    Lc = cache_k.shape[2]
    tb, mask = na_bias_table(rpb, kh)
    G = GROUP_W
    return pl.pallas_call(
        functools.partial(_na_kernel, rows=rows, kh=kh, rows_per_step=rps),
        grid=(B, rows // rps),
        in_specs=[pl.BlockSpec((1, rps * GRID_W, G), lambda b, r: (b, r, 0)),
                  pl.BlockSpec((1, L, G), lambda b, r: (b, 0, 1)),
                  pl.BlockSpec((1, L, G), lambda b, r: (b, 0, 2)),
                  pl.BlockSpec((1, 1, Lc, G), lambda b, r: (b, layer, 0, 0)),
                  pl.BlockSpec((1, 1, Lc, G), lambda b, r: (b, layer, 0, 0)),
                  pl.BlockSpec((N_HEADS, kh, GRID_W, kh * GRID_W), lambda b, r: (0, 0, 0, 0)),
                  pl.BlockSpec((GRID_W, kh * GRID_W), lambda b, r: (0, 0))],
        out_specs=pl.BlockSpec((1, rps * GRID_W, G), lambda b, r: (b, r, 0)),
        out_shape=jax.ShapeDtypeStruct((B, L, G), F32),
        compiler_params=pltpu.CompilerParams(
            dimension_semantics=("parallel", "arbitrary"), vmem_limit_bytes=VMEM_LIMIT_BYTES),
        name="na_attention",
    )(proj, proj, proj, cache_k, cache_v, tb, mask)


def _ctx_attn_kernel(q_ref, k_ref, v_ref, o_ref):
    q = (q_ref[0] * (HEAD_DIM ** -0.5)).astype(BF16)
    k = k_ref[0].astype(BF16)
    v = v_ref[0].astype(BF16)
    def head(h):
        sl = slice(h * HEAD_DIM, (h + 1) * HEAD_DIM)
        s = _dot(q[:, sl], k[:, sl], _NT)
        yield
        return (yield from _softmax_pv([s], [v[:, sl]]))

    o_ref[0] = jnp.concatenate(_interleave([head(h) for h in range(N_HEADS)]), axis=1)


def context_attention(proj):
    B, L, _ = proj.shape
    G = GROUP_W
    return pl.pallas_call(
        _ctx_attn_kernel,
        grid=(B,),
        in_specs=[pl.BlockSpec((1, L, G), lambda b: (b, 0, 0)),
                  pl.BlockSpec((1, L, G), lambda b: (b, 0, 1)),
                  pl.BlockSpec((1, L, G), lambda b: (b, 0, 2))],
        out_specs=pl.BlockSpec((1, L, G), lambda b: (b, 0, 0)),
        out_shape=jax.ShapeDtypeStruct((B, L, G), F32),
        compiler_params=pltpu.CompilerParams(dimension_semantics=("parallel",), vmem_limit_bytes=VMEM_LIMIT_BYTES),
        name="ctx_attention",
    )(proj, proj, proj)


def token_mixers(x, mod, rows_per_mod, p, cache):
    B, L, _ = x.shape
    latent = cache is not None
    x2d = x.reshape(B * L, D_MODEL)
    proj2d = norm_matmul(x2d, mod, rows_per_mod, p["norm_g"], p["w_in"])
    proj = proj2d.reshape(B, L, -1)
    a_k = proj[..., GROUP_W:2 * GROUP_W].reshape(B, L, N_HEADS, HEAD_DIM)
    a_v = proj[..., 2 * GROUP_W:3 * GROUP_W].reshape(B, L, N_HEADS, HEAD_DIM)
    if latent:
        o_a = neighbourhood_attention(proj, cache["na_k"], cache["na_v"], cache["layer"], p["na_rpb"])
    else:
        o_a = context_attention(proj)

    o_b = conv_module(proj2d, L, p["conv_dw"], p["conv_ln_g"], p["conv_ln_b"])

    ret0 = cache["ret"] if latent else jnp.zeros((B, 2, N_HEADS, HEAD_DIM, HEAD_DIM), F32)
    o_cf, o_cb, ret_state = retention(proj, p["ret_decay"], ret0, rope=latent)

    r, v, kk, lw2, k2, a2, bonus, gate = rwkv_prep(proj2d, L, p)
    rwkv0 = cache["rwkv"] if latent else jnp.zeros((B, 2, N_HEADS, HEAD_DIM, HEAD_DIM), F32)
    tok = lambda t: t.reshape(t.shape[:-2] + (B, L, GROUP_W))
    o_df, o_db, st = rwkv7_scan(tok(r), tok(v), tok(kk), tok(lw2), tok(k2), tok(a2), rwkv0.transpose(1, 0, 2, 3, 4))
    rwkv_state = st.transpose(1, 0, 2, 3, 4)

    flat = lambda t: t.reshape(B * L, GROUP_W)
    x2d = mix_out(flat(o_a), o_b, flat(o_cf), flat(o_cb), proj2d, flat(o_df), flat(o_db), bonus, gate, x2d, mod,
                  rows_per_mod, p["norm_g"], p["ret_gn"], p["rwkv_gn"], p["w_out"])
    return x2d, (None if latent else (a_k, a_v, ret_state, rwkv_state))


def trunk_layer(x, mod_vec, p, cache):
    B, L, _ = x.shape
    n_mod = mod_vec.shape[0]
    rows_per_mod = B * L // n_mod
    mod = jnp.pad(mod_vec.reshape(n_mod, 6, D_MODEL), ((0, 0), (0, MOD_ROWS - 6), (0, 0)))
    x2d, state = token_mixers(x, mod, rows_per_mod, p, cache)
    x2d = ffn_block(x2d, mod, rows_per_mod, L, p["norm_g"], p["ffn_up"], p["ffn_conv"], p["ffn_down"])
    return x2d.reshape(B, L, D_MODEL), state


def kernel(x_prompt, x_sample, cache_na_k, cache_na_v, state_retention, state_rwkv, c, c_ctx, ada_w, ada_b, norm_g, w_in, w_out, na_rpb, conv_dw, conv_ln_g, conv_ln_b, ret_decay, ret_gn, rwkv_shift, rwkv_w0, rwkv_w2, rwkv_a0, rwkv_a2, rwkv_g2, rwkv_kk, rwkv_ka, rwkv_rk, rwkv_gn, ffn_up, ffn_conv, ffn_down):
    y_prompt, y_sample = x_prompt, x_sample
    new_k, new_v, new_ret, new_rwkv = [], [], [], []
    for l in range(DEPTH):
        p = {"norm_g": norm_g[l], "w_in": w_in[l].astype(BF16), "w_out": w_out[l].astype(BF16), "na_rpb": na_rpb[l],
             "conv_dw": conv_dw[l], "conv_ln_g": conv_ln_g[l], "conv_ln_b": conv_ln_b[l],
             "ret_decay": ret_decay[l], "ret_gn": ret_gn[l], "rwkv_shift": rwkv_shift[l],
             "rwkv_w0": rwkv_w0[l], "rwkv_w2": rwkv_w2[l], "rwkv_a0": rwkv_a0[l], "rwkv_a2": rwkv_a2[l],
             "rwkv_g2": rwkv_g2[l], "rwkv_kk": rwkv_kk[l], "rwkv_ka": rwkv_ka[l], "rwkv_rk": rwkv_rk[l],
             "rwkv_gn": rwkv_gn[l], "ffn_up": ffn_up[l].astype(BF16), "ffn_conv": ffn_conv[l],
             "ffn_down": ffn_down[l].astype(BF16)}
        mod_all = jax.nn.silu(jnp.concatenate([c_ctx[None], c], axis=0)) @ ada_w[l] + ada_b[l]
        mod_ctx, mod_lat = mod_all[:1], mod_all[1:]
        y_prompt, (k_l, v_l, ret_l, rwkv_l) = trunk_layer(y_prompt, mod_ctx, p, None)
        new_k.append(k_l)
        new_v.append(v_l)
        new_ret.append(ret_l)
        new_rwkv.append(rwkv_l)
        cache_l = {"na_k": cache_na_k.reshape(cache_na_k.shape[:3] + (GROUP_W,)),
                   "na_v": cache_na_v.reshape(cache_na_v.shape[:3] + (GROUP_W,)), "layer": l,
                   "ret": state_retention[:, l], "rwkv": state_rwkv[:, l]}
        y_sample, _ = trunk_layer(y_sample, mod_lat, p, cache_l)
    return (y_prompt, y_sample, jnp.stack(new_k, axis=1), jnp.stack(new_v, axis=1),
            jnp.stack(new_ret, axis=1), jnp.stack(new_rwkv, axis=1))
```

```python
import functools

import jax
import jax.numpy as jnp
import numpy as np
from jax import lax
from jax.experimental import pallas as pl
from jax.experimental.pallas import tpu as pltpu

F32 = jnp.float32
BF16 = jnp.bfloat16

D_MODEL = 1024
DEPTH = 2
GRID_W = 64
N_MIXERS = 4
GROUP_W = D_MODEL // N_MIXERS
HEAD_DIM = 64
N_HEADS = GROUP_W // HEAD_DIM
NA_WIN_H = 8
NA_WIN_W = 16
CONV_W = 31
RET_CHUNK = 128
RWKV_LORA_W = 64
RWKV_LORA_A = 64
RWKV_LORA_G = 128
D_FF = 2816
ROPE_BASE = 10000.0
EPS = 1e-6
RWKV_GN_EPS = 64e-5
NEG_INF = -1e30

RWKV_CHUNK = 64
INV_BASE = 8
RET_REQUESTS_PER_STEP = 2
RWKV_REQUESTS_PER_STEP = 4
VMEM_LIMIT_BYTES = 48 * 1024 * 1024
SUBLANES = 8
MOD_ROWS = 8
NA_ROWS_PER_STEP = 8
ROW_TILE = 512
FFN_CHUNK = 256


def _rms(x, g):
    return x * lax.rsqrt(jnp.mean(x * x, axis=-1, keepdims=True) + EPS) * g


def _resident(shape):
    return pl.BlockSpec(shape, lambda *_: (0,) * len(shape), pipeline_mode=pl.Buffered(1))


def _mod_spec(tm, rows_per_mod):
    return pl.BlockSpec((1, MOD_ROWS, D_MODEL), lambda i: ((i * tm) // rows_per_mod, 0, 0))


def _norm_mm_kernel(x_ref, mod_ref, g_ref, w_ref, o_ref):
    mod = mod_ref[0]
    h = _rms(x_ref[...], g_ref[0:1]) * (1.0 + mod[1:2]) + mod[0:1]
    o_ref[...] = jnp.dot(h.astype(BF16), w_ref[...], preferred_element_type=F32)


def norm_matmul(x, mod, rows_per_mod, norm_g, w_bf16, tm=ROW_TILE):
    m, n = x.shape[0], w_bf16.shape[1]
    tm = min(tm, rows_per_mod)
    return pl.pallas_call(
        _norm_mm_kernel,
        grid=(m // tm,),
        in_specs=[pl.BlockSpec((tm, D_MODEL), lambda i: (i, 0)), _mod_spec(tm, rows_per_mod),
                  _resident(norm_g.shape), _resident(w_bf16.shape)],
        out_specs=pl.BlockSpec((tm, n), lambda i: (i, 0)),
        out_shape=jax.ShapeDtypeStruct((m, n), F32),
        compiler_params=pltpu.CompilerParams(dimension_semantics=("parallel",), vmem_limit_bytes=VMEM_LIMIT_BYTES),
        name="norm_matmul",
    )(x, mod, norm_g, w_bf16)


def _head_sum(y):
    g = y.shape[1]
    row = lax.broadcasted_iota(jnp.int32, (g, g), 0) // HEAD_DIM
    col = lax.broadcasted_iota(jnp.int32, (g, g), 1) // HEAD_DIM
    ones = jnp.where(row == col, 1.0, 0.0).astype(BF16)
    hi = y.astype(BF16)
    lo = (y - hi.astype(F32)).astype(BF16)
    return jnp.dot(hi, ones, preferred_element_type=F32) + jnp.dot(lo, ones, preferred_element_type=F32)


def _head_norm(y, g, eps):
    yc = y - _head_sum(y) * (1.0 / HEAD_DIM)
    return yc * lax.rsqrt(_head_sum(yc * yc) * (1.0 / HEAD_DIM) + eps) * g


def _mix_out_kernel(oa_ref, ob_ref, cf_ref, cb_ref, rg_ref, df_ref, db_ref, bonus_ref, gate_ref, x_ref, mod_ref,
                    g_ref, gn_ref, w_ref, o_ref):
    G = GROUP_W
    rg = rg_ref[...]
    o_c = _head_norm(cf_ref[...] + cb_ref[...], gn_ref[0:1], EPS) * (rg * jax.nn.sigmoid(rg))
    o_d = (_head_norm(df_ref[...] + db_ref[...], gn_ref[1:2], RWKV_GN_EPS) + bonus_ref[...]) * gate_ref[...]
    m = 0.0
    for j, o in enumerate((oa_ref[...], ob_ref[...], o_c, o_d)):
        m = m + jnp.dot(o.astype(BF16), w_ref[j * G:(j + 1) * G, :], preferred_element_type=F32)
    o_ref[...] = x_ref[...] + mod_ref[0][2:3] * _rms(m, g_ref[1:2])


def mix_out(o_a, o_b, o_cf, o_cb, proj2d, o_df, o_db, bonus, gate, x, mod, rows_per_mod, norm_g, ret_gn, rwkv_gn,
            w_bf16, tm=ROW_TILE):
    m = x.shape[0]
    tm = min(tm, rows_per_mod)
    G = GROUP_W
    grp = pl.BlockSpec((tm, G), lambda i: (i, 0))
    gn = jnp.stack([ret_gn, rwkv_gn])
    return pl.pallas_call(
        _mix_out_kernel,
        grid=(m // tm,),
        in_specs=[grp, grp, grp, grp, pl.BlockSpec((tm, G), lambda i: (i, 8)), grp, grp, grp, grp,
                  pl.BlockSpec((tm, D_MODEL), lambda i: (i, 0)), _mod_spec(tm, rows_per_mod),
                  _resident(norm_g.shape), _resident(gn.shape), _resident(w_bf16.shape)],
        out_specs=pl.BlockSpec((tm, D_MODEL), lambda i: (i, 0)),
        out_shape=jax.ShapeDtypeStruct((m, D_MODEL), F32),
        compiler_params=pltpu.CompilerParams(dimension_semantics=("parallel",), vmem_limit_bytes=VMEM_LIMIT_BYTES),
        name="mix_out",
    )(o_a, o_b, o_cf, o_cb, proj2d, o_df, o_db, bonus, gate, x, mod, norm_g, gn, w_bf16)


def _rwkv_prep_kernel(rkv_ref, rkvp_ref, rkvn_ref, lora_ref, sh_ref, w0_ref, w2_ref, a0_ref, a2_ref, g2_ref, vec_ref,
                      r_ref, v_ref, kk_ref, lw_ref, k_ref, a_ref, bonus_ref, gate_ref, *, seq_len):
    G = GROUP_W
    x = rkv_ref[...]
    x_prev, x_next = _shift_rows(x, rkvp_ref[SUBLANES - 1:SUBLANES], rkvn_ref[0:1], *_seq_ends(x.shape[0], seq_len))
    rkv = sh_ref[0:1] * x_prev + sh_ref[1:2] * x + sh_ref[2:3] * x_next
    d_r, d_k, d_v = rkv[:, :G], rkv[:, G:2 * G], rkv[:, 2 * G:]
    lora = lora_ref[...]
    w_low = jnp.tanh(lora[:, :RWKV_LORA_W]).astype(BF16)
    a_low = lora[:, RWKV_LORA_W:RWKV_LORA_W + RWKV_LORA_A].astype(BF16)
    g_low = jax.nn.sigmoid(lora[:, RWKV_LORA_W + RWKV_LORA_A:]).astype(BF16)
    kk_w, ka, rk = vec_ref[0:1], vec_ref[1:2], vec_ref[2:3]
    k_sum = 0.0
    for d in range(2):
        z = w0_ref[d:d + 1] + jnp.dot(w_low, w2_ref[d].astype(BF16), preferred_element_type=F32)
        lw_ref[d] = -float(np.exp(-0.5)) * jax.nn.sigmoid(z)
        a = jax.nn.sigmoid(a0_ref[d:d + 1] + jnp.dot(a_low, a2_ref[d].astype(BF16), preferred_element_type=F32))
        k_d = d_k * (1.0 + (a - 1.0) * ka)
        a_ref[d] = a
        k_ref[d] = k_d
        k_sum = k_sum + k_d
    kk = d_k * kk_w
    r_ref[...] = d_r
    v_ref[...] = d_v.astype(BF16)
    kk_ref[...] = kk * lax.rsqrt(_head_sum(kk * kk) + 1e-12)
    bonus_ref[...] = _head_sum(d_r * k_sum * rk) * d_v
    gate_ref[...] = jnp.dot(g_low, g2_ref[...].astype(BF16), preferred_element_type=F32)


def rwkv_prep(proj2d, seq_len, p, tm=ROW_TILE):
    m = proj2d.shape[0]
    tm = _tile_rows(tm, seq_len)
    G = GROUP_W
    prev_spec, next_spec = _halo_specs(tm, m, 3 * G, col=3)
    vec = jnp.stack([p["rwkv_kk"], p["rwkv_ka"], p["rwkv_rk"]])
    one = pl.BlockSpec((tm, G), lambda i: (i, 0))
    two = pl.BlockSpec((2, tm, G), lambda i: (0, i, 0))
    sds1 = jax.ShapeDtypeStruct((m, G), F32)
    sds2 = jax.ShapeDtypeStruct((2, m, G), F32)
    params = (p["rwkv_shift"], p["rwkv_w0"], p["rwkv_w2"], p["rwkv_a0"], p["rwkv_a2"], p["rwkv_g2"], vec)
    return pl.pallas_call(
        functools.partial(_rwkv_prep_kernel, seq_len=seq_len),
        grid=(m // tm,),
        in_specs=[pl.BlockSpec((tm, 3 * G), lambda i: (i, 3)), prev_spec, next_spec,
                  pl.BlockSpec((tm, G), lambda i: (i, 12))] + [_resident(t.shape) for t in params],
        out_specs=[one, one, one, two, two, two, one, one],
        out_shape=[sds1, jax.ShapeDtypeStruct((m, G), BF16), sds1, sds2, sds2, sds2, sds1, sds1],
        compiler_params=pltpu.CompilerParams(dimension_semantics=("parallel",), vmem_limit_bytes=VMEM_LIMIT_BYTES),
        name="rwkv_prep",
    )(proj2d, proj2d, proj2d, proj2d, *params)


def _halo_specs(tm, m, width, col=0):
    blocks = tm // SUBLANES
    return (pl.BlockSpec((SUBLANES, width), lambda i: (jnp.maximum(i * blocks - 1, 0), col)),
            pl.BlockSpec((SUBLANES, width), lambda i: (jnp.minimum((i + 1) * blocks, m // SUBLANES - 1), col)))


def _seq_ends(tm, seq_len):
    if seq_len >= tm:
        t = pl.program_id(0) % (seq_len // tm)
        return t == 0, t == seq_len // tm - 1
    assert tm % seq_len == 0 and seq_len & (seq_len - 1) == 0
    pos = lax.broadcasted_iota(jnp.int32, (tm, 1), 0) & (seq_len - 1)
    return pos == 0, pos == seq_len - 1


def _tile_rows(tm, seq_len):
    return tm if (seq_len % tm == 0 or tm % seq_len == 0) else seq_len


def _shift_rows(u, prev_row, next_row, first, last):
    tm = u.shape[0]
    if first.ndim:
        return (jnp.where(first, 0.0, pltpu.roll(u, 1, axis=0)), jnp.where(last, 0.0, pltpu.roll(u, tm - 1, axis=0)))
    row = lax.broadcasted_iota(jnp.int32, (tm, 1), 0)
    u_prev = jnp.where(row == 0, prev_row * jnp.where(first, 0.0, 1.0), pltpu.roll(u, 1, axis=0))
    u_next = jnp.where(row == tm - 1, next_row * jnp.where(last, 0.0, 1.0), pltpu.roll(u, tm - 1, axis=0))
    return u_prev, u_next


def _ffn_kernel(x_ref, xp_ref, xn_ref, mod_ref, g_ref, up_ref, cw_ref, down_ref, o_ref, act_scr, *, seq_len, cw):
    first, last = _seq_ends(x_ref.shape[0], seq_len)
    mod = mod_ref[0]
    g2, g3 = g_ref[2:3], g_ref[3:4]

    def pre(x):
        return (_rms(x, g2) * (1.0 + mod[4:5]) + mod[3:4]).astype(BF16)

    x = x_ref[...]
    h = pre(x)
    hh = pre(jnp.concatenate([xp_ref[...], xn_ref[...]], axis=0))
    def cols(j, half):
        return slice(half * D_FF + j * cw, half * D_FF + (j + 1) * cw)

    def up(j):
        return [(jnp.dot(h, up_ref[:, cols(j, half)], preferred_element_type=F32),
                 jnp.dot(hh, up_ref[:, cols(j, half)], preferred_element_type=F32)) for half in range(2)]

    def conv_act(j, ups):
        conv = []
        for half, (u, uh) in enumerate(ups):
            u_prev, u_next = _shift_rows(u, uh[SUBLANES - 1:SUBLANES], uh[SUBLANES:SUBLANES + 1], first, last)
            wc = cw_ref[:, cols(j, half)]
            conv.append(wc[0:1] * u_prev + wc[1:2] * u + wc[2:3] * u_next)
        return (conv[0] * jax.nn.sigmoid(conv[0]) * conv[1]).astype(BF16)

    n = D_FF // cw
    ahead = 2
    ups = [up(j) for j in range(min(ahead, n))]
    for j in range(n):
        if j + ahead < n:
            ups.append(up(j + ahead))
        act_scr[:, j * cw:(j + 1) * cw] = conv_act(j, ups[j])
    f = jnp.dot(act_scr[...], down_ref[...], preferred_element_type=F32)
    o_ref[...] = x + mod[5:6] * _rms(f, g3)


def ffn_block(x, mod, rows_per_mod, seq_len, norm_g, up_bf16, w_conv, down_bf16, tm=ROW_TILE, cw=FFN_CHUNK):
    m = x.shape[0]
    tm = _tile_rows(tm, seq_len)
    assert D_FF % cw == 0
    prev_spec, next_spec = _halo_specs(tm, m, D_MODEL)
    return pl.pallas_call(
        functools.partial(_ffn_kernel, seq_len=seq_len, cw=cw),
        grid=(m // tm,),
        in_specs=[pl.BlockSpec((tm, D_MODEL), lambda i: (i, 0)), prev_spec, next_spec, _mod_spec(tm, rows_per_mod),
                  _resident(norm_g.shape), _resident(up_bf16.shape), _resident(w_conv.shape),
                  _resident(down_bf16.shape)],
        out_specs=pl.BlockSpec((tm, D_MODEL), lambda i: (i, 0)),
        out_shape=jax.ShapeDtypeStruct((m, D_MODEL), F32),
        scratch_shapes=[pltpu.VMEM((tm, D_FF), BF16)],
        compiler_params=pltpu.CompilerParams(dimension_semantics=("parallel",), vmem_limit_bytes=VMEM_LIMIT_BYTES),
        name="ffn_block",
    )(x, x, x, mod, norm_g, up_bf16, w_conv, down_bf16)


def _conv_module_kernel(a_ref, b_ref, ap_ref, bp_ref, an_ref, bn_ref, w_ref, ln_ref, o_ref, pad_scr, sh_scr, *,
                        seq_tiles):
    i = pl.program_id(0)
    tm = a_ref.shape[0]
    halo = 2 * SUBLANES
    keep_prev = jnp.where(i % seq_tiles == 0, 0.0, 1.0)
    keep_next = jnp.where(i % seq_tiles == seq_tiles - 1, 0.0, 1.0)

    def glu(a, b):
        return a * jax.nn.sigmoid(b)

    pad_scr[0:halo] = glu(ap_ref[...], bp_ref[...]) * keep_prev
    pad_scr[halo:halo + tm] = glu(a_ref[...], b_ref[...])
    pad_scr[halo + tm:2 * halo + tm] = glu(an_ref[...], bn_ref[...]) * keep_next
    offs = [halo - CONV_W // 2 + j for j in range(CONV_W)]
    span = tm + (max(offs) // SUBLANES) * SUBLANES
    for b in range(SUBLANES):
        sh_scr[b] = pad_scr[b:b + span]
    acc = jnp.zeros((tm, GROUP_W), F32)
    for j, off in enumerate(offs):
        a, b = divmod(off, SUBLANES)
        acc = acc + w_ref[j:j + 1] * sh_scr[b, a * SUBLANES:a * SUBLANES + tm]
    xc = acc - jnp.mean(acc, axis=-1, keepdims=True)
    y = xc * lax.rsqrt(jnp.mean(xc * xc, axis=-1, keepdims=True) + EPS) * ln_ref[0:1] + ln_ref[1:2]
    o_ref[...] = (y * jax.nn.sigmoid(y)).astype(o_ref.dtype)


def conv_module(proj2d, seq_len, w_dw, ln_g, ln_b, tm=ROW_TILE):
    m = proj2d.shape[0]
    tm = min(tm, seq_len)
    G = GROUP_W
    halo = 2 * SUBLANES
    assert CONV_W // 2 <= halo and seq_len % tm == 0
    blocks = tm // halo

    def prev(col):
        return pl.BlockSpec((halo, G), lambda i: (jnp.maximum(i * blocks - 1, 0), col))

    def nxt(col):
        return pl.BlockSpec((halo, G), lambda i: (jnp.minimum((i + 1) * blocks, m // halo - 1), col))

    ln = jnp.stack([ln_g, ln_b])
    return pl.pallas_call(
        functools.partial(_conv_module_kernel, seq_tiles=seq_len // tm),
        grid=(m // tm,),
        in_specs=[pl.BlockSpec((tm, G), lambda i: (i, 3)), pl.BlockSpec((tm, G), lambda i: (i, 4)),
                  prev(3), prev(4), nxt(3), nxt(4), _resident(w_dw.shape), _resident(ln.shape)],
        out_specs=pl.BlockSpec((tm, G), lambda i: (i, 0)),
        out_shape=jax.ShapeDtypeStruct((m, G), BF16),
        scratch_shapes=[pltpu.VMEM((tm + 2 * halo, G), F32),
                        pltpu.VMEM((SUBLANES, tm + ((halo + CONV_W // 2) // SUBLANES) * SUBLANES, G), F32)],
        compiler_params=pltpu.CompilerParams(dimension_semantics=("parallel",), vmem_limit_bytes=VMEM_LIMIT_BYTES),
        name="conv_module",
    )(proj2d, proj2d, proj2d, proj2d, proj2d, proj2d, w_dw, ln)


def _dot(a, b, dims):
    return lax.dot_general(a, b, (dims, ((), ())), preferred_element_type=F32)


_NN = ((1,), (0,))
_NT = ((1,), (1,))
_TN = ((0,), (0,))


def _bdot(a, b, dims=_NN):
    return _dot(a.astype(BF16), b.astype(BF16), dims)


def _tri_inverse(n_mat, eye, row, col):
    C = n_mat.shape[0]
    nd = jnp.where((row // INV_BASE) == (col // INV_BASE), n_mat, 0.0)
    s1 = eye + nd
    p1 = _bdot(nd, nd)
    yield
    s2 = _bdot(p1, s1)
    p2 = _bdot(p1, p1)
    yield
    s2 = s1 + s2
    d = _bdot(p2, s2)
    yield
    d = s2 + d
    size = INV_BASE
    while size < C:
        inner = (row // size) == (col // size)
        outer = (row // (2 * size)) == (col // (2 * size))
        n_off = jnp.where(outer & jnp.logical_not(inner), n_mat, 0.0)
        t = _bdot(n_off, d)
        yield
        t = _bdot(d, t)
        yield
        d = d + t
        size *= 2
    return d


def _interleave(gens):
    results = [None] * len(gens)
    active = list(enumerate(gens))
    while active:
        still = []
        for i, g in active:
            try:
                next(g)
                still.append((i, g))
            except StopIteration as stop:
                results[i] = stop.value
        active = still
    return results


def _rwkv_masks(fwd, C):
    row = lax.broadcasted_iota(jnp.int32, (C, C), 0)
    col = lax.broadcasted_iota(jnp.int32, (C, C), 1)
    incl = (row >= col) if fwd else (row <= col)
    strict = (row > col) if fwd else (row < col)
    return row, col, incl, strict


def _rwkv_cumsum(fwd, lw, C):
    incl_bf = jnp.where(_rwkv_masks(fwd, C)[2], 1.0, 0.0).astype(BF16)
    l_hi = lw.astype(BF16)
    l_lo = (lw - l_hi.astype(F32)).astype(BF16)
    return _dot(incl_bf, l_hi, _NN) + _dot(incl_bf, l_lo, _NN)


def _rwkv_chain(fwd, r, v, kk, lw, k, a, cs, s0, C):
    v = v.astype(BF16)
    row, col, incl, strict = _rwkv_masks(fwd, C)
    eye = jnp.where(row == col, 1.0, 0.0)
    tot = jnp.sum(lw, axis=0, keepdims=True)

    beta = kk * a
    e_out = jnp.exp(-cs)
    e_rem = jnp.exp(tot - cs)
    at = -kk * jnp.exp(cs - lw)
    rt = r * jnp.exp(cs)
    ar = jnp.concatenate([at, rt], axis=0)

    big = _bdot(ar, jnp.concatenate([beta * e_out, k * e_out, s0], axis=0), _NT)
    yield
    big_b, big_k, x = big[:, :C], big[:, C:2 * C], big[:, 2 * C:]
    a_ab = jnp.where(strict, big_b[:C], 0.0)
    a_ak = jnp.where(strict, big_k[:C], 0.0)
    a_rb = jnp.where(incl, big_b[C:], 0.0)
    a_rk = jnp.where(incl, big_k[C:], 0.0)
    akv = _bdot(a_ak, v)
    ork = _bdot(a_rk, v)
    vk = _bdot(v, k * e_rem, _TN)
    minv = yield from _tri_inverse(a_ab, eye, row, col)

    u = _bdot(minv, x[:C] + akv)
    yield
    o = x[C:] + ork + _bdot(a_rb, u)
    return o, s0 * jnp.exp(tot) + vk + _bdot(u, beta * e_rem, _TN)


def _rwkv_kernel(rf_ref, vf_ref, kkf_ref, lwf_ref, kf_ref, af_ref, rb_ref, vb_ref, kkb_ref, lwb_ref, kb_ref, ab_ref,
                 s0_ref, of_ref, ob_ref, st_ref, t_scr, *, chunk, n_chunks):
    c = pl.program_id(1)
    nb = rf_ref.shape[0]

    @pl.when(c == 0)
    def _():
        t_scr[...] = s0_ref[...]

    chains = []
    for b in range(nb):
        fw = [ref[b] for ref in (rf_ref, vf_ref, kkf_ref)] + [ref[0, b] for ref in (lwf_ref, kf_ref, af_ref)]
        bw = [ref[b] for ref in (rb_ref, vb_ref, kkb_ref)] + [ref[0, b] for ref in (lwb_ref, kb_ref, ab_ref)]
        for d, (fwd, tiles) in enumerate(((True, fw), (False, bw))):
            tiles = tiles + [_rwkv_cumsum(fwd, tiles[3], chunk)]
            for h in range(N_HEADS):
                sl = slice(h * HEAD_DIM, (h + 1) * HEAD_DIM)
                chains.append(_rwkv_chain(fwd, *[t[:, sl] for t in tiles], t_scr[d, b, h], chunk))
    res = _interleave(chains)
    for b in range(nb):
        for d, o_ref in enumerate((of_ref, ob_ref)):
            first = (b * 2 + d) * N_HEADS
            o_ref[b] = jnp.concatenate([o for o, _ in res[first:first + N_HEADS]], axis=1)
            for h in range(N_HEADS):
                t_scr[d, b, h] = res[first + h][1]

    @pl.when(c == n_chunks - 1)
    def _():
        st_ref[...] = t_scr[...]


def rwkv7_scan(r, v, kk, lw, k, a, s0, chunk=RWKV_CHUNK, nb=RWKV_REQUESTS_PER_STEP):
    B, L, G = r.shape
    n = L // chunk
    assert L % chunk == 0 and B % nb == 0
    sh_f = pl.BlockSpec((nb, chunk, G), lambda b, c: (b, c, 0))
    sh_b = pl.BlockSpec((nb, chunk, G), lambda b, c: (b, n - 1 - c, 0))
    pd_f = pl.BlockSpec((1, nb, chunk, G), lambda b, c: (0, b, c, 0))
    pd_b = pl.BlockSpec((1, nb, chunk, G), lambda b, c: (1, b, n - 1 - c, 0))
    state = pl.BlockSpec((2, nb, N_HEADS, HEAD_DIM, HEAD_DIM), lambda b, c: (0, b, 0, 0, 0))
    return pl.pallas_call(
        functools.partial(_rwkv_kernel, chunk=chunk, n_chunks=n),
        grid=(B // nb, n),
        in_specs=[sh_f, sh_f, sh_f, pd_f, pd_f, pd_f, sh_b, sh_b, sh_b, pd_b, pd_b, pd_b, state],
        out_specs=[sh_f, sh_b, state],
        out_shape=[jax.ShapeDtypeStruct((B, L, G), F32), jax.ShapeDtypeStruct((B, L, G), F32),
                   jax.ShapeDtypeStruct((2, B, N_HEADS, HEAD_DIM, HEAD_DIM), F32)],
        scratch_shapes=[pltpu.VMEM((2, nb, N_HEADS, HEAD_DIM, HEAD_DIM), F32)],
        compiler_params=pltpu.CompilerParams(
            dimension_semantics=("parallel", "arbitrary"), vmem_limit_bytes=VMEM_LIMIT_BYTES),
        name="rwkv7_scan",
    )(r, v, kk, lw, k, a, r, v, kk, lw, k, a, s0)


def _ret_chain(fwd, b, h, q, k, v, lg, o_ref, s_scr, C):
    sl = slice(h * HEAD_DIM, (h + 1) * HEAD_DIM)
    row = lax.broadcasted_iota(jnp.int32, (C, C), 0)
    col = lax.broadcasted_iota(jnp.int32, (C, C), 1)
    dist = ((row - col) if fwd else (col - row)).astype(F32)
    decay_in = jnp.where(dist >= 0, jnp.exp(jnp.maximum(dist, 0.0) * lg), 0.0)
    pos = lax.broadcasted_iota(jnp.int32, (C, 1), 0).astype(F32)
    step = pos if fwd else (C - 1.0) - pos
    decay_q = jnp.exp((step + 1.0) * lg)
    decay_k = jnp.exp((C - 1.0 - step) * lg)
    qh, kh, vh = q[:, sl], k[:, sl], v[:, sl].astype(BF16)
    d = 0 if fwd else 1
    s0 = s_scr[b, d, h]
    qk = _bdot(qh, kh, _NT)
    qs = _bdot(qh, s0)
    kv = _bdot(kh * decay_k, vh, _TN)
    yield
    ov = _bdot(qk * decay_in, vh)
    yield
    o_ref[b, :, sl] = ov + qs * decay_q
    s_scr[b, d, h] = s0 * jnp.exp(C * lg) + kv


def _rope(x, cos, sin):
    lane = lax.broadcasted_iota(jnp.int32, x.shape, 1)
    n = x.shape[1]
    swapped = jnp.where((lane % HEAD_DIM) < HEAD_DIM // 2,
                        pltpu.roll(x, n - HEAD_DIM // 2, axis=1), pltpu.roll(x, HEAD_DIM // 2, axis=1))
    return x * cos + swapped * sin


def _ret_kernel(lg_ref, qf_ref, kf_ref, vf_ref, qb_ref, kb_ref, vb_ref, cf_ref, sf_ref, cb_ref, sb_ref, s0_ref,
                of_ref, ob_ref, st_ref, s_scr, *, chunk, n_chunks, rope):
    c = pl.program_id(1)

    @pl.when(c == 0)
    def _():
        s_scr[...] = s0_ref[...]

    scale = HEAD_DIM ** -0.5
    chains = []
    for b in range(qf_ref.shape[0]):
        qf, kf, qb, kb = qf_ref[b], kf_ref[b] * scale, qb_ref[b], kb_ref[b] * scale
        if rope:
            qf, kf = _rope(qf, cf_ref[...], sf_ref[...]), _rope(kf, cf_ref[...], sf_ref[...])
            qb, kb = _rope(qb, cb_ref[...], sb_ref[...]), _rope(kb, cb_ref[...], sb_ref[...])
        vf, vb = vf_ref[b], vb_ref[b]
        chains += [_ret_chain(True, b, h, qf, kf, vf, lg_ref[0, h], of_ref, s_scr, chunk) for h in range(N_HEADS)]
        chains += [_ret_chain(False, b, h, qb, kb, vb, lg_ref[1, h], ob_ref, s_scr, chunk) for h in range(N_HEADS)]
    _interleave(chains)

    @pl.when(c == n_chunks - 1)
    def _():
        st_ref[...] = s_scr[...]


def rope_tables(length):
    t = jnp.arange(length)
    n_freq = HEAD_DIM // 4
    inv = ROPE_BASE ** (-jnp.arange(n_freq, dtype=F32) / n_freq)
    ang = jnp.concatenate([(t // GRID_W).astype(F32)[:, None] * inv,
                           (t % GRID_W).astype(F32)[:, None] * inv], axis=-1)
    cos, sin = jnp.cos(ang), jnp.sin(ang)
    return (jnp.tile(jnp.concatenate([cos, cos], axis=-1), (1, N_HEADS)),
            jnp.tile(jnp.concatenate([-sin, sin], axis=-1), (1, N_HEADS)))


def retention(proj, ret_decay, s0, rope, chunk=RET_CHUNK, nb=RET_REQUESTS_PER_STEP):
    B, L, _ = proj.shape
    G = GROUP_W
    n = L // chunk
    assert L % chunk == 0 and B % nb == 0
    log_g = jax.nn.log_sigmoid(ret_decay)
    cos, sin = rope_tables(L)
    fw = lambda j: pl.BlockSpec((nb, chunk, G), lambda b, c: (b, c, j))
    bw = lambda j: pl.BlockSpec((nb, chunk, G), lambda b, c: (b, n - 1 - c, j))
    tab_f = pl.BlockSpec((chunk, G), lambda b, c: (c, 0))
    tab_b = pl.BlockSpec((chunk, G), lambda b, c: (n - 1 - c, 0))
    state = pl.BlockSpec((nb, 2, N_HEADS, HEAD_DIM, HEAD_DIM), lambda b, c: (b, 0, 0, 0, 0))
    return pl.pallas_call(
        functools.partial(_ret_kernel, chunk=chunk, n_chunks=n, rope=rope),
        grid=(B // nb, n),
        in_specs=[pl.BlockSpec(memory_space=pltpu.SMEM), fw(5), fw(6), fw(7), bw(5), bw(6), bw(7),
                  tab_f, tab_f, tab_b, tab_b, state],
        out_specs=[pl.BlockSpec((nb, chunk, G), lambda b, c: (b, c, 0)),
                   pl.BlockSpec((nb, chunk, G), lambda b, c: (b, n - 1 - c, 0)), state],
        out_shape=[jax.ShapeDtypeStruct((B, L, G), F32), jax.ShapeDtypeStruct((B, L, G), F32),
                   jax.ShapeDtypeStruct((B, 2, N_HEADS, HEAD_DIM, HEAD_DIM), F32)],
        scratch_shapes=[pltpu.VMEM((nb, 2, N_HEADS, HEAD_DIM, HEAD_DIM), F32)],
        compiler_params=pltpu.CompilerParams(
            dimension_semantics=("parallel", "arbitrary"), vmem_limit_bytes=VMEM_LIMIT_BYTES),
        name="retention",
    )(log_g, proj, proj, proj, proj, proj, proj, cos, sin, cos, sin, s0)


def _softmax_pv(s_list, v_list):
    m = s_list[0].max(axis=-1, keepdims=True)
    for s in s_list[1:]:
        m = jnp.maximum(m, s.max(axis=-1, keepdims=True))
    yield
    den = 0.0
    acc = 0.0
    for s, v in zip(s_list, v_list):
        p = jnp.exp(s - m)
        den = den + p.sum(axis=-1, keepdims=True)
        acc = acc + _dot(p.astype(BF16), v, _NN)
    yield
    return acc / den


def _na_kernel(q_ref, k_ref, v_ref, ck_ref, cv_ref, tb_ref, mask_ref, o_ref, *, rows, kh, rows_per_step):
    ck = ck_ref[0, 0].astype(BF16)
    cv = cv_ref[0, 0].astype(BF16)
    mask = mask_ref[...] > 0.0

    def head(j, h):
        r = pl.program_id(1) * rows_per_step + j
        rs = jnp.clip(r - kh // 2, 0, rows - kh)
        start = pl.multiple_of(rs * GRID_W, GRID_W)
        sl = slice(h * HEAD_DIM, (h + 1) * HEAD_DIM)
        kwin = k_ref[0, pl.ds(start, kh * GRID_W), sl].astype(BF16)
        vwin = v_ref[0, pl.ds(start, kh * GRID_W), sl].astype(BF16)
        qh = (q_ref[0, j * GRID_W:(j + 1) * GRID_W, sl] * (HEAD_DIM ** -0.5)).astype(BF16)
        s_loc = _dot(qh, kwin, _NT)
        s_ctx = _dot(qh, ck[:, sl], _NT)
        yield
        s_loc = jnp.where(mask, s_loc + tb_ref[h, r - rs], NEG_INF)
        return (yield from _softmax_pv([s_loc, s_ctx], [vwin, cv[:, sl]]))

    outs = _interleave([head(j, h) for j in range(rows_per_step) for h in range(N_HEADS)])
    for j in range(rows_per_step):
        o_ref[0, j * GRID_W:(j + 1) * GRID_W, :] = jnp.concatenate(
            outs[j * N_HEADS:(j + 1) * N_HEADS], axis=1).astype(o_ref.dtype)


def na_bias_table(rpb, kh):
    col = np.arange(GRID_W)
    d_col = np.clip(col[None, :] - col[:, None], -(NA_WIN_W - 1), NA_WIN_W - 1) + (NA_WIN_W - 1)
    onehot = jnp.asarray(d_col[:, :, None] == np.arange(2 * NA_WIN_W - 1), F32)
    rows = jnp.stack([rpb[:, NA_WIN_H - 1 - p:NA_WIN_H - 1 - p + kh] for p in range(kh)], axis=1)
    tb = jnp.einsum("hpic,qkc->hpqik", rows, onehot, precision=lax.Precision.HIGHEST)
    col_start = np.clip(col - NA_WIN_W // 2, 0, GRID_W - NA_WIN_W)
    col_in = (col[None, :] >= col_start[:, None]) & (col[None, :] < col_start[:, None] + NA_WIN_W)
    mask = jnp.asarray(np.tile(col_in.astype(np.float32), (1, kh)))
    return tb.reshape(N_HEADS, kh, GRID_W, kh * GRID_W), mask


def neighbourhood_attention(proj, cache_k, cache_v, layer, rpb):
    B, L, _ = proj.shape
    rows = L // GRID_W
    kh = NA_WIN_H
    rps = NA_ROWS_PER_STEP
    assert rows >= kh and rows % rps == 0
    Lc = cache_k.shape[2]
    tb, mask = na_bias_table(rpb, kh)
    G = GROUP_W
    return pl.pallas_call(
        functools.partial(_na_kernel, rows=rows, kh=kh, rows_per_step=rps),
        grid=(B, rows // rps),
        in_specs=[pl.BlockSpec((1, rps * GRID_W, G), lambda b, r: (b, r, 0)),
                  pl.BlockSpec((1, L, G), lambda b, r: (b, 0, 1)),
                  pl.BlockSpec((1, L, G), lambda b, r: (b, 0, 2)),
                  pl.BlockSpec((1, 1, Lc, G), lambda b, r: (b, layer, 0, 0)),
                  pl.BlockSpec((1, 1, Lc, G), lambda b, r: (b, layer, 0, 0)),
                  pl.BlockSpec((N_HEADS, kh, GRID_W, kh * GRID_W), lambda b, r: (0, 0, 0, 0)),
                  pl.BlockSpec((GRID_W, kh * GRID_W), lambda b, r: (0, 0))],
        out_specs=pl.BlockSpec((1, rps * GRID_W, G), lambda b, r: (b, r, 0)),
        out_shape=jax.ShapeDtypeStruct((B, L, G), BF16),
        compiler_params=pltpu.CompilerParams(
            dimension_semantics=("parallel", "arbitrary"), vmem_limit_bytes=VMEM_LIMIT_BYTES),
        name="na_attention",
    )(proj, proj, proj, cache_k, cache_v, tb, mask)


def _ctx_attn_kernel(q_ref, k_ref, v_ref, o_ref):
    q = (q_ref[0] * (HEAD_DIM ** -0.5)).astype(BF16)
    k = k_ref[0].astype(BF16)
    v = v_ref[0].astype(BF16)
    def head(h):
        sl = slice(h * HEAD_DIM, (h + 1) * HEAD_DIM)
        s = _dot(q[:, sl], k[:, sl], _NT)
        yield
        return (yield from _softmax_pv([s], [v[:, sl]]))

    o_ref[0] = jnp.concatenate(_interleave([head(h) for h in range(N_HEADS)]), axis=1).astype(o_ref.dtype)


def context_attention(proj):
    B, L, _ = proj.shape
    G = GROUP_W
    return pl.pallas_call(
        _ctx_attn_kernel,
        grid=(B,),
        in_specs=[pl.BlockSpec((1, L, G), lambda b: (b, 0, 0)),
                  pl.BlockSpec((1, L, G), lambda b: (b, 0, 1)),
                  pl.BlockSpec((1, L, G), lambda b: (b, 0, 2))],
        out_specs=pl.BlockSpec((1, L, G), lambda b: (b, 0, 0)),
        out_shape=jax.ShapeDtypeStruct((B, L, G), BF16),
        compiler_params=pltpu.CompilerParams(dimension_semantics=("parallel",), vmem_limit_bytes=VMEM_LIMIT_BYTES),
        name="ctx_attention",
    )(proj, proj, proj)


def token_mixers(x, mod, rows_per_mod, p, cache):
    B, L, _ = x.shape
    latent = cache is not None
    x2d = x.reshape(B * L, D_MODEL)
    proj2d = norm_matmul(x2d, mod, rows_per_mod, p["norm_g"], p["w_in"])
    proj = proj2d.reshape(B, L, -1)
    a_k = proj[..., GROUP_W:2 * GROUP_W].reshape(B, L, N_HEADS, HEAD_DIM)
    a_v = proj[..., 2 * GROUP_W:3 * GROUP_W].reshape(B, L, N_HEADS, HEAD_DIM)
    if latent:
        o_a = neighbourhood_attention(proj, cache["na_k"], cache["na_v"], cache["layer"], p["na_rpb"])
    else:
        o_a = context_attention(proj)

    o_b = conv_module(proj2d, L, p["conv_dw"], p["conv_ln_g"], p["conv_ln_b"])

    ret0 = cache["ret"] if latent else jnp.zeros((B, 2, N_HEADS, HEAD_DIM, HEAD_DIM), F32)
    o_cf, o_cb, ret_state = retention(proj, p["ret_decay"], ret0, rope=latent)

    r, v, kk, lw2, k2, a2, bonus, gate = rwkv_prep(proj2d, L, p)
    rwkv0 = cache["rwkv"] if latent else jnp.zeros((B, 2, N_HEADS, HEAD_DIM, HEAD_DIM), F32)
    tok = lambda t: t.reshape(t.shape[:-2] + (B, L, GROUP_W))
    o_df, o_db, st = rwkv7_scan(tok(r), tok(v), tok(kk), tok(lw2), tok(k2), tok(a2), rwkv0.transpose(1, 0, 2, 3, 4))
    rwkv_state = st.transpose(1, 0, 2, 3, 4)

    flat = lambda t: t.reshape(B * L, GROUP_W)
    x2d = mix_out(flat(o_a), o_b, flat(o_cf), flat(o_cb), proj2d, flat(o_df), flat(o_db), bonus, gate, x2d, mod,
                  rows_per_mod, p["norm_g"], p["ret_gn"], p["rwkv_gn"], p["w_out"])
    return x2d, (None if latent else (a_k, a_v, ret_state, rwkv_state))


def trunk_layer(x, mod_vec, p, cache):
    B, L, _ = x.shape
    n_mod = mod_vec.shape[0]
    rows_per_mod = B * L // n_mod
    mod = jnp.pad(mod_vec.reshape(n_mod, 6, D_MODEL), ((0, 0), (0, MOD_ROWS - 6), (0, 0)))
    x2d, state = token_mixers(x, mod, rows_per_mod, p, cache)
    x2d = ffn_block(x2d, mod, rows_per_mod, L, p["norm_g"], p["ffn_up"], p["ffn_conv"], p["ffn_down"])
    return x2d.reshape(B, L, D_MODEL), state


def kernel(x_prompt, x_sample, cache_na_k, cache_na_v, state_retention, state_rwkv, c, c_ctx, ada_w, ada_b, norm_g, w_in, w_out, na_rpb, conv_dw, conv_ln_g, conv_ln_b, ret_decay, ret_gn, rwkv_shift, rwkv_w0, rwkv_w2, rwkv_a0, rwkv_a2, rwkv_g2, rwkv_kk, rwkv_ka, rwkv_rk, rwkv_gn, ffn_up, ffn_conv, ffn_down):
    y_prompt, y_sample = x_prompt, x_sample
    new_k, new_v, new_ret, new_rwkv = [], [], [], []
    for l in range(DEPTH):
        p = {"norm_g": norm_g[l], "w_in": w_in[l].astype(BF16), "w_out": w_out[l].astype(BF16), "na_rpb": na_rpb[l],
             "conv_dw": conv_dw[l], "conv_ln_g": conv_ln_g[l], "conv_ln_b": conv_ln_b[l],
             "ret_decay": ret_decay[l], "ret_gn": ret_gn[l], "rwkv_shift": rwkv_shift[l],
             "rwkv_w0": rwkv_w0[l], "rwkv_w2": rwkv_w2[l], "rwkv_a0": rwkv_a0[l], "rwkv_a2": rwkv_a2[l],
             "rwkv_g2": rwkv_g2[l], "rwkv_kk": rwkv_kk[l], "rwkv_ka": rwkv_ka[l], "rwkv_rk": rwkv_rk[l],
             "rwkv_gn": rwkv_gn[l], "ffn_up": ffn_up[l].astype(BF16), "ffn_conv": ffn_conv[l],
             "ffn_down": ffn_down[l].astype(BF16)}
        mod_all = jax.nn.silu(jnp.concatenate([c_ctx[None], c], axis=0)) @ ada_w[l] + ada_b[l]
        mod_ctx, mod_lat = mod_all[:1], mod_all[1:]
        y_prompt, (k_l, v_l, ret_l, rwkv_l) = trunk_layer(y_prompt, mod_ctx, p, None)
        new_k.append(k_l)
        new_v.append(v_l)
        new_ret.append(ret_l)
        new_rwkv.append(rwkv_l)
        cache_l = {"na_k": cache_na_k.reshape(cache_na_k.shape[:3] + (GROUP_W,)),
                   "na_v": cache_na_v.reshape(cache_na_v.shape[:3] + (GROUP_W,)), "layer": l,
                   "ret": state_retention[:, l], "rwkv": state_rwkv[:, l]}
        y_sample, _ = trunk_layer(y_sample, mod_lat, p, cache_l)
    return (y_prompt, y_sample, jnp.stack(new_k, axis=1), jnp.stack(new_v, axis=1),
            jnp.stack(new_ret, axis=1), jnp.stack(new_rwkv, axis=1))
```
